```python
import math
import jax, jax.numpy as jnp
from jax import lax
import numpy as np

D_MODEL = 1024
BATCH = 4
SEQ = 8192
DEPTH = 2

SB_HEADS = 8
SB_HEAD_DIM = 64
SB_WIDTH = SB_HEADS * SB_HEAD_DIM
Q_BLOCK = 128
S5_GROUPS = 32
S5_GROUP_CH = 16
S5_WIDTH = S5_GROUPS * S5_GROUP_CH
S5_STATE = 64
DT_MIN = 0.001
DT_MAX = 0.1
RET_HEADS = 8
RET_HEAD_DIM = 64
RET_WIDTH = RET_HEADS * RET_HEAD_DIM
RET_CHUNK = 128
ROPE_BASE = 10000.0
N_BRANCH = 3
BRANCH_WIDTH = 512
IN_COLS = 3 * SB_WIDTH + S5_WIDTH + 4 * RET_WIDTH + N_BRANCH * D_MODEL
N_GROUPS = 4
EXPERTS_PER_GROUP = 8
N_EXPERTS = N_GROUPS * EXPERTS_PER_GROUP
TOP_K = 2
EXPERT_FF = 256
EPS = 1e-6

kernel_name = "hybrid_sba_s5_retention_hmoe"


def rmsnorm(x, w):
    xf = x.astype(jnp.float32)
    y = xf * lax.rsqrt(jnp.mean(xf * xf, axis=-1, keepdims=True) + EPS)
    return (y * w.astype(jnp.float32)).astype(x.dtype)


def stick_breaking_attention(q, k, v):
    b, s, h, d = q.shape
    nblk = s // Q_BLOCK
    scale = d ** -0.5
    qb = (q * scale).reshape(b, nblk, Q_BLOCK, h, d).transpose(1, 0, 3, 2, 4)
    kt = k.transpose(0, 2, 1, 3)
    vt = v.transpose(0, 2, 1, 3)
    key_pos = jnp.arange(s)

    def block(args):
        q_blk, start = args
        z = jnp.einsum('bhqd,bhkd->bhqk', q_blk, kt).astype(jnp.float32)
        q_pos = start + jnp.arange(Q_BLOCK)
        causal = key_pos[None, :] < q_pos[:, None]
        log_beta = jax.nn.log_sigmoid(z)
        log_1m_beta = jnp.where(causal, jax.nn.log_sigmoid(-z), 0.0)
        suffix = lax.cumsum(log_1m_beta, axis=3, reverse=True) - log_1m_beta
        weights = jnp.where(causal, jnp.exp(log_beta + suffix), 0.0)
        return jnp.einsum('bhqk,bhkd->bhqd', weights.astype(vt.dtype), vt)

    starts = jnp.arange(nblk) * Q_BLOCK
    out = lax.map(block, (qb, starts))
    return out.transpose(1, 0, 3, 2, 4).reshape(b, s, h * d)


def cmul(ar, ai, br, bi):
    return ar * br - ai * bi, ar * bi + ai * br


def s5_branch(u, lam_re, lam_im, log_dt, b_re, b_im, c_re, c_im, d_skip, w_glu):
    bsz, s, _ = u.shape
    f32 = jnp.float32
    uf = u.astype(f32).reshape(bsz, s, S5_GROUPS, S5_GROUP_CH)
    dt = jnp.exp(log_dt.astype(f32))[:, None]
    lr = lam_re.astype(f32)
    li = lam_im.astype(f32)
    mag = jnp.exp(lr * dt)
    ar = mag * jnp.cos(li * dt)
    ai = mag * jnp.sin(li * dt)
    den = lr * lr + li * li
    nr = ar - 1.0
    fr = (nr * lr + ai * li) / den
    fi = (ai * lr - nr * li) / den
    br_, bi_ = cmul(fr[..., None], fi[..., None], b_re.astype(f32), b_im.astype(f32))
    bu_re = jnp.einsum('bsgc,gpc->bsgp', uf, br_)
    bu_im = jnp.einsum('bsgc,gpc->bsgp', uf, bi_)
    a_re = jnp.broadcast_to(ar, (1, s, S5_GROUPS, S5_STATE))
    a_im = jnp.broadcast_to(ai, (1, s, S5_GROUPS, S5_STATE))

    def combine(left, right):
        a1r, a1i, b1r, b1i = left
        a2r, a2i, b2r, b2i = right
        anr, ani = cmul(a2r, a2i, a1r, a1i)
        bnr, bni = cmul(a2r, a2i, b1r, b1i)
        return anr, ani, bnr + b2r, bni + b2i

    _, _, xr, xi = lax.associative_scan(combine, (a_re, a_im, bu_re, bu_im), axis=1)
    y = (jnp.einsum('bsgp,gcp->bsgc', xr, c_re.astype(f32))
         - jnp.einsum('bsgp,gcp->bsgc', xi, c_im.astype(f32))
         + d_skip.astype(f32).reshape(S5_GROUPS, S5_GROUP_CH) * uf)
    y = jax.nn.gelu(y).reshape(bsz, s, S5_WIDTH)
    val, gate = jnp.split(y @ w_glu.astype(f32), 2, axis=-1)
    return (val * jax.nn.sigmoid(gate)).astype(u.dtype)


def rotary(x, pos):
    half = x.shape[-1] // 2
    inv = ROPE_BASE ** (-jnp.arange(half, dtype=jnp.float32) / half)
    ang = pos[:, None] * inv[None, :]
    cos = jnp.cos(ang)[None, :, None, :]
    sin = jnp.sin(ang)[None, :, None, :]
    x1, x2 = x[..., :half], x[..., half:]
    return jnp.concatenate([x1 * cos - x2 * sin, x2 * cos + x1 * sin], axis=-1)


def retention_branch(q, k, v, g, norm_w):
    f32 = jnp.float32
    b, s, h, d = q.shape
    c = RET_CHUNK
    n = s // c
    pos = jnp.arange(s, dtype=f32)
    q = rotary(q.astype(f32), pos)
    k = rotary(k.astype(f32), pos) * (d ** -0.5)
    v = v.astype(f32)
    gamma = 1.0 - 2.0 ** (-5.0 - jnp.arange(h, dtype=f32))
    log_g = jnp.log(gamma)
    idx = jnp.arange(c, dtype=f32)
    diff = idx[:, None] - idx[None, :]
    decay_intra = jnp.where(diff >= 0, jnp.exp(jnp.maximum(diff, 0.0)[None] * log_g[:, None, None]), 0.0)
    q_decay = jnp.exp((idx + 1.0)[None, :] * log_g[:, None])
    k_decay = jnp.exp((c - 1.0 - idx)[None, :] * log_g[:, None])
    chunk_decay = jnp.exp(c * log_g)

    def to_chunks(t):
        return t.reshape(b, n, c, h, d).transpose(1, 0, 3, 2, 4)

    def step(state, inp):
        qi, ki, vi = inp
        scores = jnp.einsum('bhqd,bhkd->bhqk', qi, ki) * decay_intra[None]
        out = (jnp.einsum('bhqk,bhkv->bhqv', scores, vi)
               + jnp.einsum('bhqd,bhdv->bhqv', qi, state) * q_decay[None, :, :, None])
        state = (state * chunk_decay[None, :, None, None]
                 + jnp.einsum('bhkd,bhkv->bhdv', ki * k_decay[None, :, :, None], vi))
        return state, out

    state0 = jnp.zeros((b, h, d, d), f32)
    _, out = lax.scan(step, state0, (to_chunks(q), to_chunks(k), to_chunks(v)))
    out = out.transpose(1, 0, 3, 2, 4).reshape(b, s, h, d)
    mu = jnp.mean(out, axis=-1, keepdims=True)
    var = jnp.mean(jnp.square(out - mu), axis=-1, keepdims=True)
    out = ((out - mu) * lax.rsqrt(var + EPS)).reshape(b, s, h * d) * norm_w.astype(f32)
    return (jax.nn.silu(g.astype(f32)) * out).astype(g.dtype)


def hierarchical_moe(x, w_group, b_group, w_router, b_router, w_gate, w_up, w_down):
    b, s, dm = x.shape
    f32 = jnp.float32
    t = x.reshape(b * s, dm)
    group_prob = jax.nn.softmax((t @ w_group).astype(f32) + b_group.astype(f32), axis=-1)
    g_w, g_idx = lax.top_k(group_prob, 1)
    exp_logits = jnp.einsum('td,gde->tge', t, w_router).astype(f32) + b_router.astype(f32)[None]
    sel_logits = jnp.take_along_axis(exp_logits, g_idx[:, :, None], axis=1)[:, 0]
    top_logits, e_idx = lax.top_k(sel_logits, TOP_K)
    e_w = jax.nn.softmax(top_logits, axis=-1) * g_w
    expert_id = g_idx * EXPERTS_PER_GROUP + e_idx
    combine = jnp.sum(jax.nn.one_hot(expert_id, N_EXPERTS, dtype=f32) * e_w[..., None], axis=1)
    y = jnp.zeros((b * s, dm), f32)
    for e in range(N_EXPERTS):
        hdn = jax.nn.silu(t @ w_gate[e]) * (t @ w_up[e])
        y = y + combine[:, e:e + 1] * (hdn @ w_down[e]).astype(f32)
    return y.astype(x.dtype).reshape(b, s, dm)


def setup_inputs(seed: int = 0) -> dict:
    key = jax.random.key(seed)
    ks = list(jax.random.split(key, 32))
    f32 = jnp.float32

    def nrm(k, shape, scale):
        return jax.random.normal(k, shape, f32) * scale

    x = jax.random.normal(ks[0], (BATCH, SEQ, D_MODEL), f32)
    norm1_w = 1.0 + nrm(ks[1], (DEPTH, D_MODEL), 0.01)
    w_in = nrm(ks[2], (DEPTH, D_MODEL, IN_COLS), D_MODEL ** -0.5)
    s5_lambda_re = -0.5 + nrm(ks[3], (DEPTH, S5_GROUPS, S5_STATE), 0.01)
    s5_lambda_im = (math.pi * jnp.arange(S5_STATE, dtype=f32))[None, None, :] + nrm(ks[4], (DEPTH, S5_GROUPS, S5_STATE), 0.01)
    s5_log_dt = jax.random.uniform(ks[5], (DEPTH, S5_GROUPS), f32, math.log(DT_MIN), math.log(DT_MAX))
    s5_b_re = nrm(ks[6], (DEPTH, S5_GROUPS, S5_STATE, S5_GROUP_CH), (2 * S5_GROUP_CH) ** -0.5)
    s5_b_im = nrm(ks[7], (DEPTH, S5_GROUPS, S5_STATE, S5_GROUP_CH), (2 * S5_GROUP_CH) ** -0.5)
    s5_c_re = nrm(ks[8], (DEPTH, S5_GROUPS, S5_GROUP_CH, S5_STATE), S5_STATE ** -0.5)
    s5_c_im = nrm(ks[9], (DEPTH, S5_GROUPS, S5_GROUP_CH, S5_STATE), S5_STATE ** -0.5)
    s5_d = nrm(ks[10], (DEPTH, S5_WIDTH), 1.0)
    s5_w_glu = nrm(ks[11], (DEPTH, S5_WIDTH, 2 * S5_WIDTH), S5_WIDTH ** -0.5)
    ret_norm_w = 1.0 + nrm(ks[12], (DEPTH, RET_WIDTH), 0.01)
    w_branch = nrm(ks[13], (DEPTH, N_BRANCH, BRANCH_WIDTH, D_MODEL), BRANCH_WIDTH ** -0.5)
    w_out = nrm(ks[14], (DEPTH, D_MODEL, D_MODEL), D_MODEL ** -0.5)
    norm2_w = 1.0 + nrm(ks[15], (DEPTH, D_MODEL), 0.01)
    w_group = nrm(ks[16], (DEPTH, D_MODEL, N_GROUPS), D_MODEL ** -0.5)
    b_group = nrm(ks[17], (DEPTH, N_GROUPS), 0.01)
    w_router = nrm(ks[18], (DEPTH, N_GROUPS, D_MODEL, EXPERTS_PER_GROUP), D_MODEL ** -0.5)
    b_router = nrm(ks[19], (DEPTH, N_GROUPS, EXPERTS_PER_GROUP), 0.01)
    w_gate = nrm(ks[20], (DEPTH, N_EXPERTS, D_MODEL, EXPERT_FF), D_MODEL ** -0.5)
    w_up = nrm(ks[21], (DEPTH, N_EXPERTS, D_MODEL, EXPERT_FF), D_MODEL ** -0.5)
    w_down = nrm(ks[22], (DEPTH, N_EXPERTS, EXPERT_FF, D_MODEL), EXPERT_FF ** -0.5)
    final_norm_w = 1.0 + nrm(ks[23], (D_MODEL,), 0.01)
    return {"x": x, "norm1_w": norm1_w, "w_in": w_in,
            "s5_lambda_re": s5_lambda_re, "s5_lambda_im": s5_lambda_im, "s5_log_dt": s5_log_dt,
            "s5_b_re": s5_b_re, "s5_b_im": s5_b_im, "s5_c_re": s5_c_re, "s5_c_im": s5_c_im,
            "s5_d": s5_d, "s5_w_glu": s5_w_glu, "ret_norm_w": ret_norm_w,
            "w_branch": w_branch, "w_out": w_out, "norm2_w": norm2_w,
            "w_group": w_group, "b_group": b_group, "w_router": w_router, "b_router": b_router,
            "w_gate": w_gate, "w_up": w_up, "w_down": w_down, "final_norm_w": final_norm_w}


def reference(x, norm1_w, w_in, s5_lambda_re, s5_lambda_im, s5_log_dt, s5_b_re, s5_b_im, s5_c_re, s5_c_im,
              s5_d, s5_w_glu, ret_norm_w, w_branch, w_out, norm2_w, w_group, b_group, w_router, b_router,
              w_gate, w_up, w_down, final_norm_w):
    b, s, _ = x.shape
    sizes = [SB_WIDTH] * 3 + [S5_WIDTH] + [RET_WIDTH] * 4 + [N_BRANCH * D_MODEL]
    offsets = np.cumsum(sizes)[:-1].tolist()
    h = x
    for layer in range(DEPTH):
        xn = rmsnorm(h, norm1_w[layer])
        proj = xn @ w_in[layer]
        q_sb, k_sb, v_sb, u_s5, q_r, k_r, v_r, g_r, gate_logits = jnp.split(proj, offsets, axis=-1)
        y_sb = stick_breaking_attention(q_sb.reshape(b, s, SB_HEADS, SB_HEAD_DIM),
                                        k_sb.reshape(b, s, SB_HEADS, SB_HEAD_DIM),
                                        v_sb.reshape(b, s, SB_HEADS, SB_HEAD_DIM))
        y_s5 = s5_branch(u_s5, s5_lambda_re[layer], s5_lambda_im[layer], s5_log_dt[layer],
                         s5_b_re[layer], s5_b_im[layer], s5_c_re[layer], s5_c_im[layer],
                         s5_d[layer], s5_w_glu[layer])
        y_ret = retention_branch(q_r.reshape(b, s, RET_HEADS, RET_HEAD_DIM),
                                 k_r.reshape(b, s, RET_HEADS, RET_HEAD_DIM),
                                 v_r.reshape(b, s, RET_HEADS, RET_HEAD_DIM), g_r, ret_norm_w[layer])
        branches = jnp.stack([y_sb.astype(h.dtype), y_s5.astype(h.dtype), y_ret.astype(h.dtype)], axis=2)
        branch_out = jnp.einsum('bsnc,ncd->bsnd', branches, w_branch[layer])
        gates = jax.nn.sigmoid(gate_logits.astype(jnp.float32)).reshape(b, s, N_BRANCH, D_MODEL)
        merged = jnp.sum(gates * branch_out.astype(jnp.float32), axis=2).astype(h.dtype)
        h = h + merged @ w_out[layer]
        hn = rmsnorm(h, norm2_w[layer])
        h = h + hierarchical_moe(hn, w_group[layer], b_group[layer], w_router[layer], b_router[layer],
                                 w_gate[layer], w_up[layer], w_down[layer])
    return rmsnorm(h, final_norm_w)
```

```python
import functools
import math

import jax
import jax.numpy as jnp
import numpy as np
from jax import lax
from jax.experimental import pallas as pl
from jax.experimental.pallas import tpu as pltpu

F32 = jnp.float32
BF16 = jnp.bfloat16

D_MODEL = 1024
SB_HEADS = 8
HEAD_DIM = 64
BRANCH_WIDTH = 512
S5_GROUPS = 32
S5_GROUP_CH = 16
S5_STATE = 64
RET_HEADS = 8
RET_CHUNK = 128
ROPE_BASE = 10000.0
N_BRANCH = 3
N_GROUPS = 4
EXPERTS_PER_GROUP = 8
N_EXPERTS = 32
EXPERT_FF = 256
EPS = 1e-6

LANES = 128
VMEM_LIMIT = 56 * 1024 * 1024
S5_CHUNK = 16
SB_BLOCK = 256
SB_SKIP_LOG = -106.0
ROUTER_LANE0 = N_GROUPS


def _cparams(sem):
    return pltpu.CompilerParams(dimension_semantics=sem, vmem_limit_bytes=VMEM_LIMIT)


def _split_bf16(x):
    hi = x.astype(BF16)
    lo = (x - hi.astype(F32)).astype(BF16)
    return hi, lo


def _dot(a, b):
    return jnp.dot(a, b, preferred_element_type=F32)


def _dot_nt(a, b):
    return lax.dot_general(a, b, (((1,), (1,)), ((), ())), preferred_element_type=F32)


def _dot_tn(a, b):
    return lax.dot_general(a, b, (((0,), (0,)), ((), ())), preferred_element_type=F32)


def _inproj_kernel(h_ref, nw_ref, w_ref, qkv_ref, u_ref, ret_ref, gate_ref):
    x = h_ref[...]
    ms = jnp.mean(x * x, axis=-1, keepdims=True)
    xn = (x * lax.rsqrt(ms + EPS) * nw_ref[...]).astype(BF16)
    off = 0
    for ref in (qkv_ref, u_ref, ret_ref, gate_ref):
        width = ref.shape[1]
        for c in range(0, width, 512):
            ref[:, c:c + 512] = _dot(xn, w_ref[:, off + c:off + c + 512]).astype(ref.dtype)
        off += width


def _inproj(h, norm_w, w_in, tm=256):
    t = h.shape[0]
    widths = (3 * BRANCH_WIDTH, BRANCH_WIDTH, 4 * BRANCH_WIDTH, N_BRANCH * D_MODEL)
    dtypes = (BF16, F32, BF16, BF16)
    return pl.pallas_call(
        _inproj_kernel,
        grid=(t // tm,),
        in_specs=[pl.BlockSpec((tm, D_MODEL), lambda i: (i, 0)),
                  pl.BlockSpec((1, D_MODEL), lambda i: (0, 0)),
                  pl.BlockSpec(w_in.shape, lambda i: (0, 0))],
        out_specs=[pl.BlockSpec((tm, w), lambda i: (i, 0)) for w in widths],
        out_shape=[jax.ShapeDtypeStruct((t, w), d) for w, d in zip(widths, dtypes)],
        compiler_params=_cparams(("parallel",)),
        name="inproj",
    )(h, norm_w, w_in)


def _sba_kernel(q_ref, k_ref, v_ref, m_ref, o_ref, acc_ref, car_ref, *, blk):
    i = pl.program_id(2)
    q = q_ref[0]
    lane = lax.broadcasted_iota(jnp.int32, (blk, LANES), 1)
    zero = jnp.zeros_like(q)
    q_heads = (jnp.where(lane < HEAD_DIM, q, zero), jnp.where(lane >= HEAD_DIM, q, zero))
    scale = HEAD_DIM ** -0.5
    m_ext = m_ref[...]
    acc_ref[...] = jnp.zeros_like(acc_ref)
    car_ref[...] = jnp.zeros_like(car_ref)

    def visit(j, diagonal):
        start = pl.multiple_of(j * blk, blk)
        kb = k_ref[0, pl.ds(start, blk), :]
        vb = v_ref[0, pl.ds(start, blk), :]
        if diagonal:
            row = lax.broadcasted_iota(jnp.int32, (blk, blk), 0)
            col = lax.broadcasted_iota(jnp.int32, (blk, blk), 1)
            causal = col < row
        for hd in range(2):
            z = _dot_nt(q_heads[hd], kb) * scale
            sp = jnp.log(1.0 + jnp.exp(-jnp.abs(z)))
            log_beta = jnp.minimum(z, 0.0) - sp
            log_1m = jnp.minimum(-z, 0.0) - sp
            if diagonal:
                log_1m = jnp.where(causal, log_1m, 0.0)
            hi, lo = _split_bf16(log_1m)
            sfx = _dot(hi, m_ext) + _dot(lo, m_ext)
            car = car_ref[hd]
            suffix = sfx[:, :blk] + jnp.concatenate([car] * (blk // LANES), axis=1)
            w = jnp.exp(log_beta + suffix)
            if diagonal:
                w = jnp.where(causal, w, 0.0)
            acc_ref[hd] += _dot(w.astype(BF16), vb)
            car_ref[hd] = car + sfx[:, blk:]

    def mass_left():
        most = jnp.maximum(jnp.max(car_ref[0]), jnp.max(car_ref[1]))
        return (most > SB_SKIP_LOG).astype(jnp.int32)

    visit(i, True)

    def cond(c):
        j, go = c
        return jnp.logical_and(j >= 0, go > 0)

    def body(c):
        j, _ = c
        visit(j, False)
        return j - 1, mass_left()

    lax.while_loop(cond, body, (i - 1, mass_left()))
    o_ref[0] = jnp.where(lane < HEAD_DIM, acc_ref[0], acc_ref[1]).astype(o_ref.dtype)


def _sb_attention(qkv, batch, seq, blk=SB_BLOCK):
    qkv3 = qkv.reshape(batch, seq, 3 * BRANCH_WIDTH)
    pairs = BRANCH_WIDTH // LANES
    r = np.arange(blk)
    m_ext = np.concatenate([(r[:, None] > r[None, :]), np.ones((blk, LANES), bool)], axis=1)
    m_ext = jnp.asarray(m_ext, BF16)
    out = pl.pallas_call(
        functools.partial(_sba_kernel, blk=blk),
        grid=(batch, pairs, seq // blk),
        in_specs=[pl.BlockSpec((1, blk, LANES), lambda b, p, i: (b, i, p)),
                  pl.BlockSpec((1, seq, LANES), lambda b, p, i: (b, 0, pairs + p)),
                  pl.BlockSpec((1, seq, LANES), lambda b, p, i: (b, 0, 2 * pairs + p)),
                  pl.BlockSpec(m_ext.shape, lambda b, p, i: (0, 0))],
        out_specs=pl.BlockSpec((1, blk, LANES), lambda b, p, i: (b, i, p)),
        out_shape=jax.ShapeDtypeStruct((batch, seq, BRANCH_WIDTH), BF16),
        scratch_shapes=[pltpu.VMEM((2, blk, LANES), F32), pltpu.VMEM((2, blk, LANES), F32)],
        compiler_params=_cparams(("parallel", "parallel", "parallel")),
        name="sb_attention",
    )(qkv3, qkv3, qkv3, m_ext)
    return out.reshape(batch * seq, BRANCH_WIDTH)


def _s5_operators(lam_re, lam_im, log_dt, b_re, b_im, c_re, c_im, d_skip):
    L = S5_CHUNK
    g, p, c = S5_GROUPS, S5_STATE, S5_GROUP_CH
    dt = jnp.exp(log_dt.astype(F32))[:, None]
    lr = lam_re.astype(F32)
    li = lam_im.astype(F32)

    def a_pow(n):
        n = jnp.asarray(n, F32)[..., None, None]
        mag = jnp.exp(lr * dt * n)
        return mag * jnp.cos(li * dt * n), mag * jnp.sin(li * dt * n)

    ar, ai = a_pow(1.0)
    den = lr * lr + li * li
    nr = ar - 1.0
    fr = (nr * lr + ai * li) / den
    fi = (ai * lr - nr * li) / den
    bbr = fr[..., None] * b_re - fi[..., None] * b_im
    bbi = fr[..., None] * b_im + fi[..., None] * b_re
    cr = c_re.astype(F32)
    ci = c_im.astype(F32)

    pr, pi = a_pow(np.arange(L))
    car = cr[None] * pr[:, :, None, :] - ci[None] * pi[:, :, None, :]
    cai = cr[None] * pi[:, :, None, :] + ci[None] * pr[:, :, None, :]
    klag = jnp.einsum('jgop,gpi->jgoi', car, bbr) - jnp.einsum('jgop,gpi->jgoi', cai, bbi)
    s_idx = np.arange(L)[:, None]
    t_idx = np.arange(L)[None, :]
    lag = np.clip(t_idx - s_idx, 0, L - 1)
    toe = klag[lag]
    toe = jnp.where((t_idx >= s_idx)[:, :, None, None, None], toe, 0.0)
    toe = toe.transpose(2, 0, 4, 1, 3).reshape(g, L * c, L * c)

    qr, qi = a_pow(L - 1 - np.arange(L))
    p_re = qr[:, :, :, None] * bbr[None] - qi[:, :, :, None] * bbi[None]
    p_im = qr[:, :, :, None] * bbi[None] + qi[:, :, :, None] * bbr[None]
    p_re = p_re.transpose(1, 0, 3, 2).reshape(g, L * c, p)
    p_im = p_im.transpose(1, 0, 3, 2).reshape(g, L * c, p)

    er, ei = a_pow(1 + np.arange(L))
    q_re = (cr[None] * er[:, :, None, :] - ci[None] * ei[:, :, None, :])
    q_im = -(cr[None] * ei[:, :, None, :] + ci[None] * er[:, :, None, :])
    q_re = q_re.transpose(1, 3, 0, 2).reshape(g, p, L * c)
    q_im = q_im.transpose(1, 3, 0, 2).reshape(g, p, L * c)

    a_lr, a_li = a_pow(float(L))

    def pair_diag(m):
        gg, r, cc = m.shape
        m = m.reshape(gg // 2, 2, r, cc)
        z = jnp.zeros_like(m[:, 0])
        top = jnp.concatenate([m[:, 0], z], axis=2)
        bot = jnp.concatenate([z, m[:, 1]], axis=2)
        return jnp.concatenate([top, bot], axis=1)

    toe_p = pair_diag(toe).astype(BF16)
    p_p = jnp.concatenate([pair_diag(p_re), pair_diag(p_im)], axis=2).astype(BF16)
    q_p = jnp.concatenate([pair_diag(q_re), pair_diag(q_im)], axis=1).astype(BF16)
    a_re = a_lr.reshape(1, g * p)
    a_im = a_li.reshape(1, g * p)
    d_p = jnp.tile(d_skip.astype(F32).reshape(g // 2, 2, 1, c), (1, 1, L, 1)).reshape(g // 2, 1, 2 * L * c)
    return toe_p, p_p, q_p, a_re, a_im, d_p


def _s5_state_in_kernel(u_ref, p_ref, vre_ref, vim_ref):
    v = _dot(u_ref[0].astype(BF16), p_ref[0])
    vre_ref[...] = v[:, :LANES]
    vim_ref[...] = v[:, LANES:]


def _s5_scan_kernel(vre_ref, vim_ref, are_ref, aim_ref, sre_ref, sim_ref):
    n = vre_ref.shape[0]
    ar = are_ref[...]
    ai = aim_ref[...]

    def step(r, carry):
        xr, xi = carry
        sre_ref[pl.ds(r, 1), :] = xr
        sim_ref[pl.ds(r, 1), :] = xi
        vr = vre_ref[pl.ds(r, 1), :]
        vi = vim_ref[pl.ds(r, 1), :]
        return ar * xr - ai * xi + vr, ar * xi + ai * xr + vi

    z = jnp.zeros_like(ar)
    lax.fori_loop(0, n, step, (z, z))


def _s5_out_kernel(u_ref, sre_ref, sim_ref, toe_ref, q_ref, d_ref, y_ref):
    u = u_ref[0]
    q = q_ref[0]
    y = _dot(u.astype(BF16), toe_ref[0])
    y += _dot(sre_ref[...].astype(BF16), q[:LANES]) + _dot(sim_ref[...].astype(BF16), q[LANES:])
    y += d_ref[0] * u
    y_ref[0] = jax.nn.gelu(y).astype(y_ref.dtype)


def _s5_ssm(u, batch, seq, ops, nb=512):
    toe_p, p_p, q_p, a_re, a_im, d_p = ops
    t = u.shape[0]
    L = S5_CHUNK
    npair = S5_GROUPS // 2
    nc = t // L
    row = 2 * L * S5_GROUP_CH
    ncols = S5_GROUPS * S5_STATE
    u_p = u.reshape(nc, L, npair, 2, S5_GROUP_CH).transpose(2, 0, 3, 1, 4).reshape(npair, nc, row)
    nb = min(nb, nc)
    v_re, v_im = pl.pallas_call(
        _s5_state_in_kernel,
        grid=(npair, nc // nb),
        in_specs=[pl.BlockSpec((1, nb, row), lambda g, i: (g, i, 0)),
                  pl.BlockSpec((1, row, 2 * LANES), lambda g, i: (g, 0, 0))],
        out_specs=[pl.BlockSpec((nb, LANES), lambda g, i: (i, g))] * 2,
        out_shape=[jax.ShapeDtypeStruct((nc, ncols), F32)] * 2,
        compiler_params=_cparams(("parallel", "parallel")),
        name="s5_chunk_state",
    )(u_p, p_p)
    ncb = nc // batch
    cw = 512
    s_re, s_im = pl.pallas_call(
        _s5_scan_kernel,
        grid=(batch, ncols // cw),
        in_specs=[pl.BlockSpec((ncb, cw), lambda b, j: (b, j)),
                  pl.BlockSpec((ncb, cw), lambda b, j: (b, j)),
                  pl.BlockSpec((1, cw), lambda b, j: (0, j)),
                  pl.BlockSpec((1, cw), lambda b, j: (0, j))],
        out_specs=[pl.BlockSpec((ncb, cw), lambda b, j: (b, j))] * 2,
        out_shape=[jax.ShapeDtypeStruct((nc, ncols), F32)] * 2,
        compiler_params=_cparams(("parallel", "parallel")),
        name="s5_scan",
    )(v_re, v_im, a_re, a_im)
    y_p = pl.pallas_call(
        _s5_out_kernel,
        grid=(npair, nc // nb),
        in_specs=[pl.BlockSpec((1, nb, row), lambda g, i: (g, i, 0)),
                  pl.BlockSpec((nb, LANES), lambda g, i: (i, g)),
                  pl.BlockSpec((nb, LANES), lambda g, i: (i, g)),
                  pl.BlockSpec((1, row, row), lambda g, i: (g, 0, 0)),
                  pl.BlockSpec((1, 2 * LANES, row), lambda g, i: (g, 0, 0)),
                  pl.BlockSpec((1, 1, row), lambda g, i: (g, 0, 0))],
        out_specs=pl.BlockSpec((1, nb, row), lambda g, i: (g, i, 0)),
        out_shape=jax.ShapeDtypeStruct((npair, nc, row), BF16),
        compiler_params=_cparams(("parallel", "parallel")),
        name="s5_out",
    )(u_p, s_re, s_im, toe_p, q_p, d_p)
    return y_p.reshape(npair, nc, 2, L, S5_GROUP_CH).transpose(1, 3, 0, 2, 4).reshape(t, BRANCH_WIDTH)


def _ret_tables(seq):
    half = HEAD_DIM // 2
    pos = jnp.arange(seq, dtype=F32)
    inv = ROPE_BASE ** (-jnp.arange(half, dtype=F32) / half)
    ang = pos[:, None] * inv[None, :]
    cos = jnp.tile(jnp.cos(ang), (1, LANES // half))
    sin = jnp.sin(ang)
    sin = jnp.tile(jnp.concatenate([-sin, sin], axis=1), (1, LANES // HEAD_DIM))
    c = RET_CHUNK
    gamma = 1.0 - 2.0 ** (-5.0 - jnp.arange(RET_HEADS, dtype=F32))
    log_g = jnp.log(gamma)
    idx = jnp.arange(c, dtype=F32)
    diff = idx[:, None] - idx[None, :]
    intra = jnp.where(diff >= 0, jnp.exp(jnp.maximum(diff, 0.0)[None] * log_g[:, None, None]), 0.0)
    q_dec = jnp.exp((idx + 1.0)[None, :] * log_g[:, None])
    k_dec = jnp.exp((c - 1.0 - idx)[None, :] * log_g[:, None])
    ch_dec = jnp.exp(c * log_g)

    def lanes(x):
        x = x.reshape(RET_HEADS // 2, 2, c)
        return jnp.repeat(x.transpose(0, 2, 1), HEAD_DIM, axis=2)

    q_dec_l = lanes(q_dec)
    k_dec_l = lanes(k_dec)
    hp = RET_HEADS // 2
    blockdiag = np.kron(np.eye(2), np.ones((HEAD_DIM, HEAD_DIM)))
    st_dec = jnp.repeat(ch_dec.reshape(hp, 2), HEAD_DIM, axis=1)[:, :, None] * blockdiag[None]
    avg = jnp.asarray(blockdiag / HEAD_DIM, BF16)
    return cos, sin, intra, q_dec_l, k_dec_l, st_dec.astype(F32), avg


def _ret_kernel(q_ref, k_ref, v_ref, g_ref, cos_ref, sin_ref, intra_ref, qd_ref, kd_ref, sd_ref, avg_ref, nw_ref,
                o_ref, st_ref, *, nchunk):
    c = RET_CHUNK

    @pl.when(pl.program_id(2) == 0)
    def _():
        st_ref[...] = jnp.zeros_like(st_ref)

    lane = lax.broadcasted_iota(jnp.int32, (c, LANES), 1)
    first_half = (lane % HEAD_DIM) < (HEAD_DIM // 2)
    head0 = lane < HEAD_DIM
    sd = sd_ref[0]
    avg = avg_ref[...]
    same_head = avg > 0
    kscale = HEAD_DIM ** -0.5

    def rotary(x, cos, sin):
        swapped = jnp.where(first_half, pltpu.roll(x, LANES - HEAD_DIM // 2, 1), pltpu.roll(x, HEAD_DIM // 2, 1))
        return x * cos + swapped * sin

    for ci in range(nchunk):
        rows = slice(ci * c, (ci + 1) * c)
        cos = cos_ref[rows, :]
        sin = sin_ref[rows, :]
        q = rotary(q_ref[0, rows, :].astype(F32), cos, sin)
        k = rotary(k_ref[0, rows, :].astype(F32), cos, sin) * kscale
        v = v_ref[0, rows, :]
        qb = q.astype(BF16)
        kb = k.astype(BF16)
        zero = jnp.zeros_like(qb)
        out = None
        for hd in range(2):
            qh = jnp.where(head0, qb, zero) if hd == 0 else jnp.where(head0, zero, qb)
            scores = _dot_nt(qh, kb) * intra_ref[0, hd]
            oh = _dot(scores.astype(BF16), v)
            out = oh if hd == 0 else jnp.where(head0, out, oh)
        state = st_ref[...]
        out = out + _dot(qb, state.astype(BF16)) * qd_ref[0]
        kd_t = (k * kd_ref[0]).T.astype(BF16)
        st_ref[...] = state * sd + jnp.where(same_head, _dot(kd_t, v), 0.0)
        hi, lo = _split_bf16(out)
        mu = _dot(hi, avg) + _dot(lo, avg)
        dlt = out - mu
        hi, lo = _split_bf16(dlt * dlt)
        var = _dot(hi, avg) + _dot(lo, avg)
        gate = g_ref[0, rows, :].astype(F32)
        y = dlt * lax.rsqrt(var + EPS) * nw_ref[...] * (gate * jax.nn.sigmoid(gate))
        o_ref[0, rows, :] = y.astype(o_ref.dtype)


def _retention(ret, norm_w, batch, seq, tables, tq=512):
    cos, sin, intra, q_dec_l, k_dec_l, st_dec, avg = tables
    pairs = BRANCH_WIDTH // LANES
    ret3 = ret.reshape(batch, seq, 4 * BRANCH_WIDTH)
    tq = min(tq, seq)
    c = RET_CHUNK

    def col(k):
        return pl.BlockSpec((1, tq, LANES), lambda b, p, i: (b, i, k * pairs + p))

    out = pl.pallas_call(
        functools.partial(_ret_kernel, nchunk=tq // c),
        grid=(batch, pairs, seq // tq),
        in_specs=[col(0), col(1), col(2), col(3),
                  pl.BlockSpec((tq, LANES), lambda b, p, i: (i, 0)),
                  pl.BlockSpec((tq, LANES), lambda b, p, i: (i, 0)),
                  pl.BlockSpec((1, 2, c, c), lambda b, p, i: (p, 0, 0, 0)),
                  pl.BlockSpec((1, c, LANES), lambda b, p, i: (p, 0, 0)),
                  pl.BlockSpec((1, c, LANES), lambda b, p, i: (p, 0, 0)),
                  pl.BlockSpec((1, LANES, LANES), lambda b, p, i: (p, 0, 0)),
                  pl.BlockSpec((LANES, LANES), lambda b, p, i: (0, 0)),
                  pl.BlockSpec((1, LANES), lambda b, p, i: (0, p))],
        out_specs=pl.BlockSpec((1, tq, LANES), lambda b, p, i: (b, i, p)),
        out_shape=jax.ShapeDtypeStruct((batch, seq, BRANCH_WIDTH), BF16),
        scratch_shapes=[pltpu.VMEM((LANES, LANES), F32)],
        compiler_params=_cparams(("parallel", "parallel", "arbitrary")),
        name="retention",
    )(ret3, ret3, ret3, ret3, cos, sin, intra.reshape(pairs, 2, c, c), q_dec_l, k_dec_l, st_dec, avg,
      norm_w.reshape(1, BRANCH_WIDTH).astype(F32))
    return out.reshape(batch * seq, BRANCH_WIDTH)


def _route(logits):
    lane = lax.broadcasted_iota(jnp.int32, logits.shape, 1).astype(F32)
    neg = jnp.float32(-jnp.inf)
    big = jnp.float32(1 << 20)
    gl = jnp.where(lane < N_GROUPS, logits, neg)
    gmax = jnp.max(gl, axis=1, keepdims=True)
    gsum = jnp.sum(jnp.exp(gl - gmax), axis=1, keepdims=True)
    g_w = 1.0 / gsum
    g_idx = jnp.min(jnp.where(gl == gmax, lane, big), axis=1, keepdims=True)
    lo = ROUTER_LANE0 + EXPERTS_PER_GROUP * g_idx
    sl = jnp.where((lane >= lo) & (lane < lo + EXPERTS_PER_GROUP), logits, neg)
    m1 = jnp.max(sl, axis=1, keepdims=True)
    i1 = jnp.min(jnp.where(sl == m1, lane, big), axis=1, keepdims=True)
    sl2 = jnp.where(lane == i1, neg, sl)
    m2 = jnp.max(sl2, axis=1, keepdims=True)
    i2 = jnp.min(jnp.where(sl2 == m2, lane, big), axis=1, keepdims=True)
    e2 = jnp.exp(m2 - m1)
    w1 = g_w / (1.0 + e2)
    w2 = g_w * e2 / (1.0 + e2)
    return jnp.where(lane == i1, w1, 0.0) + jnp.where(lane == i2, w2, 0.0)


def _merge_kernel(ysb_ref, ys5_ref, yret_ref, gate_ref, h_ref, wglu_ref, wbr_ref, wout_ref, n2_ref, wr_hi_ref,
                  wr_lo_ref, br_ref, h_out_ref, hn_ref, comb_ref):
    glu = _dot(ys5_ref[...], wglu_ref[...])
    y_s5 = (glu[:, :BRANCH_WIDTH] * jax.nn.sigmoid(glu[:, BRANCH_WIDTH:])).astype(BF16)
    merged = None
    for n, y in enumerate((ysb_ref[...], y_s5, yret_ref[...])):
        g = jax.nn.sigmoid(gate_ref[:, n * D_MODEL:(n + 1) * D_MODEL].astype(F32))
        term = g * _dot(y, wbr_ref[n])
        merged = term if merged is None else merged + term
    h = h_ref[...] + _dot(merged.astype(BF16), wout_ref[...])
    h_out_ref[...] = h
    ms = jnp.mean(h * h, axis=-1, keepdims=True)
    hn = h * lax.rsqrt(ms + EPS) * n2_ref[...]
    hn_ref[...] = hn.astype(BF16)
    hi, lo = _split_bf16(hn)
    logits = _dot(hi, wr_hi_ref[...]) + _dot(hi, wr_lo_ref[...]) + _dot(lo, wr_hi_ref[...]) + br_ref[...]
    comb_ref[...] = _route(logits)


def _merge(y_sb, y_s5, y_ret, gates, h, w_glu, w_br, w_out, norm2_w, wr_hi, wr_lo, b_r, tm=256):
    t = h.shape[0]

    def rows(w):
        return pl.BlockSpec((tm, w), lambda i: (i, 0))

    def full(a):
        return pl.BlockSpec(a.shape, lambda i: (0,) * a.ndim)

    return pl.pallas_call(
        _merge_kernel,
        grid=(t // tm,),
        in_specs=[rows(BRANCH_WIDTH), rows(BRANCH_WIDTH), rows(BRANCH_WIDTH), rows(N_BRANCH * D_MODEL),
                  rows(D_MODEL), full(w_glu), full(w_br), full(w_out), full(norm2_w), full(wr_hi), full(wr_lo),
                  full(b_r)],
        out_specs=[rows(D_MODEL), rows(D_MODEL), rows(LANES)],
        out_shape=[jax.ShapeDtypeStruct((t, D_MODEL), F32), jax.ShapeDtypeStruct((t, D_MODEL), BF16),
                   jax.ShapeDtypeStruct((t, LANES), F32)],
        compiler_params=_cparams(("parallel",)),
        name="merge_route",
    )(y_sb, y_s5, y_ret, gates, h, w_glu, w_br, w_out, norm2_w, wr_hi, wr_lo, b_r)


def _moe_kernel(hn_ref, comb_ref, h_ref, wg_ref, wu_ref, wd_ref, o_ref):
    e = pl.program_id(1)

    @pl.when(e == 0)
    def _():
        o_ref[...] = h_ref[...]

    x = hn_ref[...]
    lane = lax.broadcasted_iota(jnp.int32, comb_ref.shape, 1)
    c_e = jnp.sum(jnp.where(lane == e + ROUTER_LANE0, comb_ref[...], 0.0), axis=1, keepdims=True)
    gate = _dot(x, wg_ref[0])
    up = _dot(x, wu_ref[0])
    hdn = gate * jax.nn.sigmoid(gate) * up
    o_ref[...] += _dot((hdn * c_e).astype(BF16), wd_ref[0])


def _moe(hn, comb, h, w_gate, w_up, w_down, tm=1024):
    t = h.shape[0]
    tm = min(tm, t)
    return pl.pallas_call(
        _moe_kernel,
        grid=(t // tm, N_EXPERTS),
        in_specs=[pl.BlockSpec((tm, D_MODEL), lambda i, e: (i, 0)),
                  pl.BlockSpec((tm, LANES), lambda i, e: (i, 0)),
                  pl.BlockSpec((tm, D_MODEL), lambda i, e: (i, 0)),
                  pl.BlockSpec((1, D_MODEL, EXPERT_FF), lambda i, e: (e, 0, 0)),
                  pl.BlockSpec((1, D_MODEL, EXPERT_FF), lambda i, e: (e, 0, 0)),
                  pl.BlockSpec((1, EXPERT_FF, D_MODEL), lambda i, e: (e, 0, 0))],
        out_specs=pl.BlockSpec((tm, D_MODEL), lambda i, e: (i, 0)),
        out_shape=jax.ShapeDtypeStruct((t, D_MODEL), F32),
        compiler_params=_cparams(("parallel", "arbitrary")),
        name="moe_experts",
    )(hn, comb, h, w_gate, w_up, w_down)


def _final_norm_kernel(h_ref, w_ref, o_ref):
    x = h_ref[...]
    ms = jnp.mean(x * x, axis=-1, keepdims=True)
    o_ref[...] = x * lax.rsqrt(ms + EPS) * w_ref[...]


def _final_norm(h, w, tm=512):
    t = h.shape[0]
    tm = min(tm, t)
    return pl.pallas_call(
        _final_norm_kernel,
        grid=(t // tm,),
        in_specs=[pl.BlockSpec((tm, D_MODEL), lambda i: (i, 0)), pl.BlockSpec((1, D_MODEL), lambda i: (0, 0))],
        out_specs=pl.BlockSpec((tm, D_MODEL), lambda i: (i, 0)),
        out_shape=jax.ShapeDtypeStruct((t, D_MODEL), F32),
        compiler_params=_cparams(("parallel",)),
        name="final_norm",
    )(h, w)


def kernel(x, norm1_w, w_in, s5_lambda_re, s5_lambda_im, s5_log_dt, s5_b_re, s5_b_im, s5_c_re, s5_c_im, s5_d,
           s5_w_glu, ret_norm_w, w_branch, w_out, norm2_w, w_group, b_group, w_router, b_router, w_gate, w_up,
           w_down, final_norm_w):
    batch, seq, _ = x.shape
    depth = w_in.shape[0]
    t = batch * seq
    h = x.reshape(t, D_MODEL).astype(F32)
    ret_tables = _ret_tables(seq)
    for layer in range(depth):
        qkv, u_s5, ret, gates = _inproj(h, norm1_w[layer].reshape(1, D_MODEL).astype(F32),
                                        w_in[layer].astype(BF16))
        y_sb = _sb_attention(qkv, batch, seq)
        s5_ops = _s5_operators(s5_lambda_re[layer], s5_lambda_im[layer], s5_log_dt[layer], s5_b_re[layer],
                               s5_b_im[layer], s5_c_re[layer], s5_c_im[layer], s5_d[layer])
        y_s5 = _s5_ssm(u_s5, batch, seq, s5_ops)
        y_ret = _retention(ret, ret_norm_w[layer], batch, seq, ret_tables)
        w_r = jnp.concatenate([w_group[layer], w_router[layer].transpose(1, 0, 2).reshape(D_MODEL, N_EXPERTS)], axis=1)
        w_r = jnp.pad(w_r.astype(F32), ((0, 0), (0, LANES - w_r.shape[1])))
        wr_hi, wr_lo = _split_bf16(w_r)
        b_r = jnp.pad(jnp.concatenate([b_group[layer], b_router[layer].reshape(-1)]).astype(F32),
                      (0, LANES - N_GROUPS - N_EXPERTS)).reshape(1, LANES)
        h, hn, comb = _merge(y_sb, y_s5, y_ret, gates, h, s5_w_glu[layer].astype(BF16), w_branch[layer].astype(BF16),
                             w_out[layer].astype(BF16), norm2_w[layer].reshape(1, D_MODEL).astype(F32), wr_hi, wr_lo,
                             b_r)
        h = _moe(hn, comb, h, w_gate[layer].astype(BF16), w_up[layer].astype(BF16), w_down[layer].astype(BF16))
    out = _final_norm(h, final_norm_w.reshape(1, D_MODEL).astype(F32))
    return out.reshape(batch, seq, D_MODEL).astype(x.dtype)
```

```python
import functools
import math

import jax
import jax.numpy as jnp
import numpy as np
from jax import lax
from jax.experimental import pallas as pl
from jax.experimental.pallas import tpu as pltpu

F32 = jnp.float32
BF16 = jnp.bfloat16

D_MODEL = 1024
SB_HEADS = 8
HEAD_DIM = 64
BRANCH_WIDTH = 512
S5_GROUPS = 32
S5_GROUP_CH = 16
S5_STATE = 64
RET_HEADS = 8
RET_CHUNK = 128
ROPE_BASE = 10000.0
N_BRANCH = 3
N_GROUPS = 4
EXPERTS_PER_GROUP = 8
N_EXPERTS = 32
EXPERT_FF = 256
EPS = 1e-6

LANES = 128
VMEM_LIMIT = 56 * 1024 * 1024
S5_CHUNK = 16
SB_BLOCK = 256
SB_SKIP_LOG = -106.0
ROUTER_LANE0 = N_GROUPS


def _cparams(sem):
    return pltpu.CompilerParams(dimension_semantics=sem, vmem_limit_bytes=VMEM_LIMIT)


def _split_bf16(x):
    hi = x.astype(BF16)
    lo = (x - hi.astype(F32)).astype(BF16)
    return hi, lo


def _dot(a, b):
    return jnp.dot(a, b, preferred_element_type=F32)


def _dot_nt(a, b):
    return lax.dot_general(a, b, (((1,), (1,)), ((), ())), preferred_element_type=F32)


def _dot_tn(a, b):
    return lax.dot_general(a, b, (((0,), (0,)), ((), ())), preferred_element_type=F32)


def _inproj_kernel(h_ref, nw_ref, w_ref, qkv_ref, u_ref, ret_ref, gate_ref):
    x = h_ref[...]
    ms = jnp.mean(x * x, axis=-1, keepdims=True)
    xn = (x * lax.rsqrt(ms + EPS) * nw_ref[...]).astype(BF16)
    off = 0
    for ref in (qkv_ref, u_ref, ret_ref, gate_ref):
        if ref is u_ref:
            u = _dot(xn, w_ref[:, off:off + BRANCH_WIDTH])
            for s in range(u_ref.shape[0]):
                u_ref[s] = u[:, s * LANES:(s + 1) * LANES]
            off += BRANCH_WIDTH
            continue
        width = ref.shape[1]
        for c in range(0, width, 512):
            ref[:, c:c + 512] = _dot(xn, w_ref[:, off + c:off + c + 512]).astype(ref.dtype)
        off += width


def _inproj(h, norm_w, w_in, tm=256):
    t = h.shape[0]
    nslab = BRANCH_WIDTH // LANES

    def rows(w):
        return pl.BlockSpec((tm, w), lambda i: (i, 0))

    return pl.pallas_call(
        _inproj_kernel,
        grid=(t // tm,),
        in_specs=[rows(D_MODEL),
                  pl.BlockSpec((1, D_MODEL), lambda i: (0, 0)),
                  pl.BlockSpec(w_in.shape, lambda i: (0, 0))],
        out_specs=[rows(3 * BRANCH_WIDTH), pl.BlockSpec((nslab, tm, LANES), lambda i: (0, i, 0)),
                   rows(4 * BRANCH_WIDTH), rows(N_BRANCH * D_MODEL)],
        out_shape=[jax.ShapeDtypeStruct((t, 3 * BRANCH_WIDTH), BF16),
                   jax.ShapeDtypeStruct((nslab, t, LANES), F32),
                   jax.ShapeDtypeStruct((t, 4 * BRANCH_WIDTH), BF16),
                   jax.ShapeDtypeStruct((t, N_BRANCH * D_MODEL), BF16)],
        compiler_params=_cparams(("parallel",)),
        name="inproj",
    )(h, norm_w, w_in)


def _sba_kernel(q_ref, k_ref, v_ref, m_ref, o_ref, acc_ref, car_ref, *, blk):
    i = pl.program_id(2)
    q = q_ref[0]
    lane = lax.broadcasted_iota(jnp.int32, (blk, LANES), 1)
    zero = jnp.zeros_like(q)
    q_heads = (jnp.where(lane < HEAD_DIM, q, zero), jnp.where(lane >= HEAD_DIM, q, zero))
    scale = HEAD_DIM ** -0.5
    m_ext = m_ref[...]
    acc_ref[...] = jnp.zeros_like(acc_ref)
    car_ref[...] = jnp.zeros_like(car_ref)

    def visit(j, diagonal):
        start = pl.multiple_of(j * blk, blk)
        kb = k_ref[0, pl.ds(start, blk), :]
        vb = v_ref[0, pl.ds(start, blk), :]
        if diagonal:
            row = lax.broadcasted_iota(jnp.int32, (blk, blk), 0)
            col = lax.broadcasted_iota(jnp.int32, (blk, blk), 1)
            causal = col < row
        for hd in range(2):
            z = _dot_nt(q_heads[hd], kb) * scale
            sp = jnp.log(1.0 + jnp.exp(-jnp.abs(z)))
            log_beta = jnp.minimum(z, 0.0) - sp
            log_1m = jnp.minimum(-z, 0.0) - sp
            if diagonal:
                log_1m = jnp.where(causal, log_1m, 0.0)
            hi, lo = _split_bf16(log_1m)
            sfx = _dot(hi, m_ext) + _dot(lo, m_ext)
            car = car_ref[hd]
            suffix = sfx[:, :blk] + jnp.concatenate([car] * (blk // LANES), axis=1)
            w = jnp.exp(log_beta + suffix)
            if diagonal:
                w = jnp.where(causal, w, 0.0)
            acc_ref[hd] += _dot(w.astype(BF16), vb)
            car_ref[hd] = car + sfx[:, blk:]

    def mass_left():
        most = jnp.maximum(jnp.max(car_ref[0]), jnp.max(car_ref[1]))
        return (most > SB_SKIP_LOG).astype(jnp.int32)

    visit(i, True)

    def cond(c):
        j, go = c
        return jnp.logical_and(j >= 0, go > 0)

    def body(c):
        j, _ = c
        visit(j, False)
        return j - 1, mass_left()

    lax.while_loop(cond, body, (i - 1, mass_left()))
    o_ref[0] = jnp.where(lane < HEAD_DIM, acc_ref[0], acc_ref[1]).astype(o_ref.dtype)


def _sb_attention(qkv, batch, seq, blk=SB_BLOCK):
    qkv3 = qkv.reshape(batch, seq, 3 * BRANCH_WIDTH)
    pairs = BRANCH_WIDTH // LANES
    r = np.arange(blk)
    m_ext = np.concatenate([(r[:, None] > r[None, :]), np.ones((blk, LANES), bool)], axis=1)
    m_ext = jnp.asarray(m_ext, BF16)
    out = pl.pallas_call(
        functools.partial(_sba_kernel, blk=blk),
        grid=(batch, pairs, seq // blk),
        in_specs=[pl.BlockSpec((1, blk, LANES), lambda b, p, i: (b, i, p)),
                  pl.BlockSpec((1, seq, LANES), lambda b, p, i: (b, 0, pairs + p)),
                  pl.BlockSpec((1, seq, LANES), lambda b, p, i: (b, 0, 2 * pairs + p)),
                  pl.BlockSpec(m_ext.shape, lambda b, p, i: (0, 0))],
        out_specs=pl.BlockSpec((1, blk, LANES), lambda b, p, i: (b, i, p)),
        out_shape=jax.ShapeDtypeStruct((batch, seq, BRANCH_WIDTH), BF16),
        scratch_shapes=[pltpu.VMEM((2, blk, LANES), F32), pltpu.VMEM((2, blk, LANES), F32)],
        compiler_params=_cparams(("parallel", "parallel", "parallel")),
        name="sb_attention",
    )(qkv3, qkv3, qkv3, m_ext)
    return out.reshape(batch * seq, BRANCH_WIDTH)


def _s5_operators(lam_re, lam_im, log_dt, b_re, b_im, c_re, c_im, d_skip):
    L = S5_CHUNK
    g, p, c = S5_GROUPS, S5_STATE, S5_GROUP_CH
    dt = jnp.exp(log_dt.astype(F32))[:, None]
    lr = lam_re.astype(F32)
    li = lam_im.astype(F32)

    def a_pow(n):
        n = jnp.asarray(n, F32)[..., None, None]
        mag = jnp.exp(lr * dt * n)
        return mag * jnp.cos(li * dt * n), mag * jnp.sin(li * dt * n)

    ar, ai = a_pow(1.0)
    den = lr * lr + li * li
    nr = ar - 1.0
    fr = (nr * lr + ai * li) / den
    fi = (ai * lr - nr * li) / den
    bbr = fr[..., None] * b_re - fi[..., None] * b_im
    bbi = fr[..., None] * b_im + fi[..., None] * b_re
    cr = c_re.astype(F32)
    ci = c_im.astype(F32)

    pr, pi = a_pow(np.arange(L))
    car = cr[None] * pr[:, :, None, :] - ci[None] * pi[:, :, None, :]
    cai = cr[None] * pi[:, :, None, :] + ci[None] * pr[:, :, None, :]
    klag = jnp.einsum('jgop,gpi->jgoi', car, bbr) - jnp.einsum('jgop,gpi->jgoi', cai, bbi)
    klag = klag.at[0].add(d_skip.astype(F32).reshape(g, c)[:, :, None] * jnp.eye(c, dtype=F32))
    s_idx = np.arange(L)[:, None]
    t_idx = np.arange(L)[None, :]
    lag = np.clip(t_idx - s_idx, 0, L - 1)
    toe = klag[lag]
    toe = jnp.where((t_idx >= s_idx)[:, :, None, None, None], toe, 0.0)
    toe = toe.transpose(2, 0, 4, 1, 3).reshape(g, L * c, L * c)

    qr, qi = a_pow(L - 1 - np.arange(L))
    p_re = qr[:, :, :, None] * bbr[None] - qi[:, :, :, None] * bbi[None]
    p_im = qr[:, :, :, None] * bbi[None] + qi[:, :, :, None] * bbr[None]
    p_re = p_re.transpose(1, 0, 3, 2).reshape(g, L * c, p)
    p_im = p_im.transpose(1, 0, 3, 2).reshape(g, L * c, p)

    er, ei = a_pow(1 + np.arange(L))
    q_re = (cr[None] * er[:, :, None, :] - ci[None] * ei[:, :, None, :])
    q_im = -(cr[None] * ei[:, :, None, :] + ci[None] * er[:, :, None, :])
    q_re = q_re.transpose(1, 3, 0, 2).reshape(g, p, L * c)
    q_im = q_im.transpose(1, 3, 0, 2).reshape(g, p, L * c)

    a_lr, a_li = a_pow(float(L))

    def pair_diag(m):
        gg, r, cc = m.shape
        m = m.reshape(gg // 2, 2, r, cc)
        z = jnp.zeros_like(m[:, 0])
        top = jnp.concatenate([m[:, 0], z], axis=2)
        bot = jnp.concatenate([z, m[:, 1]], axis=2)
        return jnp.concatenate([top, bot], axis=1)

    toe_p = pair_diag(toe).astype(BF16)
    p_p = jnp.concatenate([pair_diag(p_re), pair_diag(p_im)], axis=2).astype(BF16)
    q_p = jnp.concatenate([pair_diag(q_re), pair_diag(q_im)], axis=1).astype(BF16)
    a_re = a_lr.reshape(1, g * p)
    a_im = a_li.reshape(1, g * p)
    return toe_p, p_p, q_p, a_re, a_im


def _group_block_transpose(tiles):
    group = lax.broadcasted_iota(jnp.int32, tiles[0].shape, 1) // S5_GROUP_CH
    n = len(tiles)
    k = n // 2
    while k >= 1:
        upper = (group & k) != 0
        nxt = list(tiles)
        for i in range(n):
            if i & k == 0:
                lo, hi = tiles[i], tiles[i + k]
                nxt[i] = jnp.where(upper, pltpu.roll(hi, k * S5_GROUP_CH, 1), lo)
                nxt[i + k] = jnp.where(upper, hi, pltpu.roll(lo, LANES - k * S5_GROUP_CH, 1))
        tiles = nxt
        k //= 2
    return tiles


def _s5_state_in_kernel(u_ref, p_ref, up_ref, vre_ref, vim_ref, *, nb):
    L = S5_CHUNK
    per = LANES // S5_GROUP_CH
    for kb in range(u_ref.shape[0]):
        halves = []
        for a in range(L // per):
            steps = [u_ref[kb, pl.ds(a * per + b, nb, stride=L), :] for b in range(per)]
            halves.append(_group_block_transpose(steps))
        for pi in range(per // 2):
            pair = kb * (per // 2) + pi
            row = jnp.concatenate([halves[a][2 * pi + gi] for gi in range(2) for a in range(L // per)], axis=1)
            row = row.astype(BF16)
            up_ref[pair] = row
            v = _dot(row, p_ref[pair])
            vre_ref[:, pair * LANES:(pair + 1) * LANES] = v[:, :LANES]
            vim_ref[:, pair * LANES:(pair + 1) * LANES] = v[:, LANES:]


def _s5_scan_kernel(vre_ref, vim_ref, are_ref, aim_ref, sre_ref, sim_ref):
    n = vre_ref.shape[0]
    ar = are_ref[...]
    ai = aim_ref[...]

    def step(r, carry):
        xr, xi = carry
        sre_ref[pl.ds(r, 1), :] = xr
        sim_ref[pl.ds(r, 1), :] = xi
        vr = vre_ref[pl.ds(r, 1), :]
        vi = vim_ref[pl.ds(r, 1), :]
        return ar * xr - ai * xi + vr, ar * xi + ai * xr + vi

    z = jnp.zeros_like(ar)
    lax.fori_loop(0, n, step, (z, z))


def _s5_out_kernel(up_ref, sre_ref, sim_ref, toe_ref, q_ref, y_ref, *, nb):
    L = S5_CHUNK
    per = LANES // S5_GROUP_CH
    for kb in range(y_ref.shape[0]):
        halves = [[None] * per for _ in range(L // per)]
        for pi in range(per // 2):
            pair = kb * (per // 2) + pi
            q = q_ref[pair]
            cols = slice(pair * LANES, (pair + 1) * LANES)
            y = _dot(up_ref[pair], toe_ref[pair])
            y += _dot(sre_ref[:, cols].astype(BF16), q[:LANES]) + _dot(sim_ref[:, cols].astype(BF16), q[LANES:])
            y = jax.nn.gelu(y)
            for gi in range(2):
                for a in range(L // per):
                    j = gi * (L // per) + a
                    halves[a][2 * pi + gi] = y[:, j * LANES:(j + 1) * LANES]
        for a in range(L // per):
            steps = _group_block_transpose(halves[a])
            for b in range(per):
                y_ref[kb, pl.ds(a * per + b, nb, stride=L), :] = steps[b]


def _s5_ssm(u, batch, seq, ops, nb=128):
    toe_p, p_p, q_p, a_re, a_im = ops
    nslab, t, _ = u.shape
    L = S5_CHUNK
    npair = S5_GROUPS // 2
    nc = t // L
    row = 2 * L * S5_GROUP_CH
    ncols = S5_GROUPS * S5_STATE
    nb = min(nb, nc)

    def full(a):
        return pl.BlockSpec(a.shape, lambda i: (0,) * a.ndim)

    slab_spec = pl.BlockSpec((nslab, nb * L, LANES), lambda i: (0, i, 0))
    up_spec = pl.BlockSpec((npair, nb, row), lambda i: (0, i, 0))
    st_spec = pl.BlockSpec((nb, ncols), lambda i: (i, 0))
    u_p, v_re, v_im = pl.pallas_call(
        functools.partial(_s5_state_in_kernel, nb=nb),
        grid=(nc // nb,),
        in_specs=[slab_spec, full(p_p)],
        out_specs=[up_spec, st_spec, st_spec],
        out_shape=[jax.ShapeDtypeStruct((npair, nc, row), BF16)] + [jax.ShapeDtypeStruct((nc, ncols), F32)] * 2,
        compiler_params=_cparams(("parallel",)),
        name="s5_chunk_state",
    )(u, p_p)
    ncb = nc // batch
    cw = 512
    s_re, s_im = pl.pallas_call(
        _s5_scan_kernel,
        grid=(batch, ncols // cw),
        in_specs=[pl.BlockSpec((ncb, cw), lambda b, j: (b, j)),
                  pl.BlockSpec((ncb, cw), lambda b, j: (b, j)),
                  pl.BlockSpec((1, cw), lambda b, j: (0, j)),
                  pl.BlockSpec((1, cw), lambda b, j: (0, j))],
        out_specs=[pl.BlockSpec((ncb, cw), lambda b, j: (b, j))] * 2,
        out_shape=[jax.ShapeDtypeStruct((nc, ncols), F32)] * 2,
        compiler_params=_cparams(("parallel", "parallel")),
        name="s5_scan",
    )(v_re, v_im, a_re, a_im)
    return pl.pallas_call(
        functools.partial(_s5_out_kernel, nb=nb),
        grid=(nc // nb,),
        in_specs=[up_spec, st_spec, st_spec, full(toe_p), full(q_p)],
        out_specs=slab_spec,
        out_shape=jax.ShapeDtypeStruct((nslab, t, LANES), F32),
        compiler_params=_cparams(("parallel",)),
        name="s5_out",
    )(u_p, s_re, s_im, toe_p, q_p)


def _ret_tables(seq):
    half = HEAD_DIM // 2
    pos = jnp.arange(seq, dtype=F32)
    inv = ROPE_BASE ** (-jnp.arange(half, dtype=F32) / half)
    ang = pos[:, None] * inv[None, :]
    cos = jnp.tile(jnp.cos(ang), (1, LANES // half))
    sin = jnp.sin(ang)
    sin = jnp.tile(jnp.concatenate([-sin, sin], axis=1), (1, LANES // HEAD_DIM))
    c = RET_CHUNK
    gamma = 1.0 - 2.0 ** (-5.0 - jnp.arange(RET_HEADS, dtype=F32))
    log_g = jnp.log(gamma)
    idx = jnp.arange(c, dtype=F32)
    diff = idx[:, None] - idx[None, :]
    intra = jnp.where(diff >= 0, jnp.exp(jnp.maximum(diff, 0.0)[None] * log_g[:, None, None]), 0.0)
    q_dec = jnp.exp((idx + 1.0)[None, :] * log_g[:, None])
    k_dec = jnp.exp((c - 1.0 - idx)[None, :] * log_g[:, None])
    ch_dec = jnp.exp(c * log_g)

    def lanes(x):
        x = x.reshape(RET_HEADS // 2, 2, c)
        return jnp.repeat(x.transpose(0, 2, 1), HEAD_DIM, axis=2)

    q_dec_l = lanes(q_dec)
    k_dec_l = lanes(k_dec)
    hp = RET_HEADS // 2
    blockdiag = np.kron(np.eye(2), np.ones((HEAD_DIM, HEAD_DIM)))
    st_dec = jnp.repeat(ch_dec.reshape(hp, 2), HEAD_DIM, axis=1)[:, :, None] * blockdiag[None]
    avg = jnp.asarray(blockdiag / HEAD_DIM, BF16)
    return cos, sin, intra, q_dec_l, k_dec_l, st_dec.astype(F32), avg


def _ret_kernel(q_ref, k_ref, v_ref, g_ref, cos_ref, sin_ref, intra_ref, qd_ref, kd_ref, sd_ref, avg_ref, nw_ref,
                o_ref, st_ref, *, nchunk):
    c = RET_CHUNK

    @pl.when(pl.program_id(2) == 0)
    def _():
        st_ref[...] = jnp.zeros_like(st_ref)

    lane = lax.broadcasted_iota(jnp.int32, (c, LANES), 1)
    first_half = (lane % HEAD_DIM) < (HEAD_DIM // 2)
    head0 = lane < HEAD_DIM
    sd = sd_ref[0]
    avg = avg_ref[...]
    same_head = avg > 0
    kscale = HEAD_DIM ** -0.5

    def rotary(x, cos, sin):
        swapped = jnp.where(first_half, pltpu.roll(x, LANES - HEAD_DIM // 2, 1), pltpu.roll(x, HEAD_DIM // 2, 1))
        return x * cos + swapped * sin

    for ci in range(nchunk):
        rows = slice(ci * c, (ci + 1) * c)
        cos = cos_ref[rows, :]
        sin = sin_ref[rows, :]
        q = rotary(q_ref[0, rows, :].astype(F32), cos, sin)
        k = rotary(k_ref[0, rows, :].astype(F32), cos, sin) * kscale
        v = v_ref[0, rows, :]
        qb = q.astype(BF16)
        kb = k.astype(BF16)
        zero = jnp.zeros_like(qb)
        out = None
        for hd in range(2):
            qh = jnp.where(head0, qb, zero) if hd == 0 else jnp.where(head0, zero, qb)
            scores = _dot_nt(qh, kb) * intra_ref[0, hd]
            oh = _dot(scores.astype(BF16), v)
            out = oh if hd == 0 else jnp.where(head0, out, oh)
        state = st_ref[...]
        out = out + _dot(qb, state.astype(BF16)) * qd_ref[0]
        kd_t = (k * kd_ref[0]).T.astype(BF16)
        st_ref[...] = state * sd + jnp.where(same_head, _dot(kd_t, v), 0.0)
        hi, lo = _split_bf16(out)
        mu = _dot(hi, avg) + _dot(lo, avg)
        dlt = out - mu
        hi, lo = _split_bf16(dlt * dlt)
        var = _dot(hi, avg) + _dot(lo, avg)
        gate = g_ref[0, rows, :].astype(F32)
        y = dlt * lax.rsqrt(var + EPS) * nw_ref[...] * (gate * jax.nn.sigmoid(gate))
        o_ref[0, rows, :] = y.astype(o_ref.dtype)


def _retention(ret, norm_w, batch, seq, tables, tq=512):
    cos, sin, intra, q_dec_l, k_dec_l, st_dec, avg = tables
    pairs = BRANCH_WIDTH // LANES
    ret3 = ret.reshape(batch, seq, 4 * BRANCH_WIDTH)
    tq = min(tq, seq)
    c = RET_CHUNK

    def col(k):
        return pl.BlockSpec((1, tq, LANES), lambda b, p, i: (b, i, k * pairs + p))

    out = pl.pallas_call(
        functools.partial(_ret_kernel, nchunk=tq // c),
        grid=(batch, pairs, seq // tq),
        in_specs=[col(0), col(1), col(2), col(3),
                  pl.BlockSpec((tq, LANES), lambda b, p, i: (i, 0)),
                  pl.BlockSpec((tq, LANES), lambda b, p, i: (i, 0)),
                  pl.BlockSpec((1, 2, c, c), lambda b, p, i: (p, 0, 0, 0)),
                  pl.BlockSpec((1, c, LANES), lambda b, p, i: (p, 0, 0)),
                  pl.BlockSpec((1, c, LANES), lambda b, p, i: (p, 0, 0)),
                  pl.BlockSpec((1, LANES, LANES), lambda b, p, i: (p, 0, 0)),
                  pl.BlockSpec((LANES, LANES), lambda b, p, i: (0, 0)),
                  pl.BlockSpec((1, LANES), lambda b, p, i: (0, p))],
        out_specs=pl.BlockSpec((1, tq, LANES), lambda b, p, i: (b, i, p)),
        out_shape=jax.ShapeDtypeStruct((batch, seq, BRANCH_WIDTH), BF16),
        scratch_shapes=[pltpu.VMEM((LANES, LANES), F32)],
        compiler_params=_cparams(("parallel", "parallel", "arbitrary")),
        name="retention",
    )(ret3, ret3, ret3, ret3, cos, sin, intra.reshape(pairs, 2, c, c), q_dec_l, k_dec_l, st_dec, avg,
      norm_w.reshape(1, BRANCH_WIDTH).astype(F32))
    return out.reshape(batch * seq, BRANCH_WIDTH)


def _route(logits):
    lane = lax.broadcasted_iota(jnp.int32, logits.shape, 1).astype(F32)
    neg = jnp.float32(-jnp.inf)
    big = jnp.float32(1 << 20)
    gl = jnp.where(lane < N_GROUPS, logits, neg)
    gmax = jnp.max(gl, axis=1, keepdims=True)
    gsum = jnp.sum(jnp.exp(gl - gmax), axis=1, keepdims=True)
    g_w = 1.0 / gsum
    g_idx = jnp.min(jnp.where(gl == gmax, lane, big), axis=1, keepdims=True)
    lo = ROUTER_LANE0 + EXPERTS_PER_GROUP * g_idx
    sl = jnp.where((lane >= lo) & (lane < lo + EXPERTS_PER_GROUP), logits, neg)
    m1 = jnp.max(sl, axis=1, keepdims=True)
    i1 = jnp.min(jnp.where(sl == m1, lane, big), axis=1, keepdims=True)
    sl2 = jnp.where(lane == i1, neg, sl)
    m2 = jnp.max(sl2, axis=1, keepdims=True)
    i2 = jnp.min(jnp.where(sl2 == m2, lane, big), axis=1, keepdims=True)
    e2 = jnp.exp(m2 - m1)
    w1 = g_w / (1.0 + e2)
    w2 = g_w * e2 / (1.0 + e2)
    return jnp.where(lane == i1, w1, 0.0) + jnp.where(lane == i2, w2, 0.0)


def _merge_kernel(ysb_ref, ys5_ref, yret_ref, gate_ref, h_ref, wglu_ref, wbr_ref, wout_ref, n2_ref, wr_hi_ref,
                  wr_lo_ref, br_ref, h_out_ref, hn_ref, comb_ref):
    ys5 = jnp.concatenate([ys5_ref[s] for s in range(ys5_ref.shape[0])], axis=1).astype(BF16)
    glu = _dot(ys5, wglu_ref[...])
    y_s5 = (glu[:, :BRANCH_WIDTH] * jax.nn.sigmoid(glu[:, BRANCH_WIDTH:])).astype(BF16)
    merged = None
    for n, y in enumerate((ysb_ref[...], y_s5, yret_ref[...])):
        g = jax.nn.sigmoid(gate_ref[:, n * D_MODEL:(n + 1) * D_MODEL].astype(F32))
        term = g * _dot(y, wbr_ref[n])
        merged = term if merged is None else merged + term
    h = h_ref[...] + _dot(merged.astype(BF16), wout_ref[...])
    h_out_ref[...] = h
    ms = jnp.mean(h * h, axis=-1, keepdims=True)
    hn = h * lax.rsqrt(ms + EPS) * n2_ref[...]
    hn_ref[...] = hn.astype(BF16)
    hi, lo = _split_bf16(hn)
    logits = _dot(hi, wr_hi_ref[...]) + _dot(hi, wr_lo_ref[...]) + _dot(lo, wr_hi_ref[...]) + br_ref[...]
    comb_ref[...] = _route(logits)


def _merge(y_sb, y_s5, y_ret, gates, h, w_glu, w_br, w_out, norm2_w, wr_hi, wr_lo, b_r, tm=256):
    t = h.shape[0]

    def rows(w):
        return pl.BlockSpec((tm, w), lambda i: (i, 0))

    def full(a):
        return pl.BlockSpec(a.shape, lambda i: (0,) * a.ndim)

    return pl.pallas_call(
        _merge_kernel,
        grid=(t // tm,),
        in_specs=[rows(BRANCH_WIDTH), pl.BlockSpec((y_s5.shape[0], tm, LANES), lambda i: (0, i, 0)),
                  rows(BRANCH_WIDTH), rows(N_BRANCH * D_MODEL),
                  rows(D_MODEL), full(w_glu), full(w_br), full(w_out), full(norm2_w), full(wr_hi), full(wr_lo),
                  full(b_r)],
        out_specs=[rows(D_MODEL), rows(D_MODEL), rows(LANES)],
        out_shape=[jax.ShapeDtypeStruct((t, D_MODEL), F32), jax.ShapeDtypeStruct((t, D_MODEL), BF16),
                   jax.ShapeDtypeStruct((t, LANES), F32)],
        compiler_params=_cparams(("parallel",)),
        name="merge_route",
    )(y_sb, y_s5, y_ret, gates, h, w_glu, w_br, w_out, norm2_w, wr_hi, wr_lo, b_r)


def _moe_kernel(hn_ref, comb_ref, h_ref, wg_ref, wu_ref, wd_ref, o_ref):
    e = pl.program_id(1)

    @pl.when(e == 0)
    def _():
        o_ref[...] = h_ref[...]

    x = hn_ref[...]
    lane = lax.broadcasted_iota(jnp.int32, comb_ref.shape, 1)
    c_e = jnp.sum(jnp.where(lane == e + ROUTER_LANE0, comb_ref[...], 0.0), axis=1, keepdims=True)
    gate = _dot(x, wg_ref[0])
    up = _dot(x, wu_ref[0])
    hdn = gate * jax.nn.sigmoid(gate) * up
    o_ref[...] += _dot((hdn * c_e).astype(BF16), wd_ref[0])


def _moe(hn, comb, h, w_gate, w_up, w_down, tm=1024):
    t = h.shape[0]
    tm = min(tm, t)
    return pl.pallas_call(
        _moe_kernel,
        grid=(t // tm, N_EXPERTS),
        in_specs=[pl.BlockSpec((tm, D_MODEL), lambda i, e: (i, 0)),
                  pl.BlockSpec((tm, LANES), lambda i, e: (i, 0)),
                  pl.BlockSpec((tm, D_MODEL), lambda i, e: (i, 0)),
                  pl.BlockSpec((1, D_MODEL, EXPERT_FF), lambda i, e: (e, 0, 0)),
                  pl.BlockSpec((1, D_MODEL, EXPERT_FF), lambda i, e: (e, 0, 0)),
                  pl.BlockSpec((1, EXPERT_FF, D_MODEL), lambda i, e: (e, 0, 0))],
        out_specs=pl.BlockSpec((tm, D_MODEL), lambda i, e: (i, 0)),
        out_shape=jax.ShapeDtypeStruct((t, D_MODEL), F32),
        compiler_params=_cparams(("parallel", "arbitrary")),
        name="moe_experts",
    )(hn, comb, h, w_gate, w_up, w_down)


def _final_norm_kernel(h_ref, w_ref, o_ref):
    x = h_ref[...]
    ms = jnp.mean(x * x, axis=-1, keepdims=True)
    o_ref[...] = x * lax.rsqrt(ms + EPS) * w_ref[...]


def _final_norm(h, w, tm=512):
    t = h.shape[0]
    tm = min(tm, t)
    return pl.pallas_call(
        _final_norm_kernel,
        grid=(t // tm,),
        in_specs=[pl.BlockSpec((tm, D_MODEL), lambda i: (i, 0)), pl.BlockSpec((1, D_MODEL), lambda i: (0, 0))],
        out_specs=pl.BlockSpec((tm, D_MODEL), lambda i: (i, 0)),
        out_shape=jax.ShapeDtypeStruct((t, D_MODEL), F32),
        compiler_params=_cparams(("parallel",)),
        name="final_norm",
    )(h, w)


def kernel(x, norm1_w, w_in, s5_lambda_re, s5_lambda_im, s5_log_dt, s5_b_re, s5_b_im, s5_c_re, s5_c_im, s5_d,
           s5_w_glu, ret_norm_w, w_branch, w_out, norm2_w, w_group, b_group, w_router, b_router, w_gate, w_up,
           w_down, final_norm_w):
    batch, seq, _ = x.shape
    depth = w_in.shape[0]
    t = batch * seq
    h = x.reshape(t, D_MODEL).astype(F32)
    ret_tables = _ret_tables(seq)
    for layer in range(depth):
        qkv, u_s5, ret, gates = _inproj(h, norm1_w[layer].reshape(1, D_MODEL).astype(F32),
                                        w_in[layer].astype(BF16))
        y_sb = _sb_attention(qkv, batch, seq)
        s5_ops = _s5_operators(s5_lambda_re[layer], s5_lambda_im[layer], s5_log_dt[layer], s5_b_re[layer],
                               s5_b_im[layer], s5_c_re[layer], s5_c_im[layer], s5_d[layer])
        y_s5 = _s5_ssm(u_s5, batch, seq, s5_ops)
        y_ret = _retention(ret, ret_norm_w[layer], batch, seq, ret_tables)
        w_r = jnp.concatenate([w_group[layer], w_router[layer].transpose(1, 0, 2).reshape(D_MODEL, N_EXPERTS)], axis=1)
        w_r = jnp.pad(w_r.astype(F32), ((0, 0), (0, LANES - w_r.shape[1])))
        wr_hi, wr_lo = _split_bf16(w_r)
        b_r = jnp.pad(jnp.concatenate([b_group[layer], b_router[layer].reshape(-1)]).astype(F32),
                      (0, LANES - N_GROUPS - N_EXPERTS)).reshape(1, LANES)
        h, hn, comb = _merge(y_sb, y_s5, y_ret, gates, h, s5_w_glu[layer].astype(BF16), w_branch[layer].astype(BF16),
                             w_out[layer].astype(BF16), norm2_w[layer].reshape(1, D_MODEL).astype(F32), wr_hi, wr_lo,
                             b_r)
        h = _moe(hn, comb, h, w_gate[layer].astype(BF16), w_up[layer].astype(BF16), w_down[layer].astype(BF16))
    out = _final_norm(h, final_norm_w.reshape(1, D_MODEL).astype(F32))
    return out.reshape(batch, seq, D_MODEL).astype(x.dtype)
```

```python
import functools
import math

import jax
import jax.numpy as jnp
import numpy as np
from jax import lax
from jax.experimental import pallas as pl
from jax.experimental.pallas import tpu as pltpu

F32 = jnp.float32
BF16 = jnp.bfloat16

D_MODEL = 1024
SB_HEADS = 8
HEAD_DIM = 64
BRANCH_WIDTH = 512
S5_GROUPS = 32
S5_GROUP_CH = 16
S5_STATE = 64
RET_HEADS = 8
RET_CHUNK = 128
ROPE_BASE = 10000.0
N_BRANCH = 3
N_GROUPS = 4
EXPERTS_PER_GROUP = 8
N_EXPERTS = 32
TOP_K = 2
EXPERT_FF = 256
EPS = 1e-6

LANES = 128
VMEM_LIMIT = 56 * 1024 * 1024
S5_CHUNK = 16
SB_BLOCK = 256
SB_SKIP_LOG = -106.0
ROUTER_LANE0 = N_GROUPS


def _cparams(sem):
    return pltpu.CompilerParams(dimension_semantics=sem, vmem_limit_bytes=VMEM_LIMIT)


def _split_bf16(x):
    hi = x.astype(BF16)
    lo = (x - hi.astype(F32)).astype(BF16)
    return hi, lo


def _dot(a, b):
    return jnp.dot(a, b, preferred_element_type=F32)


def _dot_nt(a, b):
    return lax.dot_general(a, b, (((1,), (1,)), ((), ())), preferred_element_type=F32)


def _dot_tn(a, b):
    return lax.dot_general(a, b, (((0,), (0,)), ((), ())), preferred_element_type=F32)


def _inproj_kernel(h_ref, nw_ref, w_ref, qkv_ref, u_ref, ret_ref, gate_ref):
    x = h_ref[...]
    ms = jnp.mean(x * x, axis=-1, keepdims=True)
    xn = (x * lax.rsqrt(ms + EPS) * nw_ref[...]).astype(BF16)
    off = 0
    for ref in (qkv_ref, u_ref, ret_ref, gate_ref):
        if ref is u_ref:
            u = _dot(xn, w_ref[:, off:off + BRANCH_WIDTH])
            for s in range(u_ref.shape[0]):
                u_ref[s] = u[:, s * LANES:(s + 1) * LANES]
            off += BRANCH_WIDTH
            continue
        width = ref.shape[1]
        for c in range(0, width, 512):
            ref[:, c:c + 512] = _dot(xn, w_ref[:, off + c:off + c + 512]).astype(ref.dtype)
        off += width


def _inproj(h, norm_w, w_in, tm=256):
    t = h.shape[0]
    nslab = BRANCH_WIDTH // LANES

    def rows(w):
        return pl.BlockSpec((tm, w), lambda i: (i, 0))

    return pl.pallas_call(
        _inproj_kernel,
        grid=(t // tm,),
        in_specs=[rows(D_MODEL),
                  pl.BlockSpec((1, D_MODEL), lambda i: (0, 0)),
                  pl.BlockSpec(w_in.shape, lambda i: (0, 0))],
        out_specs=[rows(3 * BRANCH_WIDTH), pl.BlockSpec((nslab, tm, LANES), lambda i: (0, i, 0)),
                   rows(4 * BRANCH_WIDTH), rows(N_BRANCH * D_MODEL)],
        out_shape=[jax.ShapeDtypeStruct((t, 3 * BRANCH_WIDTH), BF16),
                   jax.ShapeDtypeStruct((nslab, t, LANES), F32),
                   jax.ShapeDtypeStruct((t, 4 * BRANCH_WIDTH), BF16),
                   jax.ShapeDtypeStruct((t, N_BRANCH * D_MODEL), BF16)],
        compiler_params=_cparams(("parallel",)),
        name="inproj",
    )(h, norm_w, w_in)


def _sba_kernel(q_ref, k_ref, v_ref, m_ref, o_ref, acc_ref, car_ref, *, blk):
    i = pl.program_id(2)
    q = q_ref[0]
    lane = lax.broadcasted_iota(jnp.int32, (blk, LANES), 1)
    zero = jnp.zeros_like(q)
    q_heads = (jnp.where(lane < HEAD_DIM, q, zero), jnp.where(lane >= HEAD_DIM, q, zero))
    scale = HEAD_DIM ** -0.5
    m_ext = m_ref[...]
    acc_ref[...] = jnp.zeros_like(acc_ref)
    car_ref[...] = jnp.zeros_like(car_ref)

    def visit(j, diagonal):
        start = pl.multiple_of(j * blk, blk)
        kb = k_ref[0, pl.ds(start, blk), :]
        vb = v_ref[0, pl.ds(start, blk), :]
        if diagonal:
            row = lax.broadcasted_iota(jnp.int32, (blk, blk), 0)
            col = lax.broadcasted_iota(jnp.int32, (blk, blk), 1)
            causal = col < row
        for hd in range(2):
            z = _dot_nt(q_heads[hd], kb) * scale
            sp = jnp.log(1.0 + jnp.exp(-jnp.abs(z)))
            log_beta = jnp.minimum(z, 0.0) - sp
            log_1m = jnp.minimum(-z, 0.0) - sp
            if diagonal:
                log_1m = jnp.where(causal, log_1m, 0.0)
            hi, lo = _split_bf16(log_1m)
            sfx = _dot(hi, m_ext) + _dot(lo, m_ext)
            car = car_ref[hd]
            suffix = sfx[:, :blk] + jnp.concatenate([car] * (blk // LANES), axis=1)
            w = jnp.exp(log_beta + suffix)
            if diagonal:
                w = jnp.where(causal, w, 0.0)
            acc_ref[hd] += _dot(w.astype(BF16), vb)
            car_ref[hd] = car + sfx[:, blk:]

    def mass_left():
        most = jnp.maximum(jnp.max(car_ref[0]), jnp.max(car_ref[1]))
        return (most > SB_SKIP_LOG).astype(jnp.int32)

    visit(i, True)

    def cond(c):
        j, go = c
        return jnp.logical_and(j >= 0, go > 0)

    def body(c):
        j, _ = c
        visit(j, False)
        return j - 1, mass_left()

    lax.while_loop(cond, body, (i - 1, mass_left()))
    o_ref[0] = jnp.where(lane < HEAD_DIM, acc_ref[0], acc_ref[1]).astype(o_ref.dtype)


def _sb_attention(qkv, batch, seq, blk=SB_BLOCK):
    qkv3 = qkv.reshape(batch, seq, 3 * BRANCH_WIDTH)
    pairs = BRANCH_WIDTH // LANES
    r = np.arange(blk)
    m_ext = np.concatenate([(r[:, None] > r[None, :]), np.ones((blk, LANES), bool)], axis=1)
    m_ext = jnp.asarray(m_ext, BF16)
    out = pl.pallas_call(
        functools.partial(_sba_kernel, blk=blk),
        grid=(batch, pairs, seq // blk),
        in_specs=[pl.BlockSpec((1, blk, LANES), lambda b, p, i: (b, i, p)),
                  pl.BlockSpec((1, seq, LANES), lambda b, p, i: (b, 0, pairs + p)),
                  pl.BlockSpec((1, seq, LANES), lambda b, p, i: (b, 0, 2 * pairs + p)),
                  pl.BlockSpec(m_ext.shape, lambda b, p, i: (0, 0))],
        out_specs=pl.BlockSpec((1, blk, LANES), lambda b, p, i: (b, i, p)),
        out_shape=jax.ShapeDtypeStruct((batch, seq, BRANCH_WIDTH), BF16),
        scratch_shapes=[pltpu.VMEM((2, blk, LANES), F32), pltpu.VMEM((2, blk, LANES), F32)],
        compiler_params=_cparams(("parallel", "parallel", "parallel")),
        name="sb_attention",
    )(qkv3, qkv3, qkv3, m_ext)
    return out.reshape(batch * seq, BRANCH_WIDTH)


def _s5_operators(lam_re, lam_im, log_dt, b_re, b_im, c_re, c_im, d_skip):
    L = S5_CHUNK
    g, p, c = S5_GROUPS, S5_STATE, S5_GROUP_CH
    dt = jnp.exp(log_dt.astype(F32))[:, None]
    lr = lam_re.astype(F32)
    li = lam_im.astype(F32)

    def a_pow(n):
        n = jnp.asarray(n, F32)[..., None, None]
        mag = jnp.exp(lr * dt * n)
        return mag * jnp.cos(li * dt * n), mag * jnp.sin(li * dt * n)

    ar, ai = a_pow(1.0)
    den = lr * lr + li * li
    nr = ar - 1.0
    fr = (nr * lr + ai * li) / den
    fi = (ai * lr - nr * li) / den
    bbr = fr[..., None] * b_re - fi[..., None] * b_im
    bbi = fr[..., None] * b_im + fi[..., None] * b_re
    cr = c_re.astype(F32)
    ci = c_im.astype(F32)

    pr, pi = a_pow(np.arange(L))
    car = cr[None] * pr[:, :, None, :] - ci[None] * pi[:, :, None, :]
    cai = cr[None] * pi[:, :, None, :] + ci[None] * pr[:, :, None, :]
    klag = jnp.einsum('jgop,gpi->jgoi', car, bbr) - jnp.einsum('jgop,gpi->jgoi', cai, bbi)
    klag = klag.at[0].add(d_skip.astype(F32).reshape(g, c)[:, :, None] * jnp.eye(c, dtype=F32))
    s_idx = np.arange(L)[:, None]
    t_idx = np.arange(L)[None, :]
    lag = np.clip(t_idx - s_idx, 0, L - 1)
    toe = klag[lag]
    toe = jnp.where((t_idx >= s_idx)[:, :, None, None, None], toe, 0.0)
    toe = toe.transpose(2, 0, 4, 1, 3).reshape(g, L * c, L * c)

    qr, qi = a_pow(L - 1 - np.arange(L))
    p_re = qr[:, :, :, None] * bbr[None] - qi[:, :, :, None] * bbi[None]
    p_im = qr[:, :, :, None] * bbi[None] + qi[:, :, :, None] * bbr[None]
    p_re = p_re.transpose(1, 0, 3, 2).reshape(g, L * c, p)
    p_im = p_im.transpose(1, 0, 3, 2).reshape(g, L * c, p)

    er, ei = a_pow(1 + np.arange(L))
    q_re = (cr[None] * er[:, :, None, :] - ci[None] * ei[:, :, None, :])
    q_im = -(cr[None] * ei[:, :, None, :] + ci[None] * er[:, :, None, :])
    q_re = q_re.transpose(1, 3, 0, 2).reshape(g, p, L * c)
    q_im = q_im.transpose(1, 3, 0, 2).reshape(g, p, L * c)

    a_lr, a_li = a_pow(float(L))

    def pair_diag(m):
        gg, r, cc = m.shape
        m = m.reshape(gg // 2, 2, r, cc)
        z = jnp.zeros_like(m[:, 0])
        top = jnp.concatenate([m[:, 0], z], axis=2)
        bot = jnp.concatenate([z, m[:, 1]], axis=2)
        return jnp.concatenate([top, bot], axis=1)

    toe_p = pair_diag(toe).astype(BF16)
    p_p = jnp.concatenate([pair_diag(p_re), pair_diag(p_im)], axis=2).astype(BF16)
    q_p = jnp.concatenate([pair_diag(q_re), pair_diag(q_im)], axis=1).astype(BF16)
    a_re = a_lr.reshape(1, g * p)
    a_im = a_li.reshape(1, g * p)
    return toe_p, p_p, q_p, a_re, a_im


def _group_block_transpose(tiles):
    group = lax.broadcasted_iota(jnp.int32, tiles[0].shape, 1) // S5_GROUP_CH
    n = len(tiles)
    k = n // 2
    while k >= 1:
        upper = (group & k) != 0
        nxt = list(tiles)
        for i in range(n):
            if i & k == 0:
                lo, hi = tiles[i], tiles[i + k]
                nxt[i] = jnp.where(upper, pltpu.roll(hi, k * S5_GROUP_CH, 1), lo)
                nxt[i + k] = jnp.where(upper, hi, pltpu.roll(lo, LANES - k * S5_GROUP_CH, 1))
        tiles = nxt
        k //= 2
    return tiles


def _s5_state_in_kernel(u_ref, p_ref, up_ref, vre_ref, vim_ref, *, nb):
    L = S5_CHUNK
    per = LANES // S5_GROUP_CH
    for kb in range(u_ref.shape[0]):
        halves = []
        for a in range(L // per):
            steps = [u_ref[kb, pl.ds(a * per + b, nb, stride=L), :] for b in range(per)]
            halves.append(_group_block_transpose(steps))
        for pi in range(per // 2):
            pair = kb * (per // 2) + pi
            row = jnp.concatenate([halves[a][2 * pi + gi] for gi in range(2) for a in range(L // per)], axis=1)
            row = row.astype(BF16)
            up_ref[pair] = row
            v = _dot(row, p_ref[pair])
            vre_ref[:, pair * LANES:(pair + 1) * LANES] = v[:, :LANES]
            vim_ref[:, pair * LANES:(pair + 1) * LANES] = v[:, LANES:]


def _s5_scan_kernel(vre_ref, vim_ref, are_ref, aim_ref, sre_ref, sim_ref):
    n = vre_ref.shape[0]
    ar = are_ref[...]
    ai = aim_ref[...]

    def step(r, carry):
        xr, xi = carry
        sre_ref[pl.ds(r, 1), :] = xr
        sim_ref[pl.ds(r, 1), :] = xi
        vr = vre_ref[pl.ds(r, 1), :]
        vi = vim_ref[pl.ds(r, 1), :]
        return ar * xr - ai * xi + vr, ar * xi + ai * xr + vi

    z = jnp.zeros_like(ar)
    lax.fori_loop(0, n, step, (z, z))


def _s5_out_kernel(up_ref, sre_ref, sim_ref, toe_ref, q_ref, y_ref, *, nb):
    L = S5_CHUNK
    per = LANES // S5_GROUP_CH
    for kb in range(y_ref.shape[0]):
        halves = [[None] * per for _ in range(L // per)]
        for pi in range(per // 2):
            pair = kb * (per // 2) + pi
            q = q_ref[pair]
            cols = slice(pair * LANES, (pair + 1) * LANES)
            y = _dot(up_ref[pair], toe_ref[pair])
            y += _dot(sre_ref[:, cols].astype(BF16), q[:LANES]) + _dot(sim_ref[:, cols].astype(BF16), q[LANES:])
            y = jax.nn.gelu(y)
            for gi in range(2):
                for a in range(L // per):
                    j = gi * (L // per) + a
                    halves[a][2 * pi + gi] = y[:, j * LANES:(j + 1) * LANES]
        for a in range(L // per):
            steps = _group_block_transpose(halves[a])
            for b in range(per):
                y_ref[kb, pl.ds(a * per + b, nb, stride=L), :] = steps[b]


def _s5_ssm(u, batch, seq, ops, nb=128):
    toe_p, p_p, q_p, a_re, a_im = ops
    nslab, t, _ = u.shape
    L = S5_CHUNK
    npair = S5_GROUPS // 2
    nc = t // L
    row = 2 * L * S5_GROUP_CH
    ncols = S5_GROUPS * S5_STATE
    nb = min(nb, nc)

    def full(a):
        return pl.BlockSpec(a.shape, lambda i: (0,) * a.ndim)

    slab_spec = pl.BlockSpec((nslab, nb * L, LANES), lambda i: (0, i, 0))
    up_spec = pl.BlockSpec((npair, nb, row), lambda i: (0, i, 0))
    st_spec = pl.BlockSpec((nb, ncols), lambda i: (i, 0))
    u_p, v_re, v_im = pl.pallas_call(
        functools.partial(_s5_state_in_kernel, nb=nb),
        grid=(nc // nb,),
        in_specs=[slab_spec, full(p_p)],
        out_specs=[up_spec, st_spec, st_spec],
        out_shape=[jax.ShapeDtypeStruct((npair, nc, row), BF16)] + [jax.ShapeDtypeStruct((nc, ncols), F32)] * 2,
        compiler_params=_cparams(("parallel",)),
        name="s5_chunk_state",
    )(u, p_p)
    ncb = nc // batch
    cw = 512
    s_re, s_im = pl.pallas_call(
        _s5_scan_kernel,
        grid=(batch, ncols // cw),
        in_specs=[pl.BlockSpec((ncb, cw), lambda b, j: (b, j)),
                  pl.BlockSpec((ncb, cw), lambda b, j: (b, j)),
                  pl.BlockSpec((1, cw), lambda b, j: (0, j)),
                  pl.BlockSpec((1, cw), lambda b, j: (0, j))],
        out_specs=[pl.BlockSpec((ncb, cw), lambda b, j: (b, j))] * 2,
        out_shape=[jax.ShapeDtypeStruct((nc, ncols), F32)] * 2,
        compiler_params=_cparams(("parallel", "parallel")),
        name="s5_scan",
    )(v_re, v_im, a_re, a_im)
    return pl.pallas_call(
        functools.partial(_s5_out_kernel, nb=nb),
        grid=(nc // nb,),
        in_specs=[up_spec, st_spec, st_spec, full(toe_p), full(q_p)],
        out_specs=slab_spec,
        out_shape=jax.ShapeDtypeStruct((nslab, t, LANES), F32),
        compiler_params=_cparams(("parallel",)),
        name="s5_out",
    )(u_p, s_re, s_im, toe_p, q_p)


def _ret_tables(seq):
    half = HEAD_DIM // 2
    pos = jnp.arange(seq, dtype=F32)
    inv = ROPE_BASE ** (-jnp.arange(half, dtype=F32) / half)
    ang = pos[:, None] * inv[None, :]
    cos = jnp.tile(jnp.cos(ang), (1, LANES // half))
    sin = jnp.sin(ang)
    sin = jnp.tile(jnp.concatenate([-sin, sin], axis=1), (1, LANES // HEAD_DIM))
    c = RET_CHUNK
    gamma = 1.0 - 2.0 ** (-5.0 - jnp.arange(RET_HEADS, dtype=F32))
    log_g = jnp.log(gamma)
    idx = jnp.arange(c, dtype=F32)
    diff = idx[:, None] - idx[None, :]
    intra = jnp.where(diff >= 0, jnp.exp(jnp.maximum(diff, 0.0)[None] * log_g[:, None, None]), 0.0)
    q_dec = jnp.exp((idx + 1.0)[None, :] * log_g[:, None])
    k_dec = jnp.exp((c - 1.0 - idx)[None, :] * log_g[:, None])
    ch_dec = jnp.exp(c * log_g)

    def lanes(x):
        x = x.reshape(RET_HEADS // 2, 2, c)
        return jnp.repeat(x.transpose(0, 2, 1), HEAD_DIM, axis=2)

    q_dec_l = lanes(q_dec)
    k_dec_l = lanes(k_dec)
    hp = RET_HEADS // 2
    blockdiag = np.kron(np.eye(2), np.ones((HEAD_DIM, HEAD_DIM)))
    st_dec = jnp.repeat(ch_dec.reshape(hp, 2), HEAD_DIM, axis=1)[:, :, None] * blockdiag[None]
    avg = jnp.asarray(blockdiag / HEAD_DIM, BF16)
    return cos, sin, intra, q_dec_l, k_dec_l, st_dec.astype(F32), avg


def _ret_kernel(q_ref, k_ref, v_ref, g_ref, cos_ref, sin_ref, intra_ref, qd_ref, kd_ref, sd_ref, avg_ref, nw_ref,
                o_ref, st_ref, *, nchunk):
    c = RET_CHUNK

    @pl.when(pl.program_id(2) == 0)
    def _():
        st_ref[...] = jnp.zeros_like(st_ref)

    lane = lax.broadcasted_iota(jnp.int32, (c, LANES), 1)
    first_half = (lane % HEAD_DIM) < (HEAD_DIM // 2)
    head0 = lane < HEAD_DIM
    sd = sd_ref[0]
    avg = avg_ref[...]
    same_head = avg > 0
    kscale = HEAD_DIM ** -0.5

    def rotary(x, cos, sin):
        swapped = jnp.where(first_half, pltpu.roll(x, LANES - HEAD_DIM // 2, 1), pltpu.roll(x, HEAD_DIM // 2, 1))
        return x * cos + swapped * sin

    for ci in range(nchunk):
        rows = slice(ci * c, (ci + 1) * c)
        cos = cos_ref[rows, :]
        sin = sin_ref[rows, :]
        q = rotary(q_ref[0, rows, :].astype(F32), cos, sin)
        k = rotary(k_ref[0, rows, :].astype(F32), cos, sin) * kscale
        v = v_ref[0, rows, :]
        qb = q.astype(BF16)
        kb = k.astype(BF16)
        zero = jnp.zeros_like(qb)
        out = None
        for hd in range(2):
            qh = jnp.where(head0, qb, zero) if hd == 0 else jnp.where(head0, zero, qb)
            scores = _dot_nt(qh, kb) * intra_ref[0, hd]
            oh = _dot(scores.astype(BF16), v)
            out = oh if hd == 0 else jnp.where(head0, out, oh)
        state = st_ref[...]
        out = out + _dot(qb, state.astype(BF16)) * qd_ref[0]
        kd_t = (k * kd_ref[0]).T.astype(BF16)
        st_ref[...] = state * sd + jnp.where(same_head, _dot(kd_t, v), 0.0)
        hi, lo = _split_bf16(out)
        mu = _dot(hi, avg) + _dot(lo, avg)
        dlt = out - mu
        hi, lo = _split_bf16(dlt * dlt)
        var = _dot(hi, avg) + _dot(lo, avg)
        gate = g_ref[0, rows, :].astype(F32)
        y = dlt * lax.rsqrt(var + EPS) * nw_ref[...] * (gate * jax.nn.sigmoid(gate))
        o_ref[0, rows, :] = y.astype(o_ref.dtype)


def _retention(ret, norm_w, batch, seq, tables, tq=512):
    cos, sin, intra, q_dec_l, k_dec_l, st_dec, avg = tables
    pairs = BRANCH_WIDTH // LANES
    ret3 = ret.reshape(batch, seq, 4 * BRANCH_WIDTH)
    tq = min(tq, seq)
    c = RET_CHUNK

    def col(k):
        return pl.BlockSpec((1, tq, LANES), lambda b, p, i: (b, i, k * pairs + p))

    out = pl.pallas_call(
        functools.partial(_ret_kernel, nchunk=tq // c),
        grid=(batch, pairs, seq // tq),
        in_specs=[col(0), col(1), col(2), col(3),
                  pl.BlockSpec((tq, LANES), lambda b, p, i: (i, 0)),
                  pl.BlockSpec((tq, LANES), lambda b, p, i: (i, 0)),
                  pl.BlockSpec((1, 2, c, c), lambda b, p, i: (p, 0, 0, 0)),
                  pl.BlockSpec((1, c, LANES), lambda b, p, i: (p, 0, 0)),
                  pl.BlockSpec((1, c, LANES), lambda b, p, i: (p, 0, 0)),
                  pl.BlockSpec((1, LANES, LANES), lambda b, p, i: (p, 0, 0)),
                  pl.BlockSpec((LANES, LANES), lambda b, p, i: (0, 0)),
                  pl.BlockSpec((1, LANES), lambda b, p, i: (0, p))],
        out_specs=pl.BlockSpec((1, tq, LANES), lambda b, p, i: (b, i, p)),
        out_shape=jax.ShapeDtypeStruct((batch, seq, BRANCH_WIDTH), BF16),
        scratch_shapes=[pltpu.VMEM((LANES, LANES), F32)],
        compiler_params=_cparams(("parallel", "parallel", "arbitrary")),
        name="retention",
    )(ret3, ret3, ret3, ret3, cos, sin, intra.reshape(pairs, 2, c, c), q_dec_l, k_dec_l, st_dec, avg,
      norm_w.reshape(1, BRANCH_WIDTH).astype(F32))
    return out.reshape(batch * seq, BRANCH_WIDTH)


def _route(logits):
    lane = lax.broadcasted_iota(jnp.int32, logits.shape, 1).astype(F32)
    neg = jnp.float32(-jnp.inf)
    big = jnp.float32(1 << 20)
    gl = jnp.where(lane < N_GROUPS, logits, neg)
    gmax = jnp.max(gl, axis=1, keepdims=True)
    gsum = jnp.sum(jnp.exp(gl - gmax), axis=1, keepdims=True)
    g_w = 1.0 / gsum
    g_idx = jnp.min(jnp.where(gl == gmax, lane, big), axis=1, keepdims=True)
    lo = ROUTER_LANE0 + EXPERTS_PER_GROUP * g_idx
    sl = jnp.where((lane >= lo) & (lane < lo + EXPERTS_PER_GROUP), logits, neg)
    m1 = jnp.max(sl, axis=1, keepdims=True)
    i1 = jnp.min(jnp.where(sl == m1, lane, big), axis=1, keepdims=True)
    sl2 = jnp.where(lane == i1, neg, sl)
    m2 = jnp.max(sl2, axis=1, keepdims=True)
    i2 = jnp.min(jnp.where(sl2 == m2, lane, big), axis=1, keepdims=True)
    e2 = jnp.exp(m2 - m1)
    w1 = g_w / (1.0 + e2)
    w2 = g_w * e2 / (1.0 + e2)
    packed = jnp.where(lane == 0.0, i1 - ROUTER_LANE0, jnp.where(lane == 1.0, i2 - ROUTER_LANE0, 0.0))
    return packed + jnp.where(lane == 2.0, w1, 0.0) + jnp.where(lane == 3.0, w2, 0.0)


def _merge_kernel(ysb_ref, ys5_ref, yret_ref, gate_ref, h_ref, wglu_ref, wbr_ref, wout_ref, n2_ref, wr_hi_ref,
                  wr_lo_ref, br_ref, h_out_ref, hn_ref, route_ref):
    ys5 = jnp.concatenate([ys5_ref[s] for s in range(ys5_ref.shape[0])], axis=1).astype(BF16)
    glu = _dot(ys5, wglu_ref[...])
    y_s5 = (glu[:, :BRANCH_WIDTH] * jax.nn.sigmoid(glu[:, BRANCH_WIDTH:])).astype(BF16)
    merged = None
    for n, y in enumerate((ysb_ref[...], y_s5, yret_ref[...])):
        g = jax.nn.sigmoid(gate_ref[:, n * D_MODEL:(n + 1) * D_MODEL].astype(F32))
        term = g * _dot(y, wbr_ref[n])
        merged = term if merged is None else merged + term
    h = h_ref[...] + _dot(merged.astype(BF16), wout_ref[...])
    h_out_ref[...] = h
    ms = jnp.mean(h * h, axis=-1, keepdims=True)
    hn = h * lax.rsqrt(ms + EPS) * n2_ref[...]
    hn_ref[...] = hn.astype(BF16)
    hi, lo = _split_bf16(hn)
    logits = _dot(hi, wr_hi_ref[...]) + _dot(hi, wr_lo_ref[...]) + _dot(lo, wr_hi_ref[...]) + br_ref[...]
    route_ref[...] = _route(logits)


def _merge(y_sb, y_s5, y_ret, gates, h, w_glu, w_br, w_out, norm2_w, wr_hi, wr_lo, b_r, tm=256):
    t = h.shape[0]

    def rows(w):
        return pl.BlockSpec((tm, w), lambda i: (i, 0))

    def full(a):
        return pl.BlockSpec(a.shape, lambda i: (0,) * a.ndim)

    return pl.pallas_call(
        _merge_kernel,
        grid=(t // tm,),
        in_specs=[rows(BRANCH_WIDTH), pl.BlockSpec((y_s5.shape[0], tm, LANES), lambda i: (0, i, 0)),
                  rows(BRANCH_WIDTH), rows(N_BRANCH * D_MODEL),
                  rows(D_MODEL), full(w_glu), full(w_br), full(w_out), full(norm2_w), full(wr_hi), full(wr_lo),
                  full(b_r)],
        out_specs=[rows(D_MODEL), rows(D_MODEL), rows(LANES)],
        out_shape=[jax.ShapeDtypeStruct((t, D_MODEL), F32), jax.ShapeDtypeStruct((t, D_MODEL), BF16),
                   jax.ShapeDtypeStruct((t, LANES), F32)],
        compiler_params=_cparams(("parallel",)),
        name="merge_route",
    )(y_sb, y_s5, y_ret, gates, h, w_glu, w_br, w_out, norm2_w, wr_hi, wr_lo, b_r)


MOE_TILE = 512
MOE_ROW_TILE = 256
SEG_ALIGN = 16
MOE_SLOTS = TOP_K * MOE_TILE + N_EXPERTS * SEG_ALIGN
MOE_CHUNKS = MOE_SLOTS // SEG_ALIGN


def _moe_plan_kernel(route_ref, tri_ref, cum_ref, pos_ref, cnt_ref):
    r = route_ref[...]
    tm = r.shape[0]
    lane = lax.broadcasted_iota(jnp.int32, r.shape, 1).astype(F32)
    e1 = r[:, 0:1]
    e2 = r[:, 1:2]
    chosen = (lane == e1) | (lane == e2)
    onehot = jnp.where(chosen, 1.0, 0.0).astype(BF16)
    incl = _dot(tri_ref[...], onehot)
    cnt = incl[tm - 1:tm, :]
    padded = jnp.floor((cnt + (SEG_ALIGN - 1)) * (1.0 / SEG_ALIGN)) * SEG_ALIGN
    start = _dot(jnp.broadcast_to(padded, (8, LANES)).astype(BF16), cum_ref[...])[0:1, :]
    slot = start + incl - 1.0
    p1 = jnp.sum(jnp.where(lane == e1, slot, 0.0), axis=1, keepdims=True)
    p2 = jnp.sum(jnp.where(lane == e2, slot, 0.0), axis=1, keepdims=True)
    pos_ref[...] = jnp.where(lane == 0.0, p1, jnp.where(lane == 1.0, p2, 0.0))
    cnt_ref[0] = jnp.broadcast_to(cnt, (8, LANES))


def _moe_plan(route, tm):
    t = route.shape[0]
    r = np.arange(tm)
    tri = jnp.asarray(r[None, :] <= r[:, None], BF16)
    l = np.arange(LANES)
    cum = jnp.asarray(l[:, None] < l[None, :], BF16)
    return pl.pallas_call(
        _moe_plan_kernel,
        grid=(t // tm,),
        in_specs=[pl.BlockSpec((tm, LANES), lambda i: (i, 0)),
                  pl.BlockSpec((tm, tm), lambda i: (0, 0)),
                  pl.BlockSpec((LANES, LANES), lambda i: (0, 0))],
        out_specs=[pl.BlockSpec((tm, LANES), lambda i: (i, 0)), pl.BlockSpec((1, 8, LANES), lambda i: (i, 0, 0))],
        out_shape=[jax.ShapeDtypeStruct((t, LANES), F32), jax.ShapeDtypeStruct((t // tm, 8, LANES), F32)],
        compiler_params=_cparams(("parallel",)),
        name="moe_plan",
    )(route, tri, cum)


def _moe_tables(cnt, rt, n_rows):
    a = SEG_ALIGN
    ntile = cnt.shape[0]
    p = (cnt + a - 1) // a * a
    lend = jnp.cumsum(p, axis=1)
    lstart = lend - p
    seg = p.sum(axis=0)
    segpad = (seg + rt - 1) // rt * rt
    gend = jnp.cumsum(segpad)
    gstart = gend - segpad
    toff = gstart[None, :] + jnp.cumsum(p, axis=0) - p
    cstart = a * jnp.arange(MOE_CHUNKS, dtype=jnp.int32)
    e_c = jnp.minimum(jnp.sum(cstart[None, :, None] >= lend[:, None, :], axis=2), N_EXPERTS - 1)
    gdst = (jnp.take_along_axis(toff, e_c, 1) + cstart[None, :] - jnp.take_along_axis(lstart, e_c, 1)) // a
    nch = lend[:, -1] // a
    tstart = rt * jnp.arange(n_rows // rt, dtype=jnp.int32)
    tile_e = jnp.minimum(jnp.sum(tstart[:, None] >= gend[None, :], axis=1), N_EXPERTS - 1)
    n_used = gend[-1] // rt
    zmax = rt // a - 1
    zper = -(-N_EXPERTS * zmax // ntile)
    k = jnp.arange(zmax, dtype=jnp.int32)
    zd = ((gstart + seg) // a)[:, None] + k[None, :]
    valid = (k[None, :] < ((segpad - seg) // a)[:, None]).reshape(-1)
    order = jnp.argsort(jnp.logical_not(valid), stable=True)
    zdst = jnp.pad(zd.reshape(-1)[order], (0, ntile * zper - N_EXPERTS * zmax))
    nz = valid.sum()
    i32 = jnp.int32
    return (gdst.reshape(-1).astype(i32), nch.astype(i32), zdst.astype(i32), nz.reshape(1).astype(i32),
            tile_e.astype(i32), n_used.reshape(1).astype(i32)), zper


def _moe_dispatch_kernel(gdst_ref, nch_ref, zdst_ref, nz_ref, n_used_ref, pos_ref, hn_ref, xs_hbm, stg, zblk, sem,
                         zsem, *, zper):
    i = pl.program_id(0)
    tm = hn_ref.shape[0]
    pos_t = pos_ref[...].T
    p1 = pos_t[0:1, :]
    p2 = pos_t[1:2, :]
    x = hn_ref[...]
    rb = MOE_ROW_TILE
    for b in range(MOE_SLOTS // rb):
        slot = (lax.broadcasted_iota(jnp.int32, (rb, tm), 0) + b * rb).astype(F32)
        onehot = jnp.where((slot == p1) | (slot == p2), 1.0, 0.0).astype(BF16)
        stg[b * rb:(b + 1) * rb, :] = _dot(onehot, x).astype(BF16)

    def chunk_copy(src_chunk, dst_chunk):
        src = pl.multiple_of(src_chunk * SEG_ALIGN, SEG_ALIGN)
        dst = pl.multiple_of(dst_chunk * SEG_ALIGN, SEG_ALIGN)
        return pltpu.make_async_copy(stg.at[pl.ds(src, SEG_ALIGN)], xs_hbm.at[pl.ds(dst, SEG_ALIGN)], sem)

    n = nch_ref[i]
    nzero = jnp.clip(nz_ref[0] - i * zper, 0, zper)

    @pl.loop(0, n)
    def _(c):
        chunk_copy(c, gdst_ref[i * MOE_CHUNKS + c]).start()

    @pl.loop(0, nzero)
    def _(k):
        chunk_copy(MOE_CHUNKS - 1, zdst_ref[i * zper + k]).start()

    @pl.loop(0, n + nzero)
    def _(c):
        chunk_copy(0, 0).wait()

    @pl.when(i == pl.num_programs(0) - 1)
    def _():
        rt = zblk.shape[0]
        zblk[...] = jnp.zeros_like(zblk)
        first = n_used_ref[0]
        last = xs_hbm.shape[0] // rt

        def tile_copy(j):
            return pltpu.make_async_copy(zblk, xs_hbm.at[pl.ds(pl.multiple_of(j * rt, rt), rt)], zsem)

        @pl.loop(first, last)
        def _(j):
            tile_copy(j).start()

        @pl.loop(first, last)
        def _(j):
            tile_copy(0).wait()


def _moe_expert_kernel(tile_e_ref, n_used_ref, xs_ref, wg_ref, wu_ref, wd_ref, ys_ref):
    used = pl.program_id(0) < n_used_ref[0]

    @pl.when(used)
    def _():
        x = xs_ref[...]
        gate = _dot(x, wg_ref[0])
        up = _dot(x, wu_ref[0])
        hdn = gate * jax.nn.sigmoid(gate) * up
        ys_ref[...] = _dot(hdn.astype(BF16), wd_ref[0]).astype(ys_ref.dtype)

    @pl.when(jnp.logical_not(used))
    def _():
        ys_ref[...] = jnp.zeros_like(ys_ref)


def _moe_combine_kernel(gdst_ref, nch_ref, route_ref, pos_ref, h_ref, nw_ref, ys_hbm, o_ref, stg, sem, *, out_norm):
    i = pl.program_id(0)
    tm = h_ref.shape[0]

    def chunk_copy(src_chunk, dst_chunk):
        src = pl.multiple_of(src_chunk * SEG_ALIGN, SEG_ALIGN)
        dst = pl.multiple_of(dst_chunk * SEG_ALIGN, SEG_ALIGN)
        return pltpu.make_async_copy(ys_hbm.at[pl.ds(src, SEG_ALIGN)], stg.at[pl.ds(dst, SEG_ALIGN)], sem)

    n = nch_ref[i]

    @pl.loop(0, n)
    def _(c):
        chunk_copy(gdst_ref[i * MOE_CHUNKS + c], c).start()

    @pl.loop(n, MOE_CHUNKS)
    def _(c):
        stg[pl.ds(pl.multiple_of(c * SEG_ALIGN, SEG_ALIGN), SEG_ALIGN), :] = jnp.zeros((SEG_ALIGN, D_MODEL), BF16)

    r = route_ref[...]
    pos = pos_ref[...]
    slot = lax.broadcasted_iota(jnp.int32, (tm, MOE_SLOTS), 1).astype(F32)
    w = jnp.where(slot == pos[:, 0:1], r[:, 2:3], 0.0) + jnp.where(slot == pos[:, 1:2], r[:, 3:4], 0.0)

    @pl.loop(0, n)
    def _(c):
        chunk_copy(0, 0).wait()

    out = h_ref[...] + _dot(w.astype(BF16), stg[...])
    if out_norm:
        ms = jnp.mean(out * out, axis=-1, keepdims=True)
        out = out * lax.rsqrt(ms + EPS) * nw_ref[...]
    o_ref[...] = out


def _moe(hn, route, h, w_gate, w_up, w_down, out_norm_w, out_norm):
    t = h.shape[0]
    tm = min(MOE_TILE, t)
    rt = MOE_ROW_TILE
    ntile = t // tm
    n_rows = TOP_K * t + ntile * N_EXPERTS * (SEG_ALIGN - 1) + N_EXPERTS * (rt - SEG_ALIGN)
    n_rows = (n_rows + rt - 1) // rt * rt
    pos, cnt = _moe_plan(route, tm)
    (gdst, nch, zdst, nz, tile_e, n_used), zper = _moe_tables(cnt[:, 0, :N_EXPERTS].astype(jnp.int32), rt, n_rows)
    xs = pl.pallas_call(
        functools.partial(_moe_dispatch_kernel, zper=zper),
        grid_spec=pltpu.PrefetchScalarGridSpec(
            num_scalar_prefetch=5,
            grid=(ntile,),
            in_specs=[pl.BlockSpec((tm, LANES), lambda i, *_: (i, 0)),
                      pl.BlockSpec((tm, D_MODEL), lambda i, *_: (i, 0))],
            out_specs=pl.BlockSpec(memory_space=pl.ANY),
            scratch_shapes=[pltpu.VMEM((MOE_SLOTS, D_MODEL), BF16), pltpu.VMEM((rt, D_MODEL), BF16),
                            pltpu.SemaphoreType.DMA(()), pltpu.SemaphoreType.DMA(())]),
        out_shape=jax.ShapeDtypeStruct((n_rows, D_MODEL), BF16),
        compiler_params=_cparams(("arbitrary",)),
        name="moe_dispatch",
    )(gdst, nch, zdst, nz, n_used, pos, hn)
    ys = pl.pallas_call(
        _moe_expert_kernel,
        grid_spec=pltpu.PrefetchScalarGridSpec(
            num_scalar_prefetch=2,
            grid=(n_rows // rt,),
            in_specs=[pl.BlockSpec((rt, D_MODEL), lambda j, te, nu: (jnp.minimum(j, nu[0] - 1), 0)),
                      pl.BlockSpec((1, D_MODEL, EXPERT_FF), lambda j, te, nu: (te[j], 0, 0)),
                      pl.BlockSpec((1, D_MODEL, EXPERT_FF), lambda j, te, nu: (te[j], 0, 0)),
                      pl.BlockSpec((1, EXPERT_FF, D_MODEL), lambda j, te, nu: (te[j], 0, 0))],
            out_specs=pl.BlockSpec((rt, D_MODEL), lambda j, te, nu: (j, 0))),
        out_shape=jax.ShapeDtypeStruct((n_rows, D_MODEL), BF16),
        compiler_params=_cparams(("arbitrary",)),
        name="moe_experts",
    )(tile_e, n_used, xs, w_gate, w_up, w_down)
    return pl.pallas_call(
        functools.partial(_moe_combine_kernel, out_norm=out_norm),
        grid_spec=pltpu.PrefetchScalarGridSpec(
            num_scalar_prefetch=2,
            grid=(ntile,),
            in_specs=[pl.BlockSpec((tm, LANES), lambda i, *_: (i, 0)),
                      pl.BlockSpec((tm, LANES), lambda i, *_: (i, 0)),
                      pl.BlockSpec((tm, D_MODEL), lambda i, *_: (i, 0)),
                      pl.BlockSpec((1, D_MODEL), lambda i, *_: (0, 0)),
                      pl.BlockSpec(memory_space=pl.ANY)],
            out_specs=pl.BlockSpec((tm, D_MODEL), lambda i, *_: (i, 0)),
            scratch_shapes=[pltpu.VMEM((MOE_SLOTS, D_MODEL), BF16), pltpu.SemaphoreType.DMA(())]),
        out_shape=jax.ShapeDtypeStruct((t, D_MODEL), F32),
        compiler_params=_cparams(("arbitrary",)),
        name="moe_combine",
    )(gdst, nch, route, pos, h, out_norm_w, ys)


def kernel(x, norm1_w, w_in, s5_lambda_re, s5_lambda_im, s5_log_dt, s5_b_re, s5_b_im, s5_c_re, s5_c_im, s5_d,
           s5_w_glu, ret_norm_w, w_branch, w_out, norm2_w, w_group, b_group, w_router, b_router, w_gate, w_up,
           w_down, final_norm_w):
    batch, seq, _ = x.shape
    depth = w_in.shape[0]
    t = batch * seq
    h = x.reshape(t, D_MODEL).astype(F32)
    ret_tables = _ret_tables(seq)
    for layer in range(depth):
        qkv, u_s5, ret, gates = _inproj(h, norm1_w[layer].reshape(1, D_MODEL).astype(F32),
                                        w_in[layer].astype(BF16))
        y_sb = _sb_attention(qkv, batch, seq)
        s5_ops = _s5_operators(s5_lambda_re[layer], s5_lambda_im[layer], s5_log_dt[layer], s5_b_re[layer],
                               s5_b_im[layer], s5_c_re[layer], s5_c_im[layer], s5_d[layer])
        y_s5 = _s5_ssm(u_s5, batch, seq, s5_ops)
        y_ret = _retention(ret, ret_norm_w[layer], batch, seq, ret_tables)
        w_r = jnp.concatenate([w_group[layer], w_router[layer].transpose(1, 0, 2).reshape(D_MODEL, N_EXPERTS)], axis=1)
        w_r = jnp.pad(w_r.astype(F32), ((0, 0), (0, LANES - w_r.shape[1])))
        wr_hi, wr_lo = _split_bf16(w_r)
        b_r = jnp.pad(jnp.concatenate([b_group[layer], b_router[layer].reshape(-1)]).astype(F32),
                      (0, LANES - N_GROUPS - N_EXPERTS)).reshape(1, LANES)
        h, hn, route = _merge(y_sb, y_s5, y_ret, gates, h, s5_w_glu[layer].astype(BF16), w_branch[layer].astype(BF16),
                             w_out[layer].astype(BF16), norm2_w[layer].reshape(1, D_MODEL).astype(F32), wr_hi, wr_lo,
                             b_r)
        h = _moe(hn, route, h, w_gate[layer].astype(BF16), w_up[layer].astype(BF16), w_down[layer].astype(BF16),
                 final_norm_w.reshape(1, D_MODEL).astype(F32), out_norm=(layer == depth - 1))
    return h.reshape(batch, seq, D_MODEL).astype(x.dtype)
```

```python
import functools
import math

import jax
import jax.numpy as jnp
import numpy as np
from jax import lax
from jax.experimental import pallas as pl
from jax.experimental.pallas import tpu as pltpu

F32 = jnp.float32
BF16 = jnp.bfloat16

D_MODEL = 1024
SB_HEADS = 8
HEAD_DIM = 64
BRANCH_WIDTH = 512
S5_GROUPS = 32
S5_GROUP_CH = 16
S5_STATE = 64
RET_HEADS = 8
RET_CHUNK = 128
ROPE_BASE = 10000.0
N_BRANCH = 3
N_GROUPS = 4
EXPERTS_PER_GROUP = 8
N_EXPERTS = 32
TOP_K = 2
EXPERT_FF = 256
EPS = 1e-6

LANES = 128
VMEM_LIMIT = 56 * 1024 * 1024
S5_CHUNK = 16
SB_BLOCK = 256
LOG2_E = 1.4426950408889634
SB_SKIP_LOG2 = -152.0
ROUTER_LANE0 = N_GROUPS


def _cparams(sem):
    return pltpu.CompilerParams(dimension_semantics=sem, vmem_limit_bytes=VMEM_LIMIT)


def _layer_block(stacked, layer):
    zeros = (0,) * (stacked.ndim - 1)
    return pl.BlockSpec((None,) + stacked.shape[1:], lambda *_: (layer,) + zeros)


def _split_bf16(x):
    hi = x.astype(BF16)
    lo = (x - hi.astype(F32)).astype(BF16)
    return hi, lo


def _dot(a, b):
    return jnp.dot(a, b, preferred_element_type=F32)


def _dot_nt(a, b):
    return lax.dot_general(a, b, (((1,), (1,)), ((), ())), preferred_element_type=F32)


def _dot_tn(a, b):
    return lax.dot_general(a, b, (((0,), (0,)), ((), ())), preferred_element_type=F32)


def _inproj_kernel(h_ref, nw_ref, w_ref, qkv_ref, u_ref, ret_ref, gate_ref):
    x = h_ref[...]
    ms = jnp.mean(x * x, axis=-1, keepdims=True)
    xn = (x * lax.rsqrt(ms + EPS) * nw_ref[...]).astype(BF16)
    off = 0
    for ref in (qkv_ref, u_ref, ret_ref, gate_ref):
        if ref is u_ref:
            u = _dot(xn, w_ref[:, off:off + BRANCH_WIDTH])
            for s in range(u_ref.shape[0]):
                u_ref[s] = u[:, s * LANES:(s + 1) * LANES]
            off += BRANCH_WIDTH
            continue
        width = ref.shape[1]
        for c in range(0, width, 512):
            ref[:, c:c + 512] = _dot(xn, w_ref[:, off + c:off + c + 512]).astype(ref.dtype)
        off += width


def _inproj(h, norm_w, w_in, layer, tm=256):
    t = h.shape[0]
    nslab = BRANCH_WIDTH // LANES

    def rows(w):
        return pl.BlockSpec((tm, w), lambda i: (i, 0))

    return pl.pallas_call(
        _inproj_kernel,
        grid=(t // tm,),
        in_specs=[rows(D_MODEL),
                  pl.BlockSpec((1, D_MODEL), lambda i: (0, 0)),
                  _layer_block(w_in, layer)],
        out_specs=[rows(3 * BRANCH_WIDTH), pl.BlockSpec((nslab, tm, LANES), lambda i: (0, i, 0)),
                   rows(4 * BRANCH_WIDTH), rows(N_BRANCH * D_MODEL)],
        out_shape=[jax.ShapeDtypeStruct((t, 3 * BRANCH_WIDTH), BF16),
                   jax.ShapeDtypeStruct((nslab, t, LANES), F32),
                   jax.ShapeDtypeStruct((t, 4 * BRANCH_WIDTH), BF16),
                   jax.ShapeDtypeStruct((t, N_BRANCH * D_MODEL), BF16)],
        compiler_params=_cparams(("parallel",)),
        name="inproj",
    )(h, norm_w, w_in)


def _sba_kernel(q_ref, k_ref, v_ref, m_ref, o_ref, acc_ref, car_ref, *, blk):
    i = pl.program_id(2)
    q = q_ref[0]
    lane = lax.broadcasted_iota(jnp.int32, (blk, LANES), 1)
    zero = jnp.zeros_like(q)
    q_heads = (jnp.where(lane < HEAD_DIM, q, zero), jnp.where(lane >= HEAD_DIM, q, zero))
    zscale = HEAD_DIM ** -0.5 * LOG2_E
    tri = m_ref[...]

    def keys(j):
        start = pl.multiple_of(j * blk, blk)
        return k_ref[0, pl.ds(start, blk), :], v_ref[0, pl.ds(start, blk), :]

    def log_terms(z2, causal):
        sp = jnp.log(1.0 + jnp.exp2(-jnp.abs(z2))) * LOG2_E
        log_beta = jnp.minimum(z2, 0.0) - sp
        log_1m = log_beta - z2
        if causal is not None:
            log_1m = jnp.where(causal, log_1m, 0.0)
        hi, lo = _split_bf16(log_1m)
        suffix = _dot(hi, tri) + _dot(lo, tri)
        return log_beta, suffix, jnp.sum(log_1m, axis=1, keepdims=True)

    def visit(block_ids, diagonal_first, car_in):
        kv = [keys(j) for j in block_ids]
        causal = None
        if diagonal_first:
            causal = lax.broadcasted_iota(jnp.int32, (blk, blk), 1) < lax.broadcasted_iota(jnp.int32, (blk, blk), 0)
        chains = [(b, hd) for b in range(len(block_ids)) for hd in range(2)]
        z2 = {c: _dot_nt(q_heads[c[1]], kv[c[0]][0]) * zscale for c in chains}
        terms = {c: log_terms(z2[c], causal if c[0] == 0 else None) for c in chains}
        out = []
        for hd in range(2):
            car = car_in[hd]
            acc = None
            for b in range(len(block_ids)):
                log_beta, suffix, tot = terms[(b, hd)]
                w = jnp.exp2(log_beta + suffix if car is None else log_beta + suffix + car)
                if causal is not None and b == 0:
                    w = jnp.where(causal, w, 0.0)
                pv = _dot(w.astype(BF16), kv[b][1])
                acc = pv if acc is None else acc + pv
                car = tot if car is None else car + tot
            out.append((acc, car))
        return out

    def first_visit(block_ids):
        for hd, (acc, car) in enumerate(visit(block_ids, True, (None, None))):
            acc_ref[hd] = acc
            car_ref[hd] = car

    pl.when(i == 0)(lambda: first_visit([i]))
    pl.when(i > 0)(lambda: first_visit([i, i - 1]))

    def mass_left():
        most = jnp.maximum(jnp.max(car_ref[0]), jnp.max(car_ref[1]))
        return (most > SB_SKIP_LOG2).astype(jnp.int32)

    def cond(c):
        j, go = c
        return jnp.logical_and(j >= 0, go > 0)

    def body(c):
        j, _ = c
        for hd, (acc, car) in enumerate(visit([j], False, (car_ref[0], car_ref[1]))):
            acc_ref[hd] += acc
            car_ref[hd] = car
        return j - 1, mass_left()

    lax.while_loop(cond, body, (i - 2, mass_left()))
    o_ref[0] = jnp.where(lane < HEAD_DIM, acc_ref[0], acc_ref[1]).astype(o_ref.dtype)


def _sb_attention(qkv, batch, seq, blk=SB_BLOCK):
    qkv3 = qkv.reshape(batch, seq, 3 * BRANCH_WIDTH)
    pairs = BRANCH_WIDTH // LANES
    r = np.arange(blk)
    m_ext = jnp.asarray(r[:, None] > r[None, :], BF16)
    out = pl.pallas_call(
        functools.partial(_sba_kernel, blk=blk),
        grid=(batch, pairs, seq // blk),
        in_specs=[pl.BlockSpec((1, blk, LANES), lambda b, p, i: (b, i, p)),
                  pl.BlockSpec((1, seq, LANES), lambda b, p, i: (b, 0, pairs + p)),
                  pl.BlockSpec((1, seq, LANES), lambda b, p, i: (b, 0, 2 * pairs + p)),
                  pl.BlockSpec(m_ext.shape, lambda b, p, i: (0, 0))],
        out_specs=pl.BlockSpec((1, blk, LANES), lambda b, p, i: (b, i, p)),
        out_shape=jax.ShapeDtypeStruct((batch, seq, BRANCH_WIDTH), BF16),
        scratch_shapes=[pltpu.VMEM((2, blk, LANES), F32), pltpu.VMEM((2, blk, 1), F32)],
        compiler_params=_cparams(("parallel", "parallel", "parallel")),
        name="sb_attention",
    )(qkv3, qkv3, qkv3, m_ext)
    return out.reshape(batch * seq, BRANCH_WIDTH)


def _s5_operators(lam_re, lam_im, log_dt, b_re, b_im, c_re, c_im, d_skip):
    L = S5_CHUNK
    g, p, c = S5_GROUPS, S5_STATE, S5_GROUP_CH
    dt = jnp.exp(log_dt.astype(F32))[:, None]
    lr = lam_re.astype(F32)
    li = lam_im.astype(F32)

    def a_pow(n):
        n = jnp.asarray(n, F32)[..., None, None]
        mag = jnp.exp(lr * dt * n)
        return mag * jnp.cos(li * dt * n), mag * jnp.sin(li * dt * n)

    ar, ai = a_pow(1.0)
    den = lr * lr + li * li
    nr = ar - 1.0
    fr = (nr * lr + ai * li) / den
    fi = (ai * lr - nr * li) / den
    bbr = fr[..., None] * b_re - fi[..., None] * b_im
    bbi = fr[..., None] * b_im + fi[..., None] * b_re
    cr = c_re.astype(F32)
    ci = c_im.astype(F32)

    pr, pi = a_pow(np.arange(L))
    car = cr[None] * pr[:, :, None, :] - ci[None] * pi[:, :, None, :]
    cai = cr[None] * pi[:, :, None, :] + ci[None] * pr[:, :, None, :]
    klag = jnp.einsum('jgop,gpi->jgoi', car, bbr) - jnp.einsum('jgop,gpi->jgoi', cai, bbi)
    klag = klag.at[0].add(d_skip.astype(F32).reshape(g, c)[:, :, None] * jnp.eye(c, dtype=F32))
    kflat = klag.transpose(1, 3, 0, 2).reshape(g, c, L * c)
    toe = jnp.stack([jnp.pad(kflat[:, :, :(L - s) * c], ((0, 0), (0, 0), (s * c, 0))) for s in range(L)], axis=1)
    toe = toe.reshape(g, L * c, L * c)

    qr, qi = a_pow(L - 1 - np.arange(L))
    p_re = qr[:, :, :, None] * bbr[None] - qi[:, :, :, None] * bbi[None]
    p_im = qr[:, :, :, None] * bbi[None] + qi[:, :, :, None] * bbr[None]
    p_re = p_re.transpose(1, 0, 3, 2).reshape(g, L * c, p)
    p_im = p_im.transpose(1, 0, 3, 2).reshape(g, L * c, p)

    er, ei = a_pow(1 + np.arange(L))
    q_re = (cr[None] * er[:, :, None, :] - ci[None] * ei[:, :, None, :])
    q_im = -(cr[None] * ei[:, :, None, :] + ci[None] * er[:, :, None, :])
    q_re = q_re.transpose(1, 3, 0, 2).reshape(g, p, L * c)
    q_im = q_im.transpose(1, 3, 0, 2).reshape(g, p, L * c)

    a_lr, a_li = a_pow(float(L))

    def pair_diag(m):
        gg, r, cc = m.shape
        m = m.reshape(gg // 2, 2, r, cc)
        z = jnp.zeros_like(m[:, 0])
        top = jnp.concatenate([m[:, 0], z], axis=2)
        bot = jnp.concatenate([z, m[:, 1]], axis=2)
        return jnp.concatenate([top, bot], axis=1)

    toe_p = pair_diag(toe).astype(BF16)
    p_p = jnp.concatenate([pair_diag(p_re), pair_diag(p_im)], axis=2).astype(BF16)
    q_p = jnp.concatenate([pair_diag(q_re), pair_diag(q_im)], axis=1).astype(BF16)
    a_re = a_lr.reshape(1, g * p)
    a_im = a_li.reshape(1, g * p)
    return toe_p, p_p, q_p, a_re, a_im


def _group_block_transpose(tiles):
    group = lax.broadcasted_iota(jnp.int32, tiles[0].shape, 1) // S5_GROUP_CH
    n = len(tiles)
    k = n // 2
    while k >= 1:
        upper = (group & k) != 0
        nxt = list(tiles)
        for i in range(n):
            if i & k == 0:
                lo, hi = tiles[i], tiles[i + k]
                nxt[i] = jnp.where(upper, pltpu.roll(hi, k * S5_GROUP_CH, 1), lo)
                nxt[i + k] = jnp.where(upper, hi, pltpu.roll(lo, LANES - k * S5_GROUP_CH, 1))
        tiles = nxt
        k //= 2
    return tiles


def _s5_state_in_kernel(u_ref, p_ref, up_ref, vre_ref, vim_ref, *, nb):
    L = S5_CHUNK
    per = LANES // S5_GROUP_CH
    for kb in range(u_ref.shape[0]):
        halves = []
        for a in range(L // per):
            steps = [u_ref[kb, pl.ds(a * per + b, nb, stride=L), :] for b in range(per)]
            halves.append(_group_block_transpose(steps))
        for pi in range(per // 2):
            pair = kb * (per // 2) + pi
            row = jnp.concatenate([halves[a][2 * pi + gi] for gi in range(2) for a in range(L // per)], axis=1)
            row = row.astype(BF16)
            up_ref[pair] = row
            v = _dot(row, p_ref[pair])
            vre_ref[:, pair * LANES:(pair + 1) * LANES] = v[:, :LANES]
            vim_ref[:, pair * LANES:(pair + 1) * LANES] = v[:, LANES:]


def _s5_scan_kernel(vre_ref, vim_ref, are_ref, aim_ref, sre_ref, sim_ref):
    n = vre_ref.shape[0]
    ar = are_ref[...]
    ai = aim_ref[...]

    def step(r, carry):
        xr, xi = carry
        sre_ref[pl.ds(r, 1), :] = xr
        sim_ref[pl.ds(r, 1), :] = xi
        vr = vre_ref[pl.ds(r, 1), :]
        vi = vim_ref[pl.ds(r, 1), :]
        return ar * xr - ai * xi + vr, ar * xi + ai * xr + vi

    z = jnp.zeros_like(ar)
    lax.fori_loop(0, n, step, (z, z))


def _s5_out_kernel(up_ref, sre_ref, sim_ref, toe_ref, q_ref, y_ref, *, nb):
    L = S5_CHUNK
    per = LANES // S5_GROUP_CH
    for kb in range(y_ref.shape[0]):
        halves = [[None] * per for _ in range(L // per)]
        for pi in range(per // 2):
            pair = kb * (per // 2) + pi
            q = q_ref[pair]
            cols = slice(pair * LANES, (pair + 1) * LANES)
            y = _dot(up_ref[pair], toe_ref[pair])
            y += _dot(sre_ref[:, cols].astype(BF16), q[:LANES]) + _dot(sim_ref[:, cols].astype(BF16), q[LANES:])
            y = jax.nn.gelu(y)
            for gi in range(2):
                for a in range(L // per):
                    j = gi * (L // per) + a
                    halves[a][2 * pi + gi] = y[:, j * LANES:(j + 1) * LANES]
        for a in range(L // per):
            steps = _group_block_transpose(halves[a])
            for b in range(per):
                y_ref[kb, pl.ds(a * per + b, nb, stride=L), :] = steps[b]


def _s5_ssm(u, batch, seq, ops, nb=128):
    toe_p, p_p, q_p, a_re, a_im = ops
    nslab, t, _ = u.shape
    L = S5_CHUNK
    npair = S5_GROUPS // 2
    nc = t // L
    row = 2 * L * S5_GROUP_CH
    ncols = S5_GROUPS * S5_STATE
    nb = min(nb, nc)

    def full(a):
        return pl.BlockSpec(a.shape, lambda i: (0,) * a.ndim)

    slab_spec = pl.BlockSpec((nslab, nb * L, LANES), lambda i: (0, i, 0))
    up_spec = pl.BlockSpec((npair, nb, row), lambda i: (0, i, 0))
    st_spec = pl.BlockSpec((nb, ncols), lambda i: (i, 0))
    u_p, v_re, v_im = pl.pallas_call(
        functools.partial(_s5_state_in_kernel, nb=nb),
        grid=(nc // nb,),
        in_specs=[slab_spec, full(p_p)],
        out_specs=[up_spec, st_spec, st_spec],
        out_shape=[jax.ShapeDtypeStruct((npair, nc, row), BF16)] + [jax.ShapeDtypeStruct((nc, ncols), F32)] * 2,
        compiler_params=_cparams(("parallel",)),
        name="s5_chunk_state",
    )(u, p_p)
    ncb = nc // batch
    cw = 512
    s_re, s_im = pl.pallas_call(
        _s5_scan_kernel,
        grid=(batch, ncols // cw),
        in_specs=[pl.BlockSpec((ncb, cw), lambda b, j: (b, j)),
                  pl.BlockSpec((ncb, cw), lambda b, j: (b, j)),
                  pl.BlockSpec((1, cw), lambda b, j: (0, j)),
                  pl.BlockSpec((1, cw), lambda b, j: (0, j))],
        out_specs=[pl.BlockSpec((ncb, cw), lambda b, j: (b, j))] * 2,
        out_shape=[jax.ShapeDtypeStruct((nc, ncols), F32)] * 2,
        compiler_params=_cparams(("parallel", "parallel")),
        name="s5_scan",
    )(v_re, v_im, a_re, a_im)
    return pl.pallas_call(
        functools.partial(_s5_out_kernel, nb=nb),
        grid=(nc // nb,),
        in_specs=[up_spec, st_spec, st_spec, full(toe_p), full(q_p)],
        out_specs=slab_spec,
        out_shape=jax.ShapeDtypeStruct((nslab, t, LANES), F32),
        compiler_params=_cparams(("parallel",)),
        name="s5_out",
    )(u_p, s_re, s_im, toe_p, q_p)


def _ret_tables(seq):
    half = HEAD_DIM // 2
    pos = jnp.arange(seq, dtype=F32)
    inv = ROPE_BASE ** (-jnp.arange(half, dtype=F32) / half)
    ang = pos[:, None] * inv[None, :]
    cos = jnp.tile(jnp.cos(ang), (1, LANES // half))
    sin = jnp.sin(ang)
    sin = jnp.tile(jnp.concatenate([-sin, sin], axis=1), (1, LANES // HEAD_DIM))
    c = RET_CHUNK
    gamma = 1.0 - 2.0 ** (-5.0 - jnp.arange(RET_HEADS, dtype=F32))
    log_g = jnp.log(gamma)
    idx = jnp.arange(c, dtype=F32)
    diff = idx[:, None] - idx[None, :]
    intra = jnp.where(diff >= 0, jnp.exp(jnp.maximum(diff, 0.0)[None] * log_g[:, None, None]), 0.0)
    q_dec = jnp.exp((idx + 1.0)[None, :] * log_g[:, None])
    k_dec = jnp.exp((c - 1.0 - idx)[None, :] * log_g[:, None])
    ch_dec = jnp.exp(c * log_g)

    def lanes(x):
        x = x.reshape(RET_HEADS // 2, 2, c)
        return jnp.repeat(x.transpose(0, 2, 1), HEAD_DIM, axis=2)

    q_dec_l = lanes(q_dec)
    k_dec_l = lanes(k_dec)
    hp = RET_HEADS // 2
    blockdiag = np.kron(np.eye(2), np.ones((HEAD_DIM, HEAD_DIM)))
    st_dec = jnp.repeat(ch_dec.reshape(hp, 2), HEAD_DIM, axis=1)[:, :, None] * blockdiag[None]
    avg = jnp.asarray(blockdiag / HEAD_DIM, BF16)
    return cos, sin, intra, q_dec_l, k_dec_l, st_dec.astype(F32), avg


def _ret_kernel(q_ref, k_ref, v_ref, g_ref, cos_ref, sin_ref, intra_ref, qd_ref, kd_ref, sd_ref, avg_ref, nw_ref,
                o_ref, st_ref, *, nchunk):
    c = RET_CHUNK

    @pl.when(pl.program_id(2) == 0)
    def _():
        st_ref[...] = jnp.zeros_like(st_ref)

    tq = q_ref.shape[1]
    lane = lax.broadcasted_iota(jnp.int32, (tq, LANES), 1)
    first_half = (lane % HEAD_DIM) < (HEAD_DIM // 2)
    head0 = lax.broadcasted_iota(jnp.int32, (c, LANES), 1) < HEAD_DIM
    sd = sd_ref[0]
    avg = avg_ref[...]
    same_head = avg > 0
    cos = cos_ref[...]
    sin = sin_ref[...]

    def rotary(x):
        swapped = jnp.where(first_half, pltpu.roll(x, LANES - HEAD_DIM // 2, 1), pltpu.roll(x, HEAD_DIM // 2, 1))
        return x * cos + swapped * sin

    q_all = rotary(q_ref[0].astype(F32)).astype(BF16)
    k_all = rotary(k_ref[0].astype(F32)) * (HEAD_DIM ** -0.5)
    kb_all = k_all.astype(BF16)
    zero = jnp.zeros((c, LANES), BF16)
    chunks = [slice(ci * c, (ci + 1) * c) for ci in range(nchunk)]
    vs = [v_ref[0, rows, :] for rows in chunks]
    scores, incs = [], []
    for ci, rows in enumerate(chunks):
        qb = q_all[rows]
        q2 = jnp.concatenate([jnp.where(head0, qb, zero), jnp.where(head0, zero, qb)], axis=0)
        scores.append(_dot_nt(q2, kb_all[rows]))
        kd_t = (k_all[rows] * kd_ref[0]).T.astype(BF16)
        incs.append(_dot(kd_t, vs[ci]))
    state = st_ref[...]
    states = []
    for inc in incs:
        states.append(state.astype(BF16))
        state = state * sd + jnp.where(same_head, inc, 0.0)
    st_ref[...] = state
    intra2 = jnp.concatenate([intra_ref[0, 0], intra_ref[0, 1]], axis=0)
    outs = []
    for ci, rows in enumerate(chunks):
        o2 = _dot((scores[ci] * intra2).astype(BF16), vs[ci])
        outs.append(jnp.where(head0, o2[:c], o2[c:]) + _dot(q_all[rows], states[ci]) * qd_ref[0])
    out = jnp.concatenate(outs, axis=0)
    hi, lo = _split_bf16(out)
    mu = _dot(hi, avg) + _dot(lo, avg)
    dlt = out - mu
    hi, lo = _split_bf16(dlt * dlt)
    var = _dot(hi, avg) + _dot(lo, avg)
    gate = g_ref[0].astype(F32)
    y = dlt * lax.rsqrt(var + EPS) * nw_ref[...] * (gate * jax.nn.sigmoid(gate))
    o_ref[0] = y.astype(o_ref.dtype)


def _retention(ret, norm_w, batch, seq, tables, tq=1024):
    cos, sin, intra, q_dec_l, k_dec_l, st_dec, avg = tables
    pairs = BRANCH_WIDTH // LANES
    ret3 = ret.reshape(batch, seq, 4 * BRANCH_WIDTH)
    tq = min(tq, seq)
    c = RET_CHUNK

    def col(k):
        return pl.BlockSpec((1, tq, LANES), lambda b, p, i: (b, i, k * pairs + p))

    out = pl.pallas_call(
        functools.partial(_ret_kernel, nchunk=tq // c),
        grid=(batch, pairs, seq // tq),
        in_specs=[col(0), col(1), col(2), col(3),
                  pl.BlockSpec((tq, LANES), lambda b, p, i: (i, 0)),
                  pl.BlockSpec((tq, LANES), lambda b, p, i: (i, 0)),
                  pl.BlockSpec((1, 2, c, c), lambda b, p, i: (p, 0, 0, 0)),
                  pl.BlockSpec((1, c, LANES), lambda b, p, i: (p, 0, 0)),
                  pl.BlockSpec((1, c, LANES), lambda b, p, i: (p, 0, 0)),
                  pl.BlockSpec((1, LANES, LANES), lambda b, p, i: (p, 0, 0)),
                  pl.BlockSpec((LANES, LANES), lambda b, p, i: (0, 0)),
                  pl.BlockSpec((1, LANES), lambda b, p, i: (0, p))],
        out_specs=pl.BlockSpec((1, tq, LANES), lambda b, p, i: (b, i, p)),
        out_shape=jax.ShapeDtypeStruct((batch, seq, BRANCH_WIDTH), BF16),
        scratch_shapes=[pltpu.VMEM((LANES, LANES), F32)],
        compiler_params=_cparams(("parallel", "parallel", "arbitrary")),
        name="retention",
    )(ret3, ret3, ret3, ret3, cos, sin, intra.reshape(pairs, 2, c, c), q_dec_l, k_dec_l, st_dec, avg,
      norm_w.reshape(1, BRANCH_WIDTH).astype(F32))
    return out.reshape(batch * seq, BRANCH_WIDTH)


def _route(logits):
    lane = lax.broadcasted_iota(jnp.int32, logits.shape, 1).astype(F32)
    neg = jnp.float32(-jnp.inf)
    big = jnp.float32(1 << 20)
    gl = jnp.where(lane < N_GROUPS, logits, neg)
    gmax = jnp.max(gl, axis=1, keepdims=True)
    gsum = jnp.sum(jnp.exp(gl - gmax), axis=1, keepdims=True)
    g_w = 1.0 / gsum
    g_idx = jnp.min(jnp.where(gl == gmax, lane, big), axis=1, keepdims=True)
    lo = ROUTER_LANE0 + EXPERTS_PER_GROUP * g_idx
    sl = jnp.where((lane >= lo) & (lane < lo + EXPERTS_PER_GROUP), logits, neg)
    m1 = jnp.max(sl, axis=1, keepdims=True)
    i1 = jnp.min(jnp.where(sl == m1, lane, big), axis=1, keepdims=True)
    sl2 = jnp.where(lane == i1, neg, sl)
    m2 = jnp.max(sl2, axis=1, keepdims=True)
    i2 = jnp.min(jnp.where(sl2 == m2, lane, big), axis=1, keepdims=True)
    e2 = jnp.exp(m2 - m1)
    w1 = g_w / (1.0 + e2)
    w2 = g_w * e2 / (1.0 + e2)
    packed = jnp.where(lane == 0.0, i1 - ROUTER_LANE0, jnp.where(lane == 1.0, i2 - ROUTER_LANE0, 0.0))
    return packed + jnp.where(lane == 2.0, w1, 0.0) + jnp.where(lane == 3.0, w2, 0.0)


def _merge_kernel(ysb_ref, ys5_ref, yret_ref, gate_ref, h_ref, wglu_ref, wbr_ref, wout_ref, n2_ref, wr_hi_ref,
                  wr_lo_ref, br_ref, h_out_ref, hn_ref, route_ref):
    ys5 = jnp.concatenate([ys5_ref[s] for s in range(ys5_ref.shape[0])], axis=1).astype(BF16)
    glu = _dot(ys5, wglu_ref[...])
    y_s5 = (glu[:, :BRANCH_WIDTH] * jax.nn.sigmoid(glu[:, BRANCH_WIDTH:])).astype(BF16)
    merged = None
    for n, y in enumerate((ysb_ref[...], y_s5, yret_ref[...])):
        g = jax.nn.sigmoid(gate_ref[:, n * D_MODEL:(n + 1) * D_MODEL].astype(F32))
        term = g * _dot(y, wbr_ref[n])
        merged = term if merged is None else merged + term
    h = h_ref[...] + _dot(merged.astype(BF16), wout_ref[...])
    h_out_ref[...] = h
    ms = jnp.mean(h * h, axis=-1, keepdims=True)
    hn = h * lax.rsqrt(ms + EPS) * n2_ref[...]
    hn_ref[...] = hn.astype(BF16)
    hi, lo = _split_bf16(hn)
    logits = _dot(hi, wr_hi_ref[...]) + _dot(hi, wr_lo_ref[...]) + _dot(lo, wr_hi_ref[...]) + br_ref[...]
    route_ref[...] = _route(logits)


def _merge(y_sb, y_s5, y_ret, gates, h, w_glu, w_br, w_out, norm2_w, wr_hi, wr_lo, b_r, layer, tm=256):
    t = h.shape[0]

    def rows(w):
        return pl.BlockSpec((tm, w), lambda i: (i, 0))

    def full(a):
        return pl.BlockSpec(a.shape, lambda i: (0,) * a.ndim)

    return pl.pallas_call(
        _merge_kernel,
        grid=(t // tm,),
        in_specs=[rows(BRANCH_WIDTH), pl.BlockSpec((y_s5.shape[0], tm, LANES), lambda i: (0, i, 0)),
                  rows(BRANCH_WIDTH), rows(N_BRANCH * D_MODEL),
                  rows(D_MODEL), _layer_block(w_glu, layer), _layer_block(w_br, layer), _layer_block(w_out, layer),
                  full(norm2_w), full(wr_hi), full(wr_lo), full(b_r)],
        out_specs=[rows(D_MODEL), rows(D_MODEL), rows(LANES)],
        out_shape=[jax.ShapeDtypeStruct((t, D_MODEL), F32), jax.ShapeDtypeStruct((t, D_MODEL), BF16),
                   jax.ShapeDtypeStruct((t, LANES), F32)],
        compiler_params=_cparams(("parallel",)),
        name="merge_route",
    )(y_sb, y_s5, y_ret, gates, h, w_glu, w_br, w_out, norm2_w, wr_hi, wr_lo, b_r)


MOE_TILE = 512
MOE_ROW_TILE = 512
SEG_ALIGN = 16
MOE_SLOTS = TOP_K * MOE_TILE + N_EXPERTS * SEG_ALIGN
MOE_CHUNKS = MOE_SLOTS // SEG_ALIGN


def _moe_plan_kernel(route_ref, tri_ref, cum_ref, pos_ref, cnt_ref):
    r = route_ref[...]
    tm = r.shape[0]
    lane = lax.broadcasted_iota(jnp.int32, r.shape, 1).astype(F32)
    e1 = r[:, 0:1]
    e2 = r[:, 1:2]
    chosen = (lane == e1) | (lane == e2)
    onehot = jnp.where(chosen, 1.0, 0.0).astype(BF16)
    incl = _dot(tri_ref[...], onehot)
    cnt = incl[tm - 1:tm, :]
    padded = jnp.floor((cnt + (SEG_ALIGN - 1)) * (1.0 / SEG_ALIGN)) * SEG_ALIGN
    start = _dot(jnp.broadcast_to(padded, (8, LANES)).astype(BF16), cum_ref[...])[0:1, :]
    slot = start + incl - 1.0
    p1 = jnp.sum(jnp.where(lane == e1, slot, 0.0), axis=1, keepdims=True)
    p2 = jnp.sum(jnp.where(lane == e2, slot, 0.0), axis=1, keepdims=True)
    pos_ref[...] = jnp.where(lane == 0.0, p1, jnp.where(lane == 1.0, p2, 0.0))
    cnt_ref[0] = jnp.broadcast_to(cnt, (8, LANES))


def _moe_plan(route, tm):
    t = route.shape[0]
    r = np.arange(tm)
    tri = jnp.asarray(r[None, :] <= r[:, None], BF16)
    l = np.arange(LANES)
    cum = jnp.asarray(l[:, None] < l[None, :], BF16)
    return pl.pallas_call(
        _moe_plan_kernel,
        grid=(t // tm,),
        in_specs=[pl.BlockSpec((tm, LANES), lambda i: (i, 0)),
                  pl.BlockSpec((tm, tm), lambda i: (0, 0)),
                  pl.BlockSpec((LANES, LANES), lambda i: (0, 0))],
        out_specs=[pl.BlockSpec((tm, LANES), lambda i: (i, 0)), pl.BlockSpec((1, 8, LANES), lambda i: (i, 0, 0))],
        out_shape=[jax.ShapeDtypeStruct((t, LANES), F32), jax.ShapeDtypeStruct((t // tm, 8, LANES), F32)],
        compiler_params=_cparams(("parallel",)),
        name="moe_plan",
    )(route, tri, cum)


def _moe_tables(cnt, rt, n_rows):
    a = SEG_ALIGN
    ntile = cnt.shape[0]
    p = (cnt + a - 1) // a * a
    lend = jnp.cumsum(p, axis=1)
    lstart = lend - p
    seg = p.sum(axis=0)
    segpad = (seg + rt - 1) // rt * rt
    gend = jnp.cumsum(segpad)
    gstart = gend - segpad
    toff = gstart[None, :] + jnp.cumsum(p, axis=0) - p
    cstart = a * jnp.arange(MOE_CHUNKS, dtype=jnp.int32)
    in_run = (cstart[None, :, None] >= lstart[:, None, :]) & (cstart[None, :, None] < lend[:, None, :])
    gdst = (jnp.sum(jnp.where(in_run, (toff - lstart)[:, None, :], 0), axis=2) + cstart[None, :]) // a
    nch = lend[:, -1] // a
    tstart = rt * jnp.arange(n_rows // rt, dtype=jnp.int32)
    tile_e = jnp.minimum(jnp.sum(tstart[:, None] >= gend[None, :], axis=1), N_EXPERTS - 1)
    n_used = gend[-1] // rt
    zmax = rt // a - 1
    zper = -(-N_EXPERTS * zmax // ntile)
    k = jnp.arange(zmax, dtype=jnp.int32)
    zd = ((gstart + seg) // a)[:, None] + k[None, :]
    valid = (k[None, :] < ((segpad - seg) // a)[:, None]).reshape(-1)
    order = jnp.argsort(jnp.logical_not(valid), stable=True)
    zdst = jnp.pad(zd.reshape(-1)[order], (0, ntile * zper - N_EXPERTS * zmax))
    nz = valid.sum()
    i32 = jnp.int32
    return (gdst.reshape(-1).astype(i32), nch.astype(i32), zdst.astype(i32), nz.reshape(1).astype(i32),
            tile_e.astype(i32), n_used.reshape(1).astype(i32)), zper


def _moe_dispatch_kernel(gdst_ref, nch_ref, zdst_ref, nz_ref, n_used_ref, pos_ref, hn_ref, xs_hbm, stg, zblk, sem,
                         zsem, *, zper):
    i = pl.program_id(0)
    last_step = pl.num_programs(0) - 1
    buf = i % 2
    tm = hn_ref.shape[0]
    pos_t = pos_ref[...].T
    p1 = pos_t[0:1, :]
    p2 = pos_t[1:2, :]
    x = hn_ref[...]
    rb = MOE_ROW_TILE
    for b in range(MOE_SLOTS // rb):
        slot = (lax.broadcasted_iota(jnp.int32, (rb, tm), 0) + b * rb).astype(F32)
        onehot = jnp.where((slot == p1) | (slot == p2), 1.0, 0.0).astype(BF16)
        stg[buf, b * rb:(b + 1) * rb, :] = _dot(onehot, x).astype(BF16)

    def chunk_copy(which, src_chunk, dst_chunk):
        src = pl.multiple_of(src_chunk * SEG_ALIGN, SEG_ALIGN)
        dst = pl.multiple_of(dst_chunk * SEG_ALIGN, SEG_ALIGN)
        return pltpu.make_async_copy(stg.at[which, pl.ds(src, SEG_ALIGN)], xs_hbm.at[pl.ds(dst, SEG_ALIGN)],
                                     sem.at[which])

    def n_copies(step):
        return nch_ref[step], jnp.clip(nz_ref[0] - step * zper, 0, zper)

    def drain(which, step):
        n, nzero = n_copies(step)

        @pl.loop(0, n + nzero)
        def _(c):
            chunk_copy(which, 0, 0).wait()

    n, nzero = n_copies(i)

    @pl.loop(0, n)
    def _(c):
        chunk_copy(buf, c, gdst_ref[i * MOE_CHUNKS + c]).start()

    @pl.loop(0, nzero)
    def _(k):
        chunk_copy(buf, MOE_CHUNKS - 1, zdst_ref[i * zper + k]).start()

    pl.when(i > 0)(lambda: drain(1 - buf, i - 1))

    @pl.when(i == last_step)
    def _():
        drain(buf, i)
        rt = zblk.shape[0]
        zblk[...] = jnp.zeros_like(zblk)
        first = n_used_ref[0]
        last = xs_hbm.shape[0] // rt

        def tile_copy(j):
            return pltpu.make_async_copy(zblk, xs_hbm.at[pl.ds(pl.multiple_of(j * rt, rt), rt)], zsem)

        @pl.loop(first, last)
        def _(j):
            tile_copy(j).start()

        @pl.loop(first, last)
        def _(j):
            tile_copy(0).wait()


def _moe_expert_kernel(tile_e_ref, n_used_ref, xs_ref, wg_ref, wu_ref, wd_ref, ys_ref):
    used = pl.program_id(0) < n_used_ref[0]

    @pl.when(used)
    def _():
        x = xs_ref[...]
        gate = _dot(x, wg_ref[0])
        up = _dot(x, wu_ref[0])
        hdn = gate * jax.nn.sigmoid(gate) * up
        ys_ref[...] = _dot(hdn.astype(BF16), wd_ref[0]).astype(ys_ref.dtype)

    @pl.when(jnp.logical_not(used))
    def _():
        ys_ref[...] = jnp.zeros_like(ys_ref)


def _moe_combine_kernel(gdst_ref, nch_ref, route_ref, pos_ref, h_ref, nw_ref, ys_hbm, o_ref, stg, sem, *, out_norm):
    i = pl.program_id(0)
    nsteps = pl.num_programs(0)
    buf = i % 2
    tm = h_ref.shape[0]

    def chunk_copy(which, src_chunk, dst_chunk):
        src = pl.multiple_of(src_chunk * SEG_ALIGN, SEG_ALIGN)
        dst = pl.multiple_of(dst_chunk * SEG_ALIGN, SEG_ALIGN)
        return pltpu.make_async_copy(ys_hbm.at[pl.ds(src, SEG_ALIGN)], stg.at[which, pl.ds(dst, SEG_ALIGN)],
                                     sem.at[which])

    def fetch(step, which):
        n = nch_ref[step]

        @pl.loop(0, n)
        def _(c):
            chunk_copy(which, gdst_ref[step * MOE_CHUNKS + c], c).start()

        @pl.loop(n, MOE_CHUNKS)
        def _(c):
            rows = pl.ds(pl.multiple_of(c * SEG_ALIGN, SEG_ALIGN), SEG_ALIGN)
            stg[which, rows, :] = jnp.zeros((SEG_ALIGN, D_MODEL), BF16)

    pl.when(i == 0)(lambda: fetch(0, 0))
    pl.when(i + 1 < nsteps)(lambda: fetch(i + 1, 1 - buf))

    r = route_ref[...]
    pos = pos_ref[...]
    slot = lax.broadcasted_iota(jnp.int32, (tm, MOE_SLOTS), 1).astype(F32)
    w = jnp.where(slot == pos[:, 0:1], r[:, 2:3], 0.0) + jnp.where(slot == pos[:, 1:2], r[:, 3:4], 0.0)

    @pl.loop(0, nch_ref[i])
    def _(c):
        chunk_copy(buf, 0, 0).wait()

    out = h_ref[...] + _dot(w.astype(BF16), stg[buf])
    if out_norm:
        ms = jnp.mean(out * out, axis=-1, keepdims=True)
        out = out * lax.rsqrt(ms + EPS) * nw_ref[...]
    o_ref[...] = out


def _moe(hn, route, h, w_gate, w_up, w_down, layer, out_norm_w, out_norm):
    t = h.shape[0]
    tm = min(MOE_TILE, t)
    rt = MOE_ROW_TILE
    ntile = t // tm
    n_rows = TOP_K * t + ntile * N_EXPERTS * (SEG_ALIGN - 1) + N_EXPERTS * (rt - SEG_ALIGN)
    n_rows = (n_rows + rt - 1) // rt * rt
    pos, cnt = _moe_plan(route, tm)
    (gdst, nch, zdst, nz, tile_e, n_used), zper = _moe_tables(cnt[:, 0, :N_EXPERTS].astype(jnp.int32), rt, n_rows)
    xs = pl.pallas_call(
        functools.partial(_moe_dispatch_kernel, zper=zper),
        grid_spec=pltpu.PrefetchScalarGridSpec(
            num_scalar_prefetch=5,
            grid=(ntile,),
            in_specs=[pl.BlockSpec((tm, LANES), lambda i, *_: (i, 0)),
                      pl.BlockSpec((tm, D_MODEL), lambda i, *_: (i, 0))],
            out_specs=pl.BlockSpec(memory_space=pl.ANY),
            scratch_shapes=[pltpu.VMEM((2, MOE_SLOTS, D_MODEL), BF16), pltpu.VMEM((rt, D_MODEL), BF16),
                            pltpu.SemaphoreType.DMA((2,)), pltpu.SemaphoreType.DMA(())]),
        out_shape=jax.ShapeDtypeStruct((n_rows, D_MODEL), BF16),
        compiler_params=_cparams(("arbitrary",)),
        name="moe_dispatch",
    )(gdst, nch, zdst, nz, n_used, pos, hn)
    ys = pl.pallas_call(
        _moe_expert_kernel,
        grid_spec=pltpu.PrefetchScalarGridSpec(
            num_scalar_prefetch=2,
            grid=(n_rows // rt,),
            in_specs=[pl.BlockSpec((rt, D_MODEL), lambda j, te, nu: (jnp.minimum(j, nu[0] - 1), 0)),
                      pl.BlockSpec((None, 1, D_MODEL, EXPERT_FF), lambda j, te, nu: (layer, te[j], 0, 0)),
                      pl.BlockSpec((None, 1, D_MODEL, EXPERT_FF), lambda j, te, nu: (layer, te[j], 0, 0)),
                      pl.BlockSpec((None, 1, EXPERT_FF, D_MODEL), lambda j, te, nu: (layer, te[j], 0, 0))],
            out_specs=pl.BlockSpec((rt, D_MODEL), lambda j, te, nu: (j, 0))),
        out_shape=jax.ShapeDtypeStruct((n_rows, D_MODEL), BF16),
        compiler_params=_cparams(("arbitrary",)),
        name="moe_experts",
    )(tile_e, n_used, xs, w_gate, w_up, w_down)
    return pl.pallas_call(
        functools.partial(_moe_combine_kernel, out_norm=out_norm),
        grid_spec=pltpu.PrefetchScalarGridSpec(
            num_scalar_prefetch=2,
            grid=(ntile,),
            in_specs=[pl.BlockSpec((tm, LANES), lambda i, *_: (i, 0)),
                      pl.BlockSpec((tm, LANES), lambda i, *_: (i, 0)),
                      pl.BlockSpec((tm, D_MODEL), lambda i, *_: (i, 0)),
                      pl.BlockSpec((1, D_MODEL), lambda i, *_: (0, 0)),
                      pl.BlockSpec(memory_space=pl.ANY)],
            out_specs=pl.BlockSpec((tm, D_MODEL), lambda i, *_: (i, 0)),
            scratch_shapes=[pltpu.VMEM((2, MOE_SLOTS, D_MODEL), BF16), pltpu.SemaphoreType.DMA((2,))]),
        out_shape=jax.ShapeDtypeStruct((t, D_MODEL), F32),
        compiler_params=_cparams(("arbitrary",)),
        name="moe_combine",
    )(gdst, nch, route, pos, h, out_norm_w, ys)


def kernel(x, norm1_w, w_in, s5_lambda_re, s5_lambda_im, s5_log_dt, s5_b_re, s5_b_im, s5_c_re, s5_c_im, s5_d,
           s5_w_glu, ret_norm_w, w_branch, w_out, norm2_w, w_group, b_group, w_router, b_router, w_gate, w_up,
           w_down, final_norm_w):
    batch, seq, _ = x.shape
    depth = w_in.shape[0]
    t = batch * seq
    h = x.reshape(t, D_MODEL).astype(F32)
    ret_tables = _ret_tables(seq)
    w_in, s5_w_glu, w_branch, w_out, w_gate, w_up, w_down = (
        w.astype(BF16) for w in (w_in, s5_w_glu, w_branch, w_out, w_gate, w_up, w_down))
    for layer in range(depth):
        qkv, u_s5, ret, gates = _inproj(h, norm1_w[layer].reshape(1, D_MODEL).astype(F32), w_in, layer)
        y_sb = _sb_attention(qkv, batch, seq)
        s5_ops = _s5_operators(s5_lambda_re[layer], s5_lambda_im[layer], s5_log_dt[layer], s5_b_re[layer],
                               s5_b_im[layer], s5_c_re[layer], s5_c_im[layer], s5_d[layer])
        y_s5 = _s5_ssm(u_s5, batch, seq, s5_ops)
        y_ret = _retention(ret, ret_norm_w[layer], batch, seq, ret_tables)
        w_r = jnp.concatenate([w_group[layer], w_router[layer].transpose(1, 0, 2).reshape(D_MODEL, N_EXPERTS)], axis=1)
        w_r = jnp.pad(w_r.astype(F32), ((0, 0), (0, LANES - w_r.shape[1])))
        wr_hi, wr_lo = _split_bf16(w_r)
        b_r = jnp.pad(jnp.concatenate([b_group[layer], b_router[layer].reshape(-1)]).astype(F32),
                      (0, LANES - N_GROUPS - N_EXPERTS)).reshape(1, LANES)
        h, hn, route = _merge(y_sb, y_s5, y_ret, gates, h, s5_w_glu, w_branch, w_out,
                              norm2_w[layer].reshape(1, D_MODEL).astype(F32), wr_hi, wr_lo, b_r, layer)
        h = _moe(hn, route, h, w_gate, w_up, w_down, layer, final_norm_w.reshape(1, D_MODEL).astype(F32),
                 out_norm=(layer == depth - 1))
    return h.reshape(batch, seq, D_MODEL).astype(x.dtype)
```

```python
import functools
import math

import jax
import jax.numpy as jnp
import numpy as np
from jax import lax
from jax.experimental import pallas as pl
from jax.experimental.pallas import tpu as pltpu

F32 = jnp.float32
BF16 = jnp.bfloat16

D_MODEL = 1024
SB_HEADS = 8
HEAD_DIM = 64
BRANCH_WIDTH = 512
S5_GROUPS = 32
S5_GROUP_CH = 16
S5_STATE = 64
RET_HEADS = 8
RET_CHUNK = 128
ROPE_BASE = 10000.0
N_BRANCH = 3
N_GROUPS = 4
EXPERTS_PER_GROUP = 8
N_EXPERTS = 32
TOP_K = 2
EXPERT_FF = 256
EPS = 1e-6

LANES = 128
VMEM_LIMIT = 56 * 1024 * 1024
S5_CHUNK = 16
SB_BLOCK = 256
LOG2_E = 1.4426950408889634
SB_SKIP_LOG2 = -152.0
ROUTER_LANE0 = N_GROUPS


def _cparams(sem):
    return pltpu.CompilerParams(dimension_semantics=sem, vmem_limit_bytes=VMEM_LIMIT)


def _layer_block(stacked, layer):
    zeros = (0,) * (stacked.ndim - 1)
    return pl.BlockSpec((None,) + stacked.shape[1:], lambda *_: (layer,) + zeros, pipeline_mode=pl.Buffered(1))


def _split_bf16(x):
    hi = x.astype(BF16)
    lo = (x - hi.astype(F32)).astype(BF16)
    return hi, lo


def _dot(a, b):
    return jnp.dot(a, b, preferred_element_type=F32)


def _dot_nt(a, b):
    return lax.dot_general(a, b, (((1,), (1,)), ((), ())), preferred_element_type=F32)


def _dot_tn(a, b):
    return lax.dot_general(a, b, (((0,), (0,)), ((), ())), preferred_element_type=F32)


def _inproj_kernel(h_ref, nw_ref, w_ref, qkv_ref, u_ref, ret_ref, gate_ref):
    x = h_ref[...]
    ms = jnp.mean(x * x, axis=-1, keepdims=True)
    xn = (x * lax.rsqrt(ms + EPS) * nw_ref[...]).astype(BF16)
    off = 0
    for ref in (qkv_ref, u_ref, ret_ref, gate_ref):
        if ref is u_ref:
            u = _dot(xn, w_ref[:, off:off + BRANCH_WIDTH])
            for s in range(u_ref.shape[0]):
                u_ref[s] = u[:, s * LANES:(s + 1) * LANES]
            off += BRANCH_WIDTH
            continue
        width = ref.shape[1]
        for c in range(0, width, 512):
            ref[:, c:c + 512] = _dot(xn, w_ref[:, off + c:off + c + 512]).astype(ref.dtype)
        off += width


def _inproj(h, norm_w, w_in, layer, tm=512):
    t = h.shape[0]
    nslab = BRANCH_WIDTH // LANES

    def rows(w):
        return pl.BlockSpec((tm, w), lambda i: (i, 0))

    return pl.pallas_call(
        _inproj_kernel,
        grid=(t // tm,),
        in_specs=[rows(D_MODEL),
                  pl.BlockSpec((1, D_MODEL), lambda i: (0, 0)),
                  _layer_block(w_in, layer)],
        out_specs=[rows(3 * BRANCH_WIDTH), pl.BlockSpec((nslab, tm, LANES), lambda i: (0, i, 0)),
                   rows(4 * BRANCH_WIDTH), rows(N_BRANCH * D_MODEL)],
        out_shape=[jax.ShapeDtypeStruct((t, 3 * BRANCH_WIDTH), BF16),
                   jax.ShapeDtypeStruct((nslab, t, LANES), F32),
                   jax.ShapeDtypeStruct((t, 4 * BRANCH_WIDTH), BF16),
                   jax.ShapeDtypeStruct((t, N_BRANCH * D_MODEL), BF16)],
        compiler_params=_cparams(("parallel",)),
        name="inproj",
    )(h, norm_w, w_in)


def _sba_kernel(q_ref, k_ref, v_ref, m_ref, o_ref, acc_ref, car_ref, *, blk):
    i = pl.program_id(2)
    q = q_ref[0]
    lane = lax.broadcasted_iota(jnp.int32, (blk, LANES), 1)
    zero = jnp.zeros_like(q)
    q_heads = (jnp.where(lane < HEAD_DIM, q, zero), jnp.where(lane >= HEAD_DIM, q, zero))
    zscale = HEAD_DIM ** -0.5 * LOG2_E
    tri = m_ref[...]

    def keys(j):
        start = pl.multiple_of(j * blk, blk)
        return k_ref[0, pl.ds(start, blk), :], v_ref[0, pl.ds(start, blk), :]

    def log_terms(z2, causal):
        sp = jnp.log(1.0 + jnp.exp2(-jnp.abs(z2))) * LOG2_E
        log_beta = jnp.minimum(z2, 0.0) - sp
        log_1m = log_beta - z2
        if causal is not None:
            log_1m = jnp.where(causal, log_1m, 0.0)
        suffix = _dot(log_1m.astype(BF16), tri)
        return log_beta, suffix, jnp.sum(log_1m, axis=1, keepdims=True)

    def visit(block_ids, diagonal_first, car_in):
        kv = [keys(j) for j in block_ids]
        causal = None
        if diagonal_first:
            causal = lax.broadcasted_iota(jnp.int32, (blk, blk), 1) < lax.broadcasted_iota(jnp.int32, (blk, blk), 0)
        chains = [(b, hd) for b in range(len(block_ids)) for hd in range(2)]
        z2 = {c: _dot_nt(q_heads[c[1]], kv[c[0]][0]) * zscale for c in chains}
        terms = {c: log_terms(z2[c], causal if c[0] == 0 else None) for c in chains}
        out = []
        for hd in range(2):
            car = car_in[hd]
            acc = None
            for b in range(len(block_ids)):
                log_beta, suffix, tot = terms[(b, hd)]
                w = jnp.exp2(log_beta + suffix if car is None else log_beta + suffix + car)
                if causal is not None and b == 0:
                    w = jnp.where(causal, w, 0.0)
                pv = _dot(w.astype(BF16), kv[b][1])
                acc = pv if acc is None else acc + pv
                car = tot if car is None else car + tot
            out.append((acc, car))
        return out

    def first_visit(block_ids):
        for hd, (acc, car) in enumerate(visit(block_ids, True, (None, None))):
            acc_ref[hd] = acc
            car_ref[hd] = car

    pl.when(i == 0)(lambda: first_visit([i]))
    pl.when(i > 0)(lambda: first_visit([i, i - 1]))

    def mass_left():
        most = jnp.maximum(jnp.max(car_ref[0]), jnp.max(car_ref[1]))
        return (most > SB_SKIP_LOG2).astype(jnp.int32)

    def cond(c):
        j, go = c
        return jnp.logical_and(j >= 0, go > 0)

    def body(c):
        j, _ = c
        for hd, (acc, car) in enumerate(visit([j], False, (car_ref[0], car_ref[1]))):
            acc_ref[hd] += acc
            car_ref[hd] = car
        return j - 1, mass_left()

    lax.while_loop(cond, body, (i - 2, mass_left()))
    o_ref[0] = jnp.where(lane < HEAD_DIM, acc_ref[0], acc_ref[1]).astype(o_ref.dtype)


def _sb_attention(qkv, batch, seq, blk=SB_BLOCK):
    qkv3 = qkv.reshape(batch, seq, 3 * BRANCH_WIDTH)
    pairs = BRANCH_WIDTH // LANES
    r = np.arange(blk)
    m_ext = jnp.asarray(r[:, None] > r[None, :], BF16)
    out = pl.pallas_call(
        functools.partial(_sba_kernel, blk=blk),
        grid=(batch, pairs, seq // blk),
        in_specs=[pl.BlockSpec((1, blk, LANES), lambda b, p, i: (b, i, p)),
                  pl.BlockSpec((1, seq, LANES), lambda b, p, i: (b, 0, pairs + p)),
                  pl.BlockSpec((1, seq, LANES), lambda b, p, i: (b, 0, 2 * pairs + p)),
                  pl.BlockSpec(m_ext.shape, lambda b, p, i: (0, 0))],
        out_specs=pl.BlockSpec((1, blk, LANES), lambda b, p, i: (b, i, p)),
        out_shape=jax.ShapeDtypeStruct((batch, seq, BRANCH_WIDTH), BF16),
        scratch_shapes=[pltpu.VMEM((2, blk, LANES), F32), pltpu.VMEM((2, blk, 1), F32)],
        compiler_params=_cparams(("parallel", "parallel", "parallel")),
        name="sb_attention",
    )(qkv3, qkv3, qkv3, m_ext)
    return out.reshape(batch * seq, BRANCH_WIDTH)


def _s5_operators(lam_re, lam_im, log_dt, b_re, b_im, c_re, c_im, d_skip):
    L = S5_CHUNK
    g, p, c = S5_GROUPS, S5_STATE, S5_GROUP_CH
    dt = jnp.exp(log_dt.astype(F32))[:, None]
    lr = lam_re.astype(F32)
    li = lam_im.astype(F32)

    def a_pow(n):
        n = jnp.asarray(n, F32)[..., None, None]
        mag = jnp.exp(lr * dt * n)
        return mag * jnp.cos(li * dt * n), mag * jnp.sin(li * dt * n)

    ar, ai = a_pow(1.0)
    den = lr * lr + li * li
    nr = ar - 1.0
    fr = (nr * lr + ai * li) / den
    fi = (ai * lr - nr * li) / den
    bbr = fr[..., None] * b_re - fi[..., None] * b_im
    bbi = fr[..., None] * b_im + fi[..., None] * b_re
    cr = c_re.astype(F32)
    ci = c_im.astype(F32)

    pr, pi = a_pow(np.arange(L))
    car = cr[None] * pr[:, :, None, :] - ci[None] * pi[:, :, None, :]
    cai = cr[None] * pi[:, :, None, :] + ci[None] * pr[:, :, None, :]
    klag = jnp.einsum('jgop,gpi->jgoi', car, bbr) - jnp.einsum('jgop,gpi->jgoi', cai, bbi)
    klag = klag.at[0].add(d_skip.astype(F32).reshape(g, c)[:, :, None] * jnp.eye(c, dtype=F32))
    kflat = klag.transpose(1, 3, 0, 2).reshape(g, c, L * c)
    toe = jnp.stack([jnp.pad(kflat[:, :, :(L - s) * c], ((0, 0), (0, 0), (s * c, 0))) for s in range(L)], axis=1)
    toe = toe.reshape(g, L * c, L * c)

    qr, qi = a_pow(L - 1 - np.arange(L))
    p_re = qr[:, :, :, None] * bbr[None] - qi[:, :, :, None] * bbi[None]
    p_im = qr[:, :, :, None] * bbi[None] + qi[:, :, :, None] * bbr[None]
    p_re = p_re.transpose(1, 0, 3, 2).reshape(g, L * c, p)
    p_im = p_im.transpose(1, 0, 3, 2).reshape(g, L * c, p)

    er, ei = a_pow(1 + np.arange(L))
    q_re = (cr[None] * er[:, :, None, :] - ci[None] * ei[:, :, None, :])
    q_im = -(cr[None] * ei[:, :, None, :] + ci[None] * er[:, :, None, :])
    q_re = q_re.transpose(1, 3, 0, 2).reshape(g, p, L * c)
    q_im = q_im.transpose(1, 3, 0, 2).reshape(g, p, L * c)

    a_lr, a_li = a_pow(float(L))

    def pair_diag(m):
        gg, r, cc = m.shape
        m = m.reshape(gg // 2, 2, r, cc)
        z = jnp.zeros_like(m[:, 0])
        top = jnp.concatenate([m[:, 0], z], axis=2)
        bot = jnp.concatenate([z, m[:, 1]], axis=2)
        return jnp.concatenate([top, bot], axis=1)

    toe_p = pair_diag(toe).astype(BF16)
    p_p = jnp.concatenate([pair_diag(p_re), pair_diag(p_im)], axis=2).astype(BF16)
    q_p = jnp.concatenate([pair_diag(q_re), pair_diag(q_im)], axis=1).astype(BF16)
    a_re = a_lr.reshape(1, g * p)
    a_im = a_li.reshape(1, g * p)
    return toe_p, p_p, q_p, a_re, a_im


def _group_block_transpose(tiles):
    group = lax.broadcasted_iota(jnp.int32, tiles[0].shape, 1) // S5_GROUP_CH
    n = len(tiles)
    k = n // 2
    while k >= 1:
        upper = (group & k) != 0
        nxt = list(tiles)
        for i in range(n):
            if i & k == 0:
                lo, hi = tiles[i], tiles[i + k]
                nxt[i] = jnp.where(upper, pltpu.roll(hi, k * S5_GROUP_CH, 1), lo)
                nxt[i + k] = jnp.where(upper, hi, pltpu.roll(lo, LANES - k * S5_GROUP_CH, 1))
        tiles = nxt
        k //= 2
    return tiles


def _s5_state_in_kernel(u_ref, p_ref, up_ref, vre_ref, vim_ref, *, nb):
    L = S5_CHUNK
    per = LANES // S5_GROUP_CH
    for kb in range(u_ref.shape[0]):
        halves = []
        for a in range(L // per):
            steps = [u_ref[kb, pl.ds(a * per + b, nb, stride=L), :] for b in range(per)]
            halves.append(_group_block_transpose(steps))
        for pi in range(per // 2):
            pair = kb * (per // 2) + pi
            row = jnp.concatenate([halves[a][2 * pi + gi] for gi in range(2) for a in range(L // per)], axis=1)
            row = row.astype(BF16)
            up_ref[pair] = row
            v = _dot(row, p_ref[pair])
            vre_ref[:, pair * LANES:(pair + 1) * LANES] = v[:, :LANES]
            vim_ref[:, pair * LANES:(pair + 1) * LANES] = v[:, LANES:]


def _s5_scan_kernel(vre_ref, vim_ref, are_ref, aim_ref, sre_ref, sim_ref):
    n = vre_ref.shape[0]
    ar = are_ref[...]
    ai = aim_ref[...]

    def step(r, carry):
        xr, xi = carry
        sre_ref[pl.ds(r, 1), :] = xr
        sim_ref[pl.ds(r, 1), :] = xi
        vr = vre_ref[pl.ds(r, 1), :]
        vi = vim_ref[pl.ds(r, 1), :]
        return ar * xr - ai * xi + vr, ar * xi + ai * xr + vi

    z = jnp.zeros_like(ar)
    lax.fori_loop(0, n, step, (z, z))


def _s5_out_kernel(up_ref, sre_ref, sim_ref, toe_ref, q_ref, y_ref, *, nb):
    L = S5_CHUNK
    per = LANES // S5_GROUP_CH
    for kb in range(y_ref.shape[0]):
        halves = [[None] * per for _ in range(L // per)]
        for pi in range(per // 2):
            pair = kb * (per // 2) + pi
            q = q_ref[pair]
            cols = slice(pair * LANES, (pair + 1) * LANES)
            y = _dot(up_ref[pair], toe_ref[pair])
            y += _dot(sre_ref[:, cols].astype(BF16), q[:LANES]) + _dot(sim_ref[:, cols].astype(BF16), q[LANES:])
            y = jax.nn.gelu(y)
            for gi in range(2):
                for a in range(L // per):
                    j = gi * (L // per) + a
                    halves[a][2 * pi + gi] = y[:, j * LANES:(j + 1) * LANES]
        for a in range(L // per):
            steps = _group_block_transpose(halves[a])
            for b in range(per):
                y_ref[kb, pl.ds(a * per + b, nb, stride=L), :] = steps[b]


def _s5_ssm(u, batch, seq, ops, nb=128):
    toe_p, p_p, q_p, a_re, a_im = ops
    nslab, t, _ = u.shape
    L = S5_CHUNK
    npair = S5_GROUPS // 2
    nc = t // L
    row = 2 * L * S5_GROUP_CH
    ncols = S5_GROUPS * S5_STATE
    nb = min(nb, nc)

    def full(a):
        return pl.BlockSpec(a.shape, lambda i: (0,) * a.ndim)

    slab_spec = pl.BlockSpec((nslab, nb * L, LANES), lambda i: (0, i, 0))
    up_spec = pl.BlockSpec((npair, nb, row), lambda i: (0, i, 0))
    st_spec = pl.BlockSpec((nb, ncols), lambda i: (i, 0))
    u_p, v_re, v_im = pl.pallas_call(
        functools.partial(_s5_state_in_kernel, nb=nb),
        grid=(nc // nb,),
        in_specs=[slab_spec, full(p_p)],
        out_specs=[up_spec, st_spec, st_spec],
        out_shape=[jax.ShapeDtypeStruct((npair, nc, row), BF16)] + [jax.ShapeDtypeStruct((nc, ncols), F32)] * 2,
        compiler_params=_cparams(("parallel",)),
        name="s5_chunk_state",
    )(u, p_p)
    ncb = nc // batch
    cw = 512
    s_re, s_im = pl.pallas_call(
        _s5_scan_kernel,
        grid=(batch, ncols // cw),
        in_specs=[pl.BlockSpec((ncb, cw), lambda b, j: (b, j)),
                  pl.BlockSpec((ncb, cw), lambda b, j: (b, j)),
                  pl.BlockSpec((1, cw), lambda b, j: (0, j)),
                  pl.BlockSpec((1, cw), lambda b, j: (0, j))],
        out_specs=[pl.BlockSpec((ncb, cw), lambda b, j: (b, j))] * 2,
        out_shape=[jax.ShapeDtypeStruct((nc, ncols), F32)] * 2,
        compiler_params=_cparams(("parallel", "parallel")),
        name="s5_scan",
    )(v_re, v_im, a_re, a_im)
    return pl.pallas_call(
        functools.partial(_s5_out_kernel, nb=nb),
        grid=(nc // nb,),
        in_specs=[up_spec, st_spec, st_spec, full(toe_p), full(q_p)],
        out_specs=slab_spec,
        out_shape=jax.ShapeDtypeStruct((nslab, t, LANES), F32),
        compiler_params=_cparams(("parallel",)),
        name="s5_out",
    )(u_p, s_re, s_im, toe_p, q_p)


def _ret_tables(seq):
    half = HEAD_DIM // 2
    pos = jnp.arange(seq, dtype=F32)
    inv = ROPE_BASE ** (-jnp.arange(half, dtype=F32) / half)
    ang = pos[:, None] * inv[None, :]
    cos = jnp.tile(jnp.cos(ang), (1, LANES // half))
    sin = jnp.sin(ang)
    sin = jnp.tile(jnp.concatenate([-sin, sin], axis=1), (1, LANES // HEAD_DIM))
    c = RET_CHUNK
    gamma = 1.0 - 2.0 ** (-5.0 - jnp.arange(RET_HEADS, dtype=F32))
    log_g = jnp.log(gamma)
    idx = jnp.arange(c, dtype=F32)
    diff = idx[:, None] - idx[None, :]
    intra = jnp.where(diff >= 0, jnp.exp(jnp.maximum(diff, 0.0)[None] * log_g[:, None, None]), 0.0)
    q_dec = jnp.exp((idx + 1.0)[None, :] * log_g[:, None])
    k_dec = jnp.exp((c - 1.0 - idx)[None, :] * log_g[:, None])
    ch_dec = jnp.exp(c * log_g)

    def lanes(x):
        x = x.reshape(RET_HEADS // 2, 2, c)
        return jnp.repeat(x.transpose(0, 2, 1), HEAD_DIM, axis=2)

    q_dec_l = lanes(q_dec)
    k_dec_l = lanes(k_dec)
    hp = RET_HEADS // 2
    blockdiag = np.kron(np.eye(2), np.ones((HEAD_DIM, HEAD_DIM)))
    st_dec = jnp.repeat(ch_dec.reshape(hp, 2), HEAD_DIM, axis=1)[:, :, None] * blockdiag[None]
    avg = jnp.asarray(blockdiag / HEAD_DIM, BF16)
    return cos, sin, intra, q_dec_l, k_dec_l, st_dec.astype(F32), avg


def _ret_kernel(q_ref, k_ref, v_ref, g_ref, cos_ref, sin_ref, intra_ref, qd_ref, kd_ref, sd_ref, avg_ref, nw_ref,
                o_ref, st_ref, *, nchunk):
    c = RET_CHUNK

    @pl.when(pl.program_id(2) == 0)
    def _():
        st_ref[...] = jnp.zeros_like(st_ref)

    tq = q_ref.shape[1]
    lane = lax.broadcasted_iota(jnp.int32, (tq, LANES), 1)
    first_half = (lane % HEAD_DIM) < (HEAD_DIM // 2)
    head0 = lax.broadcasted_iota(jnp.int32, (c, LANES), 1) < HEAD_DIM
    sd = sd_ref[0]
    avg = avg_ref[...]
    same_head = avg > 0
    cos = cos_ref[...]
    sin = sin_ref[...]

    def rotary(x):
        swapped = jnp.where(first_half, pltpu.roll(x, LANES - HEAD_DIM // 2, 1), pltpu.roll(x, HEAD_DIM // 2, 1))
        return x * cos + swapped * sin

    q_all = rotary(q_ref[0].astype(F32)).astype(BF16)
    k_all = rotary(k_ref[0].astype(F32)) * (HEAD_DIM ** -0.5)
    kb_all = k_all.astype(BF16)
    zero = jnp.zeros((c, LANES), BF16)
    chunks = [slice(ci * c, (ci + 1) * c) for ci in range(nchunk)]
    vs = [v_ref[0, rows, :] for rows in chunks]
    scores, incs = [], []
    for ci, rows in enumerate(chunks):
        qb = q_all[rows]
        q2 = jnp.concatenate([jnp.where(head0, qb, zero), jnp.where(head0, zero, qb)], axis=0)
        scores.append(_dot_nt(q2, kb_all[rows]))
        kd_t = (k_all[rows] * kd_ref[0]).T.astype(BF16)
        incs.append(_dot(kd_t, vs[ci]))
    state = st_ref[...]
    states = []
    for inc in incs:
        states.append(state.astype(BF16))
        state = state * sd + jnp.where(same_head, inc, 0.0)
    st_ref[...] = state
    intra2 = jnp.concatenate([intra_ref[0, 0], intra_ref[0, 1]], axis=0)
    outs = []
    for ci, rows in enumerate(chunks):
        o2 = _dot((scores[ci] * intra2).astype(BF16), vs[ci])
        outs.append(jnp.where(head0, o2[:c], o2[c:]) + _dot(q_all[rows], states[ci]) * qd_ref[0])
    out = jnp.concatenate(outs, axis=0)
    hi, lo = _split_bf16(out)
    mu = _dot(hi, avg) + _dot(lo, avg)
    dlt = out - mu
    hi, lo = _split_bf16(dlt * dlt)
    var = _dot(hi, avg) + _dot(lo, avg)
    gate = g_ref[0].astype(F32)
    y = dlt * lax.rsqrt(var + EPS) * nw_ref[...] * (gate * jax.nn.sigmoid(gate))
    o_ref[0] = y.astype(o_ref.dtype)


def _retention(ret, norm_w, batch, seq, tables, tq=1024):
    cos, sin, intra, q_dec_l, k_dec_l, st_dec, avg = tables
    pairs = BRANCH_WIDTH // LANES
    ret3 = ret.reshape(batch, seq, 4 * BRANCH_WIDTH)
    tq = min(tq, seq)
    c = RET_CHUNK

    def col(k):
        return pl.BlockSpec((1, tq, LANES), lambda b, p, i: (b, i, k * pairs + p))

    out = pl.pallas_call(
        functools.partial(_ret_kernel, nchunk=tq // c),
        grid=(batch, pairs, seq // tq),
        in_specs=[col(0), col(1), col(2), col(3),
                  pl.BlockSpec((tq, LANES), lambda b, p, i: (i, 0)),
                  pl.BlockSpec((tq, LANES), lambda b, p, i: (i, 0)),
                  pl.BlockSpec((1, 2, c, c), lambda b, p, i: (p, 0, 0, 0)),
                  pl.BlockSpec((1, c, LANES), lambda b, p, i: (p, 0, 0)),
                  pl.BlockSpec((1, c, LANES), lambda b, p, i: (p, 0, 0)),
                  pl.BlockSpec((1, LANES, LANES), lambda b, p, i: (p, 0, 0)),
                  pl.BlockSpec((LANES, LANES), lambda b, p, i: (0, 0)),
                  pl.BlockSpec((1, LANES), lambda b, p, i: (0, p))],
        out_specs=pl.BlockSpec((1, tq, LANES), lambda b, p, i: (b, i, p)),
        out_shape=jax.ShapeDtypeStruct((batch, seq, BRANCH_WIDTH), BF16),
        scratch_shapes=[pltpu.VMEM((LANES, LANES), F32)],
        compiler_params=_cparams(("parallel", "parallel", "arbitrary")),
        name="retention",
    )(ret3, ret3, ret3, ret3, cos, sin, intra.reshape(pairs, 2, c, c), q_dec_l, k_dec_l, st_dec, avg,
      norm_w.reshape(1, BRANCH_WIDTH).astype(F32))
    return out.reshape(batch * seq, BRANCH_WIDTH)


def _route(logits):
    lane = lax.broadcasted_iota(jnp.int32, logits.shape, 1).astype(F32)
    neg = jnp.float32(-jnp.inf)
    big = jnp.float32(1 << 20)
    gl = jnp.where(lane < N_GROUPS, logits, neg)
    gmax = jnp.max(gl, axis=1, keepdims=True)
    gsum = jnp.sum(jnp.exp(gl - gmax), axis=1, keepdims=True)
    g_w = 1.0 / gsum
    g_idx = jnp.min(jnp.where(gl == gmax, lane, big), axis=1, keepdims=True)
    lo = ROUTER_LANE0 + EXPERTS_PER_GROUP * g_idx
    sl = jnp.where((lane >= lo) & (lane < lo + EXPERTS_PER_GROUP), logits, neg)
    m1 = jnp.max(sl, axis=1, keepdims=True)
    i1 = jnp.min(jnp.where(sl == m1, lane, big), axis=1, keepdims=True)
    sl2 = jnp.where(lane == i1, neg, sl)
    m2 = jnp.max(sl2, axis=1, keepdims=True)
    i2 = jnp.min(jnp.where(sl2 == m2, lane, big), axis=1, keepdims=True)
    e2 = jnp.exp(m2 - m1)
    w1 = g_w / (1.0 + e2)
    w2 = g_w * e2 / (1.0 + e2)
    packed = jnp.where(lane == 0.0, i1 - ROUTER_LANE0, jnp.where(lane == 1.0, i2 - ROUTER_LANE0, 0.0))
    return packed + jnp.where(lane == 2.0, w1, 0.0) + jnp.where(lane == 3.0, w2, 0.0)


def _merge_kernel(ysb_ref, ys5_ref, yret_ref, gate_ref, h_ref, wglu_ref, wbr_ref, wout_ref, n2_ref, wr_hi_ref,
                  wr_lo_ref, br_ref, h_out_ref, hn_ref, route_ref):
    ys5 = jnp.concatenate([ys5_ref[s] for s in range(ys5_ref.shape[0])], axis=1).astype(BF16)
    glu = _dot(ys5, wglu_ref[...])
    y_s5 = (glu[:, :BRANCH_WIDTH] * jax.nn.sigmoid(glu[:, BRANCH_WIDTH:])).astype(BF16)
    merged = None
    for n, y in enumerate((ysb_ref[...], y_s5, yret_ref[...])):
        g = jax.nn.sigmoid(gate_ref[:, n * D_MODEL:(n + 1) * D_MODEL].astype(F32))
        term = g * _dot(y, wbr_ref[n])
        merged = term if merged is None else merged + term
    h = h_ref[...] + _dot(merged.astype(BF16), wout_ref[...])
    h_out_ref[...] = h
    ms = jnp.mean(h * h, axis=-1, keepdims=True)
    hn = h * lax.rsqrt(ms + EPS) * n2_ref[...]
    hn_ref[...] = hn.astype(BF16)
    hi, lo = _split_bf16(hn)
    logits = _dot(hi, wr_hi_ref[...]) + _dot(hi, wr_lo_ref[...]) + _dot(lo, wr_hi_ref[...]) + br_ref[...]
    route_ref[...] = _route(logits)


def _merge(y_sb, y_s5, y_ret, gates, h, w_glu, w_br, w_out, norm2_w, wr_hi, wr_lo, b_r, layer, tm=512):
    t = h.shape[0]

    def rows(w):
        return pl.BlockSpec((tm, w), lambda i: (i, 0))

    def full(a):
        return pl.BlockSpec(a.shape, lambda i: (0,) * a.ndim)

    return pl.pallas_call(
        _merge_kernel,
        grid=(t // tm,),
        in_specs=[rows(BRANCH_WIDTH), pl.BlockSpec((y_s5.shape[0], tm, LANES), lambda i: (0, i, 0)),
                  rows(BRANCH_WIDTH), rows(N_BRANCH * D_MODEL),
                  rows(D_MODEL), _layer_block(w_glu, layer), _layer_block(w_br, layer), _layer_block(w_out, layer),
                  full(norm2_w), full(wr_hi), full(wr_lo), full(b_r)],
        out_specs=[rows(D_MODEL), rows(D_MODEL), rows(LANES)],
        out_shape=[jax.ShapeDtypeStruct((t, D_MODEL), F32), jax.ShapeDtypeStruct((t, D_MODEL), BF16),
                   jax.ShapeDtypeStruct((t, LANES), F32)],
        compiler_params=_cparams(("parallel",)),
        name="merge_route",
    )(y_sb, y_s5, y_ret, gates, h, w_glu, w_br, w_out, norm2_w, wr_hi, wr_lo, b_r)


MOE_TILE = 512
MOE_ROW_TILE = 512
SEG_ALIGN = 16
MOE_SLOTS = TOP_K * MOE_TILE + N_EXPERTS * SEG_ALIGN
MOE_CHUNKS = MOE_SLOTS // SEG_ALIGN


def _moe_plan_kernel(route_ref, tri_ref, cum_ref, pos_ref, cnt_ref):
    r = route_ref[...]
    tm = r.shape[0]
    lane = lax.broadcasted_iota(jnp.int32, r.shape, 1).astype(F32)
    e1 = r[:, 0:1]
    e2 = r[:, 1:2]
    chosen = (lane == e1) | (lane == e2)
    onehot = jnp.where(chosen, 1.0, 0.0).astype(BF16)
    incl = _dot(tri_ref[...], onehot)
    cnt = incl[tm - 1:tm, :]
    padded = jnp.floor((cnt + (SEG_ALIGN - 1)) * (1.0 / SEG_ALIGN)) * SEG_ALIGN
    start = _dot(jnp.broadcast_to(padded, (8, LANES)).astype(BF16), cum_ref[...])[0:1, :]
    slot = start + incl - 1.0
    p1 = jnp.sum(jnp.where(lane == e1, slot, 0.0), axis=1, keepdims=True)
    p2 = jnp.sum(jnp.where(lane == e2, slot, 0.0), axis=1, keepdims=True)
    pos_ref[...] = jnp.where(lane == 0.0, p1, jnp.where(lane == 1.0, p2, 0.0))
    cnt_ref[0] = jnp.broadcast_to(cnt, (8, LANES))


def _moe_plan(route, tm):
    t = route.shape[0]
    r = np.arange(tm)
    tri = jnp.asarray(r[None, :] <= r[:, None], BF16)
    l = np.arange(LANES)
    cum = jnp.asarray(l[:, None] < l[None, :], BF16)
    return pl.pallas_call(
        _moe_plan_kernel,
        grid=(t // tm,),
        in_specs=[pl.BlockSpec((tm, LANES), lambda i: (i, 0)),
                  pl.BlockSpec((tm, tm), lambda i: (0, 0)),
                  pl.BlockSpec((LANES, LANES), lambda i: (0, 0))],
        out_specs=[pl.BlockSpec((tm, LANES), lambda i: (i, 0)), pl.BlockSpec((1, 8, LANES), lambda i: (i, 0, 0))],
        out_shape=[jax.ShapeDtypeStruct((t, LANES), F32), jax.ShapeDtypeStruct((t // tm, 8, LANES), F32)],
        compiler_params=_cparams(("parallel",)),
        name="moe_plan",
    )(route, tri, cum)


def _moe_tables(cnt, rt, n_rows):
    a = SEG_ALIGN
    ntile = cnt.shape[0]
    p = (cnt + a - 1) // a * a
    lend = jnp.cumsum(p, axis=1)
    lstart = lend - p
    seg = p.sum(axis=0)
    segpad = (seg + rt - 1) // rt * rt
    gend = jnp.cumsum(segpad)
    gstart = gend - segpad
    toff = gstart[None, :] + jnp.cumsum(p, axis=0) - p
    cstart = a * jnp.arange(MOE_CHUNKS, dtype=jnp.int32)
    in_run = (cstart[None, :, None] >= lstart[:, None, :]) & (cstart[None, :, None] < lend[:, None, :])
    gdst = (jnp.sum(jnp.where(in_run, (toff - lstart)[:, None, :], 0), axis=2) + cstart[None, :]) // a
    nch = lend[:, -1] // a
    tstart = rt * jnp.arange(n_rows // rt, dtype=jnp.int32)
    tile_e = jnp.minimum(jnp.sum(tstart[:, None] >= gend[None, :], axis=1), N_EXPERTS - 1)
    n_used = gend[-1] // rt
    zmax = rt // a - 1
    zper = -(-N_EXPERTS * zmax // ntile)
    k = jnp.arange(zmax, dtype=jnp.int32)
    zd = ((gstart + seg) // a)[:, None] + k[None, :]
    valid = (k[None, :] < ((segpad - seg) // a)[:, None]).reshape(-1)
    order = jnp.argsort(jnp.logical_not(valid), stable=True)
    zdst = jnp.pad(zd.reshape(-1)[order], (0, ntile * zper - N_EXPERTS * zmax))
    nz = valid.sum()
    i32 = jnp.int32
    return (gdst.reshape(-1).astype(i32), nch.astype(i32), zdst.astype(i32), nz.reshape(1).astype(i32),
            tile_e.astype(i32), n_used.reshape(1).astype(i32)), zper


def _moe_dispatch_kernel(gdst_ref, nch_ref, zdst_ref, nz_ref, n_used_ref, pos_ref, hn_ref, xs_hbm, stg, zblk, sem,
                         zsem, *, zper):
    i = pl.program_id(0)
    last_step = pl.num_programs(0) - 1
    buf = i % 2
    tm = hn_ref.shape[0]
    pos_t = pos_ref[...].T
    p1 = pos_t[0:1, :]
    p2 = pos_t[1:2, :]
    x = hn_ref[...]
    rb = MOE_ROW_TILE
    for b in range(MOE_SLOTS // rb):
        slot = (lax.broadcasted_iota(jnp.int32, (rb, tm), 0) + b * rb).astype(F32)
        onehot = jnp.where((slot == p1) | (slot == p2), 1.0, 0.0).astype(BF16)
        stg[buf, b * rb:(b + 1) * rb, :] = _dot(onehot, x).astype(BF16)

    def chunk_copy(which, src_chunk, dst_chunk):
        src = pl.multiple_of(src_chunk * SEG_ALIGN, SEG_ALIGN)
        dst = pl.multiple_of(dst_chunk * SEG_ALIGN, SEG_ALIGN)
        return pltpu.make_async_copy(stg.at[which, pl.ds(src, SEG_ALIGN)], xs_hbm.at[pl.ds(dst, SEG_ALIGN)],
                                     sem.at[which])

    def n_copies(step):
        return nch_ref[step], jnp.clip(nz_ref[0] - step * zper, 0, zper)

    def drain(which, step):
        for count in n_copies(step):
            rows = count * SEG_ALIGN

            @pl.when(rows > 0)
            def _():
                pltpu.make_async_copy(stg.at[which, pl.ds(0, rows)], xs_hbm.at[pl.ds(0, rows)],
                                      sem.at[which]).wait()

    n, nzero = n_copies(i)

    @pl.loop(0, n)
    def _(c):
        chunk_copy(buf, c, gdst_ref[i * MOE_CHUNKS + c]).start()

    @pl.loop(0, nzero)
    def _(k):
        chunk_copy(buf, MOE_CHUNKS - 1, zdst_ref[i * zper + k]).start()

    pl.when(i > 0)(lambda: drain(1 - buf, i - 1))

    @pl.when(i == last_step)
    def _():
        drain(buf, i)
        rt = zblk.shape[0]
        zblk[...] = jnp.zeros_like(zblk)
        first = n_used_ref[0]
        last = xs_hbm.shape[0] // rt

        def tile_copy(j):
            return pltpu.make_async_copy(zblk, xs_hbm.at[pl.ds(pl.multiple_of(j * rt, rt), rt)], zsem)

        @pl.loop(first, last)
        def _(j):
            tile_copy(j).start()

        @pl.loop(first, last)
        def _(j):
            tile_copy(0).wait()


def _moe_expert_kernel(tile_e_ref, n_used_ref, xs_ref, wg_ref, wu_ref, wd_ref, ys_ref):
    used = pl.program_id(0) < n_used_ref[0]

    @pl.when(used)
    def _():
        x = xs_ref[...]
        gate = _dot(x, wg_ref[0].astype(BF16))
        up = _dot(x, wu_ref[0].astype(BF16))
        hdn = gate * jax.nn.sigmoid(gate) * up
        ys_ref[...] = _dot(hdn.astype(BF16), wd_ref[0].astype(BF16)).astype(ys_ref.dtype)

    @pl.when(jnp.logical_not(used))
    def _():
        ys_ref[...] = jnp.zeros_like(ys_ref)


def _moe_combine_kernel(gdst_ref, nch_ref, route_ref, pos_ref, h_ref, nw_ref, ys_hbm, o_ref, stg, sem, *, out_norm):
    i = pl.program_id(0)
    nsteps = pl.num_programs(0)
    buf = i % 2
    tm = h_ref.shape[0]

    def chunk_copy(which, src_chunk, dst_chunk):
        src = pl.multiple_of(src_chunk * SEG_ALIGN, SEG_ALIGN)
        dst = pl.multiple_of(dst_chunk * SEG_ALIGN, SEG_ALIGN)
        return pltpu.make_async_copy(ys_hbm.at[pl.ds(src, SEG_ALIGN)], stg.at[which, pl.ds(dst, SEG_ALIGN)],
                                     sem.at[which])

    def fetch(step, which):
        n = nch_ref[step]

        @pl.loop(0, n)
        def _(c):
            chunk_copy(which, gdst_ref[step * MOE_CHUNKS + c], c).start()

        @pl.loop(n, MOE_CHUNKS)
        def _(c):
            rows = pl.ds(pl.multiple_of(c * SEG_ALIGN, SEG_ALIGN), SEG_ALIGN)
            stg[which, rows, :] = jnp.zeros((SEG_ALIGN, D_MODEL), BF16)

    pl.when(i == 0)(lambda: fetch(0, 0))
    pl.when(i + 1 < nsteps)(lambda: fetch(i + 1, 1 - buf))

    r = route_ref[...]
    pos = pos_ref[...]
    slot = lax.broadcasted_iota(jnp.int32, (tm, MOE_SLOTS), 1).astype(F32)
    w = jnp.where(slot == pos[:, 0:1], r[:, 2:3], 0.0) + jnp.where(slot == pos[:, 1:2], r[:, 3:4], 0.0)

    rows = nch_ref[i] * SEG_ALIGN

    @pl.when(rows > 0)
    def _():
        pltpu.make_async_copy(ys_hbm.at[pl.ds(0, rows)], stg.at[buf, pl.ds(0, rows)], sem.at[buf]).wait()

    out = h_ref[...] + _dot(w.astype(BF16), stg[buf])
    if out_norm:
        ms = jnp.mean(out * out, axis=-1, keepdims=True)
        out = out * lax.rsqrt(ms + EPS) * nw_ref[...]
    o_ref[...] = out


def _moe(hn, route, h, w_gate, w_up, w_down, layer, out_norm_w, out_norm):
    t = h.shape[0]
    tm = min(MOE_TILE, t)
    rt = MOE_ROW_TILE
    ntile = t // tm
    n_rows = TOP_K * t + ntile * N_EXPERTS * (SEG_ALIGN - 1) + N_EXPERTS * (rt - SEG_ALIGN)
    n_rows = (n_rows + rt - 1) // rt * rt
    pos, cnt = _moe_plan(route, tm)
    (gdst, nch, zdst, nz, tile_e, n_used), zper = _moe_tables(cnt[:, 0, :N_EXPERTS].astype(jnp.int32), rt, n_rows)
    xs = pl.pallas_call(
        functools.partial(_moe_dispatch_kernel, zper=zper),
        grid_spec=pltpu.PrefetchScalarGridSpec(
            num_scalar_prefetch=5,
            grid=(ntile,),
            in_specs=[pl.BlockSpec((tm, LANES), lambda i, *_: (i, 0)),
                      pl.BlockSpec((tm, D_MODEL), lambda i, *_: (i, 0))],
            out_specs=pl.BlockSpec(memory_space=pl.ANY),
            scratch_shapes=[pltpu.VMEM((2, MOE_SLOTS, D_MODEL), BF16), pltpu.VMEM((rt, D_MODEL), BF16),
                            pltpu.SemaphoreType.DMA((2,)), pltpu.SemaphoreType.DMA(())]),
        out_shape=jax.ShapeDtypeStruct((n_rows, D_MODEL), BF16),
        compiler_params=_cparams(("arbitrary",)),
        name="moe_dispatch",
    )(gdst, nch, zdst, nz, n_used, pos, hn)
    ys = pl.pallas_call(
        _moe_expert_kernel,
        grid_spec=pltpu.PrefetchScalarGridSpec(
            num_scalar_prefetch=2,
            grid=(n_rows // rt,),
            in_specs=[pl.BlockSpec((rt, D_MODEL), lambda j, te, nu: (jnp.minimum(j, nu[0] - 1), 0)),
                      pl.BlockSpec((None, 1, D_MODEL, EXPERT_FF), lambda j, te, nu: (layer, te[j], 0, 0)),
                      pl.BlockSpec((None, 1, D_MODEL, EXPERT_FF), lambda j, te, nu: (layer, te[j], 0, 0)),
                      pl.BlockSpec((None, 1, EXPERT_FF, D_MODEL), lambda j, te, nu: (layer, te[j], 0, 0))],
            out_specs=pl.BlockSpec((rt, D_MODEL), lambda j, te, nu: (j, 0))),
        out_shape=jax.ShapeDtypeStruct((n_rows, D_MODEL), BF16),
        compiler_params=_cparams(("arbitrary",)),
        name="moe_experts",
    )(tile_e, n_used, xs, w_gate, w_up, w_down)
    return pl.pallas_call(
        functools.partial(_moe_combine_kernel, out_norm=out_norm),
        grid_spec=pltpu.PrefetchScalarGridSpec(
            num_scalar_prefetch=2,
            grid=(ntile,),
            in_specs=[pl.BlockSpec((tm, LANES), lambda i, *_: (i, 0)),
                      pl.BlockSpec((tm, LANES), lambda i, *_: (i, 0)),
                      pl.BlockSpec((tm, D_MODEL), lambda i, *_: (i, 0)),
                      pl.BlockSpec((1, D_MODEL), lambda i, *_: (0, 0)),
                      pl.BlockSpec(memory_space=pl.ANY)],
            out_specs=pl.BlockSpec((tm, D_MODEL), lambda i, *_: (i, 0)),
            scratch_shapes=[pltpu.VMEM((2, MOE_SLOTS, D_MODEL), BF16), pltpu.SemaphoreType.DMA((2,))]),
        out_shape=jax.ShapeDtypeStruct((t, D_MODEL), F32),
        compiler_params=_cparams(("arbitrary",)),
        name="moe_combine",
    )(gdst, nch, route, pos, h, out_norm_w, ys)


def kernel(x, norm1_w, w_in, s5_lambda_re, s5_lambda_im, s5_log_dt, s5_b_re, s5_b_im, s5_c_re, s5_c_im, s5_d,
           s5_w_glu, ret_norm_w, w_branch, w_out, norm2_w, w_group, b_group, w_router, b_router, w_gate, w_up,
           w_down, final_norm_w):
    batch, seq, _ = x.shape
    depth = w_in.shape[0]
    t = batch * seq
    h = x.reshape(t, D_MODEL).astype(F32)
    ret_tables = _ret_tables(seq)
    w_in, s5_w_glu, w_branch, w_out = (w.astype(BF16) for w in (w_in, s5_w_glu, w_branch, w_out))
    for layer in range(depth):
        qkv, u_s5, ret, gates = _inproj(h, norm1_w[layer].reshape(1, D_MODEL).astype(F32), w_in, layer)
        y_sb = _sb_attention(qkv, batch, seq)
        s5_ops = _s5_operators(s5_lambda_re[layer], s5_lambda_im[layer], s5_log_dt[layer], s5_b_re[layer],
                               s5_b_im[layer], s5_c_re[layer], s5_c_im[layer], s5_d[layer])
        y_s5 = _s5_ssm(u_s5, batch, seq, s5_ops)
        y_ret = _retention(ret, ret_norm_w[layer], batch, seq, ret_tables)
        w_r = jnp.concatenate([w_group[layer], w_router[layer].transpose(1, 0, 2).reshape(D_MODEL, N_EXPERTS)], axis=1)
        w_r = jnp.pad(w_r.astype(F32), ((0, 0), (0, LANES - w_r.shape[1])))
        wr_hi, wr_lo = _split_bf16(w_r)
        b_r = jnp.pad(jnp.concatenate([b_group[layer], b_router[layer].reshape(-1)]).astype(F32),
                      (0, LANES - N_GROUPS - N_EXPERTS)).reshape(1, LANES)
        h, hn, route = _merge(y_sb, y_s5, y_ret, gates, h, s5_w_glu, w_branch, w_out,
                              norm2_w[layer].reshape(1, D_MODEL).astype(F32), wr_hi, wr_lo, b_r, layer)
        h = _moe(hn, route, h, w_gate, w_up, w_down, layer, final_norm_w.reshape(1, D_MODEL).astype(F32),
                 out_norm=(layer == depth - 1))
    return h.reshape(batch, seq, D_MODEL).astype(x.dtype)
```

```python
import functools
import math

import jax
import jax.numpy as jnp
import numpy as np
from jax import lax
from jax.experimental import pallas as pl
from jax.experimental.pallas import tpu as pltpu

F32 = jnp.float32
BF16 = jnp.bfloat16

D_MODEL = 1024
SB_HEADS = 8
HEAD_DIM = 64
BRANCH_WIDTH = 512
S5_GROUPS = 32
S5_GROUP_CH = 16
S5_STATE = 64
RET_HEADS = 8
RET_CHUNK = 128
ROPE_BASE = 10000.0
N_BRANCH = 3
N_GROUPS = 4
EXPERTS_PER_GROUP = 8
N_EXPERTS = 32
TOP_K = 2
EXPERT_FF = 256
EPS = 1e-6

LANES = 128
VMEM_LIMIT = 56 * 1024 * 1024
S5_CHUNK = 16
SB_BLOCK = 256
SB_SUBBLOCKS = 2
LOG2_E = 1.4426950408889634
SB_SKIP_LOG2 = -152.0
ROUTER_LANE0 = N_GROUPS


def _cparams(sem):
    return pltpu.CompilerParams(dimension_semantics=sem, vmem_limit_bytes=VMEM_LIMIT)


def _layer_block(stacked, layer):
    zeros = (0,) * (stacked.ndim - 1)
    return pl.BlockSpec((None,) + stacked.shape[1:], lambda *_: (layer,) + zeros, pipeline_mode=pl.Buffered(1))


def _split_bf16(x):
    hi = x.astype(BF16)
    lo = (x - hi.astype(F32)).astype(BF16)
    return hi, lo


def _dot(a, b):
    return jnp.dot(a, b, preferred_element_type=F32)


def _dot_nt(a, b):
    return lax.dot_general(a, b, (((1,), (1,)), ((), ())), preferred_element_type=F32)


def _dot_tn(a, b):
    return lax.dot_general(a, b, (((0,), (0,)), ((), ())), preferred_element_type=F32)


def _inproj_kernel(h_ref, nw_ref, w_ref, qkv_ref, u_ref, ret_ref, gate_ref):
    x = h_ref[...]
    ms = jnp.mean(x * x, axis=-1, keepdims=True)
    xn = (x * lax.rsqrt(ms + EPS) * nw_ref[...]).astype(BF16)
    off = 0
    for ref in (qkv_ref, u_ref, ret_ref, gate_ref):
        if ref is u_ref:
            u = _dot(xn, w_ref[:, off:off + BRANCH_WIDTH])
            for s in range(u_ref.shape[0]):
                u_ref[s] = u[:, s * LANES:(s + 1) * LANES]
            off += BRANCH_WIDTH
            continue
        width = ref.shape[1]
        for c in range(0, width, 512):
            ref[:, c:c + 512] = _dot(xn, w_ref[:, off + c:off + c + 512]).astype(ref.dtype)
        off += width


def _inproj(h, norm_w, w_in, layer, tm=512):
    t = h.shape[0]
    nslab = BRANCH_WIDTH // LANES

    def rows(w):
        return pl.BlockSpec((tm, w), lambda i: (i, 0))

    return pl.pallas_call(
        _inproj_kernel,
        grid=(t // tm,),
        in_specs=[rows(D_MODEL),
                  pl.BlockSpec((1, D_MODEL), lambda i: (0, 0)),
                  _layer_block(w_in, layer)],
        out_specs=[rows(3 * BRANCH_WIDTH), pl.BlockSpec((nslab, tm, LANES), lambda i: (0, i, 0)),
                   rows(4 * BRANCH_WIDTH), rows(N_BRANCH * D_MODEL)],
        out_shape=[jax.ShapeDtypeStruct((t, 3 * BRANCH_WIDTH), BF16),
                   jax.ShapeDtypeStruct((nslab, t, LANES), F32),
                   jax.ShapeDtypeStruct((t, 4 * BRANCH_WIDTH), BF16),
                   jax.ShapeDtypeStruct((t, N_BRANCH * D_MODEL), BF16)],
        compiler_params=_cparams(("parallel",)),
        name="inproj",
    )(h, norm_w, w_in)


def _sba_kernel(q_ref, k_ref, v_ref, m_ref, o_ref, acc_ref, car_ref, *, blk, nsub):
    step = pl.program_id(2)
    lane = lax.broadcasted_iota(jnp.int32, (blk, LANES), 1)
    zscale = HEAD_DIM ** -0.5 * LOG2_E
    tri = m_ref[...]
    q_heads = []
    for sub in range(nsub):
        q = q_ref[0, sub * blk:(sub + 1) * blk, :]
        zero = jnp.zeros_like(q)
        q_heads.append((jnp.where(lane < HEAD_DIM, q, zero), jnp.where(lane >= HEAD_DIM, q, zero)))

    def keys(j):
        start = pl.multiple_of(j * blk, blk)
        return k_ref[0, pl.ds(start, blk), :], v_ref[0, pl.ds(start, blk), :]

    def log_terms(z2, causal):
        sp = jnp.log(1.0 + jnp.exp2(-jnp.abs(z2))) * LOG2_E
        log_beta = jnp.minimum(z2, 0.0) - sp
        log_1m = log_beta - z2
        if causal is not None:
            log_1m = jnp.where(causal, log_1m, 0.0)
        suffix = _dot(log_1m.astype(BF16), tri)
        return log_beta, suffix, jnp.sum(log_1m, axis=1, keepdims=True)

    def visit(key_blocks, jobs):
        causal = lax.broadcasted_iota(jnp.int32, (blk, blk), 1) < lax.broadcasted_iota(jnp.int32, (blk, blk), 0)
        kv = [keys(j) for j in key_blocks]
        chains = [(n, b, hd) for n, job in enumerate(jobs) for b in range(len(job[1])) for hd in range(2)]

        def mask(n, b):
            return causal if (jobs[n][2] and b == 0) else None

        z2 = {c: _dot_nt(q_heads[jobs[c[0]][0]][c[2]], kv[jobs[c[0]][1][c[1]]][0]) * zscale for c in chains}
        terms = {c: log_terms(z2[c], mask(c[0], c[1])) for c in chains}
        results = []
        for n, (sub, block_ids, _, car_in) in enumerate(jobs):
            per_head = []
            for hd in range(2):
                car = None if car_in is None else car_in[hd]
                acc = None
                for b in range(len(block_ids)):
                    log_beta, suffix, tot = terms[(n, b, hd)]
                    w = jnp.exp2(log_beta + suffix if car is None else log_beta + suffix + car)
                    if mask(n, b) is not None:
                        w = jnp.where(causal, w, 0.0)
                    pv = _dot(w.astype(BF16), kv[block_ids[b]][1])
                    acc = pv if acc is None else acc + pv
                    car = tot if car is None else car + tot
                per_head.append((acc, car))
            results.append(per_head)
        return results

    def first_visit(first_step):
        first = step * nsub
        key_blocks = [first + sub for sub in range(nsub)] + ([] if first_step else [first - 1])
        jobs = []
        for sub in range(nsub):
            ids = [sub]
            if sub > 0:
                ids.append(sub - 1)
            elif not first_step:
                ids.append(nsub)
            jobs.append((sub, ids, True, None))
        for sub, per_head in enumerate(visit(key_blocks, jobs)):
            for hd, (acc, car) in enumerate(per_head):
                acc_ref[sub, hd] = acc
                car_ref[sub, hd] = car

    pl.when(step == 0)(lambda: first_visit(True))
    pl.when(step > 0)(lambda: first_visit(False))

    for sub in range(nsub):
        def mass_left(sub=sub):
            most = jnp.maximum(jnp.max(car_ref[sub, 0]), jnp.max(car_ref[sub, 1]))
            return (most > SB_SKIP_LOG2).astype(jnp.int32)

        def cond(c):
            j, go = c
            return jnp.logical_and(j >= 0, go > 0)

        def body(c, sub=sub, mass_left=mass_left):
            j, _ = c
            (per_head,) = visit([j], [(sub, [0], False, (car_ref[sub, 0], car_ref[sub, 1]))])
            for hd, (acc, car) in enumerate(per_head):
                acc_ref[sub, hd] += acc
                car_ref[sub, hd] = car
            return j - 1, mass_left()

        lax.while_loop(cond, body, (step * nsub + sub - 2, mass_left()))
        o_ref[0, sub * blk:(sub + 1) * blk, :] = jnp.where(lane < HEAD_DIM, acc_ref[sub, 0],
                                                           acc_ref[sub, 1]).astype(o_ref.dtype)


def _sb_attention(qkv, batch, seq, blk=SB_BLOCK, nsub=SB_SUBBLOCKS):
    qkv3 = qkv.reshape(batch, seq, 3 * BRANCH_WIDTH)
    pairs = BRANCH_WIDTH // LANES
    tq = blk * nsub
    r = np.arange(blk)
    m_ext = jnp.asarray(r[:, None] > r[None, :], BF16)
    out = pl.pallas_call(
        functools.partial(_sba_kernel, blk=blk, nsub=nsub),
        grid=(batch, pairs, seq // tq),
        in_specs=[pl.BlockSpec((1, tq, LANES), lambda b, p, i: (b, i, p)),
                  pl.BlockSpec((1, seq, LANES), lambda b, p, i: (b, 0, pairs + p)),
                  pl.BlockSpec((1, seq, LANES), lambda b, p, i: (b, 0, 2 * pairs + p)),
                  pl.BlockSpec(m_ext.shape, lambda b, p, i: (0, 0))],
        out_specs=pl.BlockSpec((1, tq, LANES), lambda b, p, i: (b, i, p)),
        out_shape=jax.ShapeDtypeStruct((batch, seq, BRANCH_WIDTH), BF16),
        scratch_shapes=[pltpu.VMEM((nsub, 2, blk, LANES), F32), pltpu.VMEM((nsub, 2, blk, 1), F32)],
        compiler_params=_cparams(("parallel", "parallel", "parallel")),
        name="sb_attention",
    )(qkv3, qkv3, qkv3, m_ext)
    return out.reshape(batch * seq, BRANCH_WIDTH)


def _s5_operators(lam_re, lam_im, log_dt, b_re, b_im, c_re, c_im, d_skip):
    L = S5_CHUNK
    g, p, c = S5_GROUPS, S5_STATE, S5_GROUP_CH
    dt = jnp.exp(log_dt.astype(F32))[:, None]
    lr = lam_re.astype(F32)
    li = lam_im.astype(F32)

    def a_pow(n):
        n = jnp.asarray(n, F32)[..., None, None]
        mag = jnp.exp(lr * dt * n)
        return mag * jnp.cos(li * dt * n), mag * jnp.sin(li * dt * n)

    ar, ai = a_pow(1.0)
    den = lr * lr + li * li
    nr = ar - 1.0
    fr = (nr * lr + ai * li) / den
    fi = (ai * lr - nr * li) / den
    bbr = fr[..., None] * b_re - fi[..., None] * b_im
    bbi = fr[..., None] * b_im + fi[..., None] * b_re
    cr = c_re.astype(F32)
    ci = c_im.astype(F32)

    pr, pi = a_pow(np.arange(L))
    car = cr[None] * pr[:, :, None, :] - ci[None] * pi[:, :, None, :]
    cai = cr[None] * pi[:, :, None, :] + ci[None] * pr[:, :, None, :]
    klag = jnp.einsum('jgop,gpi->jgoi', car, bbr) - jnp.einsum('jgop,gpi->jgoi', cai, bbi)
    klag = klag.at[0].add(d_skip.astype(F32).reshape(g, c)[:, :, None] * jnp.eye(c, dtype=F32))
    kflat = klag.transpose(1, 3, 0, 2).reshape(g, c, L * c)
    toe = jnp.stack([jnp.pad(kflat[:, :, :(L - s) * c], ((0, 0), (0, 0), (s * c, 0))) for s in range(L)], axis=1)
    toe = toe.reshape(g, L * c, L * c)

    qr, qi = a_pow(L - 1 - np.arange(L))
    p_re = qr[:, :, :, None] * bbr[None] - qi[:, :, :, None] * bbi[None]
    p_im = qr[:, :, :, None] * bbi[None] + qi[:, :, :, None] * bbr[None]
    p_re = p_re.transpose(1, 0, 3, 2).reshape(g, L * c, p)
    p_im = p_im.transpose(1, 0, 3, 2).reshape(g, L * c, p)

    er, ei = a_pow(1 + np.arange(L))
    q_re = (cr[None] * er[:, :, None, :] - ci[None] * ei[:, :, None, :])
    q_im = -(cr[None] * ei[:, :, None, :] + ci[None] * er[:, :, None, :])
    q_re = q_re.transpose(1, 3, 0, 2).reshape(g, p, L * c)
    q_im = q_im.transpose(1, 3, 0, 2).reshape(g, p, L * c)

    a_lr, a_li = a_pow(float(L))

    def pair_diag(m):
        gg, r, cc = m.shape
        m = m.reshape(gg // 2, 2, r, cc)
        z = jnp.zeros_like(m[:, 0])
        top = jnp.concatenate([m[:, 0], z], axis=2)
        bot = jnp.concatenate([z, m[:, 1]], axis=2)
        return jnp.concatenate([top, bot], axis=1)

    toe_p = pair_diag(toe).astype(BF16)
    p_p = jnp.concatenate([pair_diag(p_re), pair_diag(p_im)], axis=2).astype(BF16)
    q_p = jnp.concatenate([pair_diag(q_re), pair_diag(q_im)], axis=1).astype(BF16)
    a_re = a_lr.reshape(1, g * p)
    a_im = a_li.reshape(1, g * p)
    return toe_p, p_p, q_p, a_re, a_im


def _group_block_transpose(tiles):
    group = lax.broadcasted_iota(jnp.int32, tiles[0].shape, 1) // S5_GROUP_CH
    n = len(tiles)
    k = n // 2
    while k >= 1:
        upper = (group & k) != 0
        nxt = list(tiles)
        for i in range(n):
            if i & k == 0:
                lo, hi = tiles[i], tiles[i + k]
                nxt[i] = jnp.where(upper, pltpu.roll(hi, k * S5_GROUP_CH, 1), lo)
                nxt[i + k] = jnp.where(upper, hi, pltpu.roll(lo, LANES - k * S5_GROUP_CH, 1))
        tiles = nxt
        k //= 2
    return tiles


def _s5_state_in_kernel(u_ref, p_ref, up_ref, vre_ref, vim_ref, *, nb):
    L = S5_CHUNK
    per = LANES // S5_GROUP_CH
    for kb in range(u_ref.shape[0]):
        halves = []
        for a in range(L // per):
            steps = [u_ref[kb, pl.ds(a * per + b, nb, stride=L), :] for b in range(per)]
            halves.append(_group_block_transpose(steps))
        for pi in range(per // 2):
            pair = kb * (per // 2) + pi
            row = jnp.concatenate([halves[a][2 * pi + gi] for gi in range(2) for a in range(L // per)], axis=1)
            row = row.astype(BF16)
            up_ref[pair] = row
            v = _dot(row, p_ref[pair])
            vre_ref[:, pair * LANES:(pair + 1) * LANES] = v[:, :LANES]
            vim_ref[:, pair * LANES:(pair + 1) * LANES] = v[:, LANES:]


def _s5_scan_kernel(vre_ref, vim_ref, are_ref, aim_ref, sre_ref, sim_ref):
    n = vre_ref.shape[0]
    ar = are_ref[...]
    ai = aim_ref[...]

    def step(r, carry):
        xr, xi = carry
        sre_ref[pl.ds(r, 1), :] = xr
        sim_ref[pl.ds(r, 1), :] = xi
        vr = vre_ref[pl.ds(r, 1), :]
        vi = vim_ref[pl.ds(r, 1), :]
        return ar * xr - ai * xi + vr, ar * xi + ai * xr + vi

    z = jnp.zeros_like(ar)
    lax.fori_loop(0, n, step, (z, z))


def _s5_out_kernel(up_ref, sre_ref, sim_ref, toe_ref, q_ref, y_ref, *, nb):
    L = S5_CHUNK
    per = LANES // S5_GROUP_CH
    for kb in range(y_ref.shape[0]):
        halves = [[None] * per for _ in range(L // per)]
        for pi in range(per // 2):
            pair = kb * (per // 2) + pi
            q = q_ref[pair]
            cols = slice(pair * LANES, (pair + 1) * LANES)
            y = _dot(up_ref[pair], toe_ref[pair])
            y += _dot(sre_ref[:, cols].astype(BF16), q[:LANES]) + _dot(sim_ref[:, cols].astype(BF16), q[LANES:])
            y = jax.nn.gelu(y)
            for gi in range(2):
                for a in range(L // per):
                    j = gi * (L // per) + a
                    halves[a][2 * pi + gi] = y[:, j * LANES:(j + 1) * LANES]
        for a in range(L // per):
            steps = _group_block_transpose(halves[a])
            for b in range(per):
                y_ref[kb, pl.ds(a * per + b, nb, stride=L), :] = steps[b]


def _s5_ssm(u, batch, seq, ops, nb=128):
    toe_p, p_p, q_p, a_re, a_im = ops
    nslab, t, _ = u.shape
    L = S5_CHUNK
    npair = S5_GROUPS // 2
    nc = t // L
    row = 2 * L * S5_GROUP_CH
    ncols = S5_GROUPS * S5_STATE
    nb = min(nb, nc)

    def full(a):
        return pl.BlockSpec(a.shape, lambda i: (0,) * a.ndim)

    slab_spec = pl.BlockSpec((nslab, nb * L, LANES), lambda i: (0, i, 0))
    up_spec = pl.BlockSpec((npair, nb, row), lambda i: (0, i, 0))
    st_spec = pl.BlockSpec((nb, ncols), lambda i: (i, 0))
    u_p, v_re, v_im = pl.pallas_call(
        functools.partial(_s5_state_in_kernel, nb=nb),
        grid=(nc // nb,),
        in_specs=[slab_spec, full(p_p)],
        out_specs=[up_spec, st_spec, st_spec],
        out_shape=[jax.ShapeDtypeStruct((npair, nc, row), BF16)] + [jax.ShapeDtypeStruct((nc, ncols), F32)] * 2,
        compiler_params=_cparams(("parallel",)),
        name="s5_chunk_state",
    )(u, p_p)
    ncb = nc // batch
    cw = 512
    s_re, s_im = pl.pallas_call(
        _s5_scan_kernel,
        grid=(batch, ncols // cw),
        in_specs=[pl.BlockSpec((ncb, cw), lambda b, j: (b, j)),
                  pl.BlockSpec((ncb, cw), lambda b, j: (b, j)),
                  pl.BlockSpec((1, cw), lambda b, j: (0, j)),
                  pl.BlockSpec((1, cw), lambda b, j: (0, j))],
        out_specs=[pl.BlockSpec((ncb, cw), lambda b, j: (b, j))] * 2,
        out_shape=[jax.ShapeDtypeStruct((nc, ncols), F32)] * 2,
        compiler_params=_cparams(("parallel", "parallel")),
        name="s5_scan",
    )(v_re, v_im, a_re, a_im)
    return pl.pallas_call(
        functools.partial(_s5_out_kernel, nb=nb),
        grid=(nc // nb,),
        in_specs=[up_spec, st_spec, st_spec, full(toe_p), full(q_p)],
        out_specs=slab_spec,
        out_shape=jax.ShapeDtypeStruct((nslab, t, LANES), F32),
        compiler_params=_cparams(("parallel",)),
        name="s5_out",
    )(u_p, s_re, s_im, toe_p, q_p)


def _ret_tables(seq):
    half = HEAD_DIM // 2
    pos = jnp.arange(seq, dtype=F32)
    inv = ROPE_BASE ** (-jnp.arange(half, dtype=F32) / half)
    ang = pos[:, None] * inv[None, :]
    cos = jnp.tile(jnp.cos(ang), (1, LANES // half))
    sin = jnp.sin(ang)
    sin = jnp.tile(jnp.concatenate([-sin, sin], axis=1), (1, LANES // HEAD_DIM))
    c = RET_CHUNK
    gamma = 1.0 - 2.0 ** (-5.0 - jnp.arange(RET_HEADS, dtype=F32))
    log_g = jnp.log(gamma)
    idx = jnp.arange(c, dtype=F32)
    diff = idx[:, None] - idx[None, :]
    intra = jnp.where(diff >= 0, jnp.exp(jnp.maximum(diff, 0.0)[None] * log_g[:, None, None]), 0.0)
    q_dec = jnp.exp((idx + 1.0)[None, :] * log_g[:, None])
    k_dec = jnp.exp((c - 1.0 - idx)[None, :] * log_g[:, None])
    ch_dec = jnp.exp(c * log_g)

    def lanes(x):
        x = x.reshape(RET_HEADS // 2, 2, c)
        return jnp.repeat(x.transpose(0, 2, 1), HEAD_DIM, axis=2)

    q_dec_l = lanes(q_dec)
    k_dec_l = lanes(k_dec)
    hp = RET_HEADS // 2
    blockdiag = np.kron(np.eye(2), np.ones((HEAD_DIM, HEAD_DIM)))
    st_dec = jnp.repeat(ch_dec.reshape(hp, 2), HEAD_DIM, axis=1)[:, :, None] * blockdiag[None]
    avg = jnp.asarray(blockdiag / HEAD_DIM, BF16)
    return cos, sin, intra, q_dec_l, k_dec_l, st_dec.astype(F32), avg


def _ret_kernel(q_ref, k_ref, v_ref, g_ref, cos_ref, sin_ref, intra_ref, qd_ref, kd_ref, sd_ref, avg_ref, nw_ref,
                o_ref, st_ref, *, nchunk):
    c = RET_CHUNK

    @pl.when(pl.program_id(2) == 0)
    def _():
        st_ref[...] = jnp.zeros_like(st_ref)

    tq = q_ref.shape[1]
    lane = lax.broadcasted_iota(jnp.int32, (tq, LANES), 1)
    first_half = (lane % HEAD_DIM) < (HEAD_DIM // 2)
    head0 = lax.broadcasted_iota(jnp.int32, (c, LANES), 1) < HEAD_DIM
    sd = sd_ref[0]
    avg = avg_ref[...]
    same_head = avg > 0
    cos = cos_ref[...]
    sin = sin_ref[...]

    def rotary(x):
        swapped = jnp.where(first_half, pltpu.roll(x, LANES - HEAD_DIM // 2, 1), pltpu.roll(x, HEAD_DIM // 2, 1))
        return x * cos + swapped * sin

    q_all = rotary(q_ref[0].astype(F32)).astype(BF16)
    k_all = rotary(k_ref[0].astype(F32)) * (HEAD_DIM ** -0.5)
    kb_all = k_all.astype(BF16)
    zero = jnp.zeros((c, LANES), BF16)
    chunks = [slice(ci * c, (ci + 1) * c) for ci in range(nchunk)]
    vs = [v_ref[0, rows, :] for rows in chunks]
    scores, incs = [], []
    for ci, rows in enumerate(chunks):
        qb = q_all[rows]
        q2 = jnp.concatenate([jnp.where(head0, qb, zero), jnp.where(head0, zero, qb)], axis=0)
        scores.append(_dot_nt(q2, kb_all[rows]))
        kd_t = (k_all[rows] * kd_ref[0]).T.astype(BF16)
        incs.append(_dot(kd_t, vs[ci]))
    state = st_ref[...]
    states = []
    for inc in incs:
        states.append(state.astype(BF16))
        state = state * sd + jnp.where(same_head, inc, 0.0)
    st_ref[...] = state
    intra2 = jnp.concatenate([intra_ref[0, 0], intra_ref[0, 1]], axis=0)
    outs = []
    for ci, rows in enumerate(chunks):
        o2 = _dot((scores[ci] * intra2).astype(BF16), vs[ci])
        outs.append(jnp.where(head0, o2[:c], o2[c:]) + _dot(q_all[rows], states[ci]) * qd_ref[0])
    out = jnp.concatenate(outs, axis=0)
    hi, lo = _split_bf16(out)
    mu = _dot(hi, avg) + _dot(lo, avg)
    dlt = out - mu
    hi, lo = _split_bf16(dlt * dlt)
    var = _dot(hi, avg) + _dot(lo, avg)
    gate = g_ref[0].astype(F32)
    y = dlt * lax.rsqrt(var + EPS) * nw_ref[...] * (gate * jax.nn.sigmoid(gate))
    o_ref[0] = y.astype(o_ref.dtype)


def _retention(ret, norm_w, batch, seq, tables, tq=1024):
    cos, sin, intra, q_dec_l, k_dec_l, st_dec, avg = tables
    pairs = BRANCH_WIDTH // LANES
    ret3 = ret.reshape(batch, seq, 4 * BRANCH_WIDTH)
    tq = min(tq, seq)
    c = RET_CHUNK

    def col(k):
        return pl.BlockSpec((1, tq, LANES), lambda b, p, i: (b, i, k * pairs + p))

    out = pl.pallas_call(
        functools.partial(_ret_kernel, nchunk=tq // c),
        grid=(batch, pairs, seq // tq),
        in_specs=[col(0), col(1), col(2), col(3),
                  pl.BlockSpec((tq, LANES), lambda b, p, i: (i, 0)),
                  pl.BlockSpec((tq, LANES), lambda b, p, i: (i, 0)),
                  pl.BlockSpec((1, 2, c, c), lambda b, p, i: (p, 0, 0, 0)),
                  pl.BlockSpec((1, c, LANES), lambda b, p, i: (p, 0, 0)),
                  pl.BlockSpec((1, c, LANES), lambda b, p, i: (p, 0, 0)),
                  pl.BlockSpec((1, LANES, LANES), lambda b, p, i: (p, 0, 0)),
                  pl.BlockSpec((LANES, LANES), lambda b, p, i: (0, 0)),
                  pl.BlockSpec((1, LANES), lambda b, p, i: (0, p))],
        out_specs=pl.BlockSpec((1, tq, LANES), lambda b, p, i: (b, i, p)),
        out_shape=jax.ShapeDtypeStruct((batch, seq, BRANCH_WIDTH), BF16),
        scratch_shapes=[pltpu.VMEM((LANES, LANES), F32)],
        compiler_params=_cparams(("parallel", "parallel", "arbitrary")),
        name="retention",
    )(ret3, ret3, ret3, ret3, cos, sin, intra.reshape(pairs, 2, c, c), q_dec_l, k_dec_l, st_dec, avg,
      norm_w.reshape(1, BRANCH_WIDTH).astype(F32))
    return out.reshape(batch * seq, BRANCH_WIDTH)


def _route(logits):
    lane = lax.broadcasted_iota(jnp.int32, logits.shape, 1).astype(F32)
    neg = jnp.float32(-jnp.inf)
    big = jnp.float32(1 << 20)
    gl = jnp.where(lane < N_GROUPS, logits, neg)
    gmax = jnp.max(gl, axis=1, keepdims=True)
    gsum = jnp.sum(jnp.exp(gl - gmax), axis=1, keepdims=True)
    g_w = 1.0 / gsum
    g_idx = jnp.min(jnp.where(gl == gmax, lane, big), axis=1, keepdims=True)
    lo = ROUTER_LANE0 + EXPERTS_PER_GROUP * g_idx
    sl = jnp.where((lane >= lo) & (lane < lo + EXPERTS_PER_GROUP), logits, neg)
    m1 = jnp.max(sl, axis=1, keepdims=True)
    i1 = jnp.min(jnp.where(sl == m1, lane, big), axis=1, keepdims=True)
    sl2 = jnp.where(lane == i1, neg, sl)
    m2 = jnp.max(sl2, axis=1, keepdims=True)
    i2 = jnp.min(jnp.where(sl2 == m2, lane, big), axis=1, keepdims=True)
    e2 = jnp.exp(m2 - m1)
    w1 = g_w / (1.0 + e2)
    w2 = g_w * e2 / (1.0 + e2)
    packed = jnp.where(lane == 0.0, i1 - ROUTER_LANE0, jnp.where(lane == 1.0, i2 - ROUTER_LANE0, 0.0))
    return packed + jnp.where(lane == 2.0, w1, 0.0) + jnp.where(lane == 3.0, w2, 0.0)


def _merge_kernel(ysb_ref, ys5_ref, yret_ref, gate_ref, h_ref, wglu_ref, wbr_ref, wout_ref, n2_ref, wr_hi_ref,
                  wr_lo_ref, br_ref, h_out_ref, hn_ref, route_ref):
    ys5 = jnp.concatenate([ys5_ref[s] for s in range(ys5_ref.shape[0])], axis=1).astype(BF16)
    glu = _dot(ys5, wglu_ref[...])
    y_s5 = (glu[:, :BRANCH_WIDTH] * jax.nn.sigmoid(glu[:, BRANCH_WIDTH:])).astype(BF16)
    merged = None
    for n, y in enumerate((ysb_ref[...], y_s5, yret_ref[...])):
        g = jax.nn.sigmoid(gate_ref[:, n * D_MODEL:(n + 1) * D_MODEL].astype(F32))
        term = g * _dot(y, wbr_ref[n])
        merged = term if merged is None else merged + term
    h = h_ref[...] + _dot(merged.astype(BF16), wout_ref[...])
    h_out_ref[...] = h
    ms = jnp.mean(h * h, axis=-1, keepdims=True)
    hn = h * lax.rsqrt(ms + EPS) * n2_ref[...]
    hn_ref[...] = hn.astype(BF16)
    hi, lo = _split_bf16(hn)
    logits = _dot(hi, wr_hi_ref[...]) + _dot(hi, wr_lo_ref[...]) + _dot(lo, wr_hi_ref[...]) + br_ref[...]
    route_ref[...] = _route(logits)


def _merge(y_sb, y_s5, y_ret, gates, h, w_glu, w_br, w_out, norm2_w, wr_hi, wr_lo, b_r, layer, tm=512):
    t = h.shape[0]

    def rows(w):
        return pl.BlockSpec((tm, w), lambda i: (i, 0))

    def full(a):
        return pl.BlockSpec(a.shape, lambda i: (0,) * a.ndim)

    return pl.pallas_call(
        _merge_kernel,
        grid=(t // tm,),
        in_specs=[rows(BRANCH_WIDTH), pl.BlockSpec((y_s5.shape[0], tm, LANES), lambda i: (0, i, 0)),
                  rows(BRANCH_WIDTH), rows(N_BRANCH * D_MODEL),
                  rows(D_MODEL), _layer_block(w_glu, layer), _layer_block(w_br, layer), _layer_block(w_out, layer),
                  full(norm2_w), full(wr_hi), full(wr_lo), full(b_r)],
        out_specs=[rows(D_MODEL), rows(D_MODEL), rows(LANES)],
        out_shape=[jax.ShapeDtypeStruct((t, D_MODEL), F32), jax.ShapeDtypeStruct((t, D_MODEL), BF16),
                   jax.ShapeDtypeStruct((t, LANES), F32)],
        compiler_params=_cparams(("parallel",)),
        name="merge_route",
    )(y_sb, y_s5, y_ret, gates, h, w_glu, w_br, w_out, norm2_w, wr_hi, wr_lo, b_r)


MOE_TILE = 512
MOE_ROW_TILE = 512
SEG_ALIGN = 16
MOE_SLOTS = TOP_K * MOE_TILE + N_EXPERTS * SEG_ALIGN
MOE_CHUNKS = MOE_SLOTS // SEG_ALIGN


def _moe_plan_kernel(route_ref, tri_ref, cum_ref, pos_ref, cnt_ref):
    r = route_ref[...]
    tm = r.shape[0]
    lane = lax.broadcasted_iota(jnp.int32, r.shape, 1).astype(F32)
    e1 = r[:, 0:1]
    e2 = r[:, 1:2]
    chosen = (lane == e1) | (lane == e2)
    onehot = jnp.where(chosen, 1.0, 0.0).astype(BF16)
    incl = _dot(tri_ref[...], onehot)
    cnt = incl[tm - 1:tm, :]
    padded = jnp.floor((cnt + (SEG_ALIGN - 1)) * (1.0 / SEG_ALIGN)) * SEG_ALIGN
    start = _dot(jnp.broadcast_to(padded, (8, LANES)).astype(BF16), cum_ref[...])[0:1, :]
    slot = start + incl - 1.0
    p1 = jnp.sum(jnp.where(lane == e1, slot, 0.0), axis=1, keepdims=True)
    p2 = jnp.sum(jnp.where(lane == e2, slot, 0.0), axis=1, keepdims=True)
    pos_ref[...] = jnp.where(lane == 0.0, p1, jnp.where(lane == 1.0, p2, 0.0))
    cnt_ref[0] = jnp.broadcast_to(cnt, (8, LANES))


def _moe_plan(route, tm):
    t = route.shape[0]
    r = np.arange(tm)
    tri = jnp.asarray(r[None, :] <= r[:, None], BF16)
    l = np.arange(LANES)
    cum = jnp.asarray(l[:, None] < l[None, :], BF16)
    return pl.pallas_call(
        _moe_plan_kernel,
        grid=(t // tm,),
        in_specs=[pl.BlockSpec((tm, LANES), lambda i: (i, 0)),
                  pl.BlockSpec((tm, tm), lambda i: (0, 0)),
                  pl.BlockSpec((LANES, LANES), lambda i: (0, 0))],
        out_specs=[pl.BlockSpec((tm, LANES), lambda i: (i, 0)), pl.BlockSpec((1, 8, LANES), lambda i: (i, 0, 0))],
        out_shape=[jax.ShapeDtypeStruct((t, LANES), F32), jax.ShapeDtypeStruct((t // tm, 8, LANES), F32)],
        compiler_params=_cparams(("parallel",)),
        name="moe_plan",
    )(route, tri, cum)


def _moe_tables(cnt, rt, n_rows):
    a = SEG_ALIGN
    ntile = cnt.shape[0]
    p = (cnt + a - 1) // a * a
    lend = jnp.cumsum(p, axis=1)
    lstart = lend - p
    seg = p.sum(axis=0)
    segpad = (seg + rt - 1) // rt * rt
    gend = jnp.cumsum(segpad)
    gstart = gend - segpad
    toff = gstart[None, :] + jnp.cumsum(p, axis=0) - p
    cstart = a * jnp.arange(MOE_CHUNKS, dtype=jnp.int32)
    in_run = (cstart[None, :, None] >= lstart[:, None, :]) & (cstart[None, :, None] < lend[:, None, :])
    gdst = (jnp.sum(jnp.where(in_run, (toff - lstart)[:, None, :], 0), axis=2) + cstart[None, :]) // a
    nch = lend[:, -1] // a
    tstart = rt * jnp.arange(n_rows // rt, dtype=jnp.int32)
    tile_e = jnp.minimum(jnp.sum(tstart[:, None] >= gend[None, :], axis=1), N_EXPERTS - 1)
    n_used = gend[-1] // rt
    zmax = rt // a - 1
    zper = -(-N_EXPERTS * zmax // ntile)
    k = jnp.arange(zmax, dtype=jnp.int32)
    zd = ((gstart + seg) // a)[:, None] + k[None, :]
    valid = (k[None, :] < ((segpad - seg) // a)[:, None]).reshape(-1)
    order = jnp.argsort(jnp.logical_not(valid), stable=True)
    zdst = jnp.pad(zd.reshape(-1)[order], (0, ntile * zper - N_EXPERTS * zmax))
    nz = valid.sum()
    i32 = jnp.int32
    return (gdst.reshape(-1).astype(i32), nch.astype(i32), zdst.astype(i32), nz.reshape(1).astype(i32),
            tile_e.astype(i32), n_used.reshape(1).astype(i32)), zper


def _moe_dispatch_kernel(gdst_ref, nch_ref, zdst_ref, nz_ref, n_used_ref, pos_ref, hn_ref, xs_hbm, stg, zblk, sem,
                         zsem, *, zper):
    i = pl.program_id(0)
    last_step = pl.num_programs(0) - 1
    buf = i % 2
    tm = hn_ref.shape[0]
    pos_t = pos_ref[...].T
    p1 = pos_t[0:1, :]
    p2 = pos_t[1:2, :]
    x = hn_ref[...]
    rb = MOE_ROW_TILE
    for b in range(MOE_SLOTS // rb):
        slot = (lax.broadcasted_iota(jnp.int32, (rb, tm), 0) + b * rb).astype(F32)
        onehot = jnp.where((slot == p1) | (slot == p2), 1.0, 0.0).astype(BF16)
        stg[buf, b * rb:(b + 1) * rb, :] = _dot(onehot, x).astype(BF16)

    def chunk_copy(which, src_chunk, dst_chunk):
        src = pl.multiple_of(src_chunk * SEG_ALIGN, SEG_ALIGN)
        dst = pl.multiple_of(dst_chunk * SEG_ALIGN, SEG_ALIGN)
        return pltpu.make_async_copy(stg.at[which, pl.ds(src, SEG_ALIGN)], xs_hbm.at[pl.ds(dst, SEG_ALIGN)],
                                     sem.at[which])

    def n_copies(step):
        return nch_ref[step], jnp.clip(nz_ref[0] - step * zper, 0, zper)

    def drain(which, step):
        for count in n_copies(step):
            rows = count * SEG_ALIGN

            @pl.when(rows > 0)
            def _():
                pltpu.make_async_copy(stg.at[which, pl.ds(0, rows)], xs_hbm.at[pl.ds(0, rows)],
                                      sem.at[which]).wait()

    n, nzero = n_copies(i)

    @pl.loop(0, n)
    def _(c):
        chunk_copy(buf, c, gdst_ref[i * MOE_CHUNKS + c]).start()

    @pl.loop(0, nzero)
    def _(k):
        chunk_copy(buf, MOE_CHUNKS - 1, zdst_ref[i * zper + k]).start()

    pl.when(i > 0)(lambda: drain(1 - buf, i - 1))

    @pl.when(i == last_step)
    def _():
        drain(buf, i)
        rt = zblk.shape[0]
        zblk[...] = jnp.zeros_like(zblk)
        first = n_used_ref[0]
        last = xs_hbm.shape[0] // rt

        def tile_copy(j):
            return pltpu.make_async_copy(zblk, xs_hbm.at[pl.ds(pl.multiple_of(j * rt, rt), rt)], zsem)

        @pl.loop(first, last)
        def _(j):
            tile_copy(j).start()

        @pl.loop(first, last)
        def _(j):
            tile_copy(0).wait()


def _moe_expert_kernel(tile_e_ref, n_used_ref, xs_ref, wg_ref, wu_ref, wd_ref, ys_ref):
    used = pl.program_id(0) < n_used_ref[0]

    @pl.when(used)
    def _():
        x = xs_ref[...]
        gate = _dot(x, wg_ref[0].astype(BF16))
        up = _dot(x, wu_ref[0].astype(BF16))
        hdn = gate * jax.nn.sigmoid(gate) * up
        ys_ref[...] = _dot(hdn.astype(BF16), wd_ref[0].astype(BF16)).astype(ys_ref.dtype)

    @pl.when(jnp.logical_not(used))
    def _():
        ys_ref[...] = jnp.zeros_like(ys_ref)


def _moe_combine_kernel(gdst_ref, nch_ref, route_ref, pos_ref, h_ref, nw_ref, ys_hbm, o_ref, stg, sem, *, out_norm):
    i = pl.program_id(0)
    nsteps = pl.num_programs(0)
    buf = i % 2
    tm = h_ref.shape[0]

    def chunk_copy(which, src_chunk, dst_chunk):
        src = pl.multiple_of(src_chunk * SEG_ALIGN, SEG_ALIGN)
        dst = pl.multiple_of(dst_chunk * SEG_ALIGN, SEG_ALIGN)
        return pltpu.make_async_copy(ys_hbm.at[pl.ds(src, SEG_ALIGN)], stg.at[which, pl.ds(dst, SEG_ALIGN)],
                                     sem.at[which])

    def fetch(step, which):
        n = nch_ref[step]

        @pl.loop(0, n)
        def _(c):
            chunk_copy(which, gdst_ref[step * MOE_CHUNKS + c], c).start()

        @pl.loop(n, MOE_CHUNKS)
        def _(c):
            rows = pl.ds(pl.multiple_of(c * SEG_ALIGN, SEG_ALIGN), SEG_ALIGN)
            stg[which, rows, :] = jnp.zeros((SEG_ALIGN, D_MODEL), BF16)

    pl.when(i == 0)(lambda: fetch(0, 0))
    pl.when(i + 1 < nsteps)(lambda: fetch(i + 1, 1 - buf))

    r = route_ref[...]
    pos = pos_ref[...]
    slot = lax.broadcasted_iota(jnp.int32, (tm, MOE_SLOTS), 1).astype(F32)
    w = jnp.where(slot == pos[:, 0:1], r[:, 2:3], 0.0) + jnp.where(slot == pos[:, 1:2], r[:, 3:4], 0.0)

    rows = nch_ref[i] * SEG_ALIGN

    @pl.when(rows > 0)
    def _():
        pltpu.make_async_copy(ys_hbm.at[pl.ds(0, rows)], stg.at[buf, pl.ds(0, rows)], sem.at[buf]).wait()

    out = h_ref[...] + _dot(w.astype(BF16), stg[buf])
    if out_norm:
        ms = jnp.mean(out * out, axis=-1, keepdims=True)
        out = out * lax.rsqrt(ms + EPS) * nw_ref[...]
    o_ref[...] = out


def _moe(hn, route, h, w_gate, w_up, w_down, layer, out_norm_w, out_norm):
    t = h.shape[0]
    tm = min(MOE_TILE, t)
    rt = MOE_ROW_TILE
    ntile = t // tm
    n_rows = TOP_K * t + ntile * N_EXPERTS * (SEG_ALIGN - 1) + N_EXPERTS * (rt - SEG_ALIGN)
    n_rows = (n_rows + rt - 1) // rt * rt
    pos, cnt = _moe_plan(route, tm)
    (gdst, nch, zdst, nz, tile_e, n_used), zper = _moe_tables(cnt[:, 0, :N_EXPERTS].astype(jnp.int32), rt, n_rows)
    xs = pl.pallas_call(
        functools.partial(_moe_dispatch_kernel, zper=zper),
        grid_spec=pltpu.PrefetchScalarGridSpec(
            num_scalar_prefetch=5,
            grid=(ntile,),
            in_specs=[pl.BlockSpec((tm, LANES), lambda i, *_: (i, 0)),
                      pl.BlockSpec((tm, D_MODEL), lambda i, *_: (i, 0))],
            out_specs=pl.BlockSpec(memory_space=pl.ANY),
            scratch_shapes=[pltpu.VMEM((2, MOE_SLOTS, D_MODEL), BF16), pltpu.VMEM((rt, D_MODEL), BF16),
                            pltpu.SemaphoreType.DMA((2,)), pltpu.SemaphoreType.DMA(())]),
        out_shape=jax.ShapeDtypeStruct((n_rows, D_MODEL), BF16),
        compiler_params=_cparams(("arbitrary",)),
        name="moe_dispatch",
    )(gdst, nch, zdst, nz, n_used, pos, hn)
    ys = pl.pallas_call(
        _moe_expert_kernel,
        grid_spec=pltpu.PrefetchScalarGridSpec(
            num_scalar_prefetch=2,
            grid=(n_rows // rt,),
            in_specs=[pl.BlockSpec((rt, D_MODEL), lambda j, te, nu: (jnp.minimum(j, nu[0] - 1), 0)),
                      pl.BlockSpec((None, 1, D_MODEL, EXPERT_FF), lambda j, te, nu: (layer, te[j], 0, 0)),
                      pl.BlockSpec((None, 1, D_MODEL, EXPERT_FF), lambda j, te, nu: (layer, te[j], 0, 0)),
                      pl.BlockSpec((None, 1, EXPERT_FF, D_MODEL), lambda j, te, nu: (layer, te[j], 0, 0))],
            out_specs=pl.BlockSpec((rt, D_MODEL), lambda j, te, nu: (j, 0))),
        out_shape=jax.ShapeDtypeStruct((n_rows, D_MODEL), BF16),
        compiler_params=_cparams(("arbitrary",)),
        name="moe_experts",
    )(tile_e, n_used, xs, w_gate, w_up, w_down)
    return pl.pallas_call(
        functools.partial(_moe_combine_kernel, out_norm=out_norm),
        grid_spec=pltpu.PrefetchScalarGridSpec(
            num_scalar_prefetch=2,
            grid=(ntile,),
            in_specs=[pl.BlockSpec((tm, LANES), lambda i, *_: (i, 0)),
                      pl.BlockSpec((tm, LANES), lambda i, *_: (i, 0)),
                      pl.BlockSpec((tm, D_MODEL), lambda i, *_: (i, 0)),
                      pl.BlockSpec((1, D_MODEL), lambda i, *_: (0, 0)),
                      pl.BlockSpec(memory_space=pl.ANY)],
            out_specs=pl.BlockSpec((tm, D_MODEL), lambda i, *_: (i, 0)),
            scratch_shapes=[pltpu.VMEM((2, MOE_SLOTS, D_MODEL), BF16), pltpu.SemaphoreType.DMA((2,))]),
        out_shape=jax.ShapeDtypeStruct((t, D_MODEL), F32),
        compiler_params=_cparams(("arbitrary",)),
        name="moe_combine",
    )(gdst, nch, route, pos, h, out_norm_w, ys)


def kernel(x, norm1_w, w_in, s5_lambda_re, s5_lambda_im, s5_log_dt, s5_b_re, s5_b_im, s5_c_re, s5_c_im, s5_d,
           s5_w_glu, ret_norm_w, w_branch, w_out, norm2_w, w_group, b_group, w_router, b_router, w_gate, w_up,
           w_down, final_norm_w):
    batch, seq, _ = x.shape
    depth = w_in.shape[0]
    t = batch * seq
    h = x.reshape(t, D_MODEL).astype(F32)
    ret_tables = _ret_tables(seq)
    w_in, s5_w_glu, w_branch, w_out = (w.astype(BF16) for w in (w_in, s5_w_glu, w_branch, w_out))
    for layer in range(depth):
        qkv, u_s5, ret, gates = _inproj(h, norm1_w[layer].reshape(1, D_MODEL).astype(F32), w_in, layer)
        y_sb = _sb_attention(qkv, batch, seq)
        s5_ops = _s5_operators(s5_lambda_re[layer], s5_lambda_im[layer], s5_log_dt[layer], s5_b_re[layer],
                               s5_b_im[layer], s5_c_re[layer], s5_c_im[layer], s5_d[layer])
        y_s5 = _s5_ssm(u_s5, batch, seq, s5_ops)
        y_ret = _retention(ret, ret_norm_w[layer], batch, seq, ret_tables)
        w_r = jnp.concatenate([w_group[layer], w_router[layer].transpose(1, 0, 2).reshape(D_MODEL, N_EXPERTS)], axis=1)
        w_r = jnp.pad(w_r.astype(F32), ((0, 0), (0, LANES - w_r.shape[1])))
        wr_hi, wr_lo = _split_bf16(w_r)
        b_r = jnp.pad(jnp.concatenate([b_group[layer], b_router[layer].reshape(-1)]).astype(F32),
                      (0, LANES - N_GROUPS - N_EXPERTS)).reshape(1, LANES)
        h, hn, route = _merge(y_sb, y_s5, y_ret, gates, h, s5_w_glu, w_branch, w_out,
                              norm2_w[layer].reshape(1, D_MODEL).astype(F32), wr_hi, wr_lo, b_r, layer)
        h = _moe(hn, route, h, w_gate, w_up, w_down, layer, final_norm_w.reshape(1, D_MODEL).astype(F32),
                 out_norm=(layer == depth - 1))
    return h.reshape(batch, seq, D_MODEL).astype(x.dtype)
```

```python
import functools
import math

import jax
import jax.numpy as jnp
import numpy as np
from jax import lax
from jax.experimental import pallas as pl
from jax.experimental.pallas import tpu as pltpu

F32 = jnp.float32
BF16 = jnp.bfloat16

D_MODEL = 1024
SB_HEADS = 8
HEAD_DIM = 64
BRANCH_WIDTH = 512
S5_GROUPS = 32
S5_GROUP_CH = 16
S5_STATE = 64
RET_HEADS = 8
RET_CHUNK = 128
ROPE_BASE = 10000.0
N_BRANCH = 3
N_GROUPS = 4
EXPERTS_PER_GROUP = 8
N_EXPERTS = 32
TOP_K = 2
EXPERT_FF = 256
EPS = 1e-6

LANES = 128
VMEM_LIMIT = 56 * 1024 * 1024
S5_CHUNK = 16
SB_BLOCK = 256
SB_SUBBLOCKS = 4
LOG2_E = 1.4426950408889634
SB_SKIP_LOG2 = -152.0
ROUTER_LANE0 = N_GROUPS


def _cparams(sem):
    return pltpu.CompilerParams(dimension_semantics=sem, vmem_limit_bytes=VMEM_LIMIT)


def _layer_block(stacked, layer):
    zeros = (0,) * (stacked.ndim - 1)
    return pl.BlockSpec((None,) + stacked.shape[1:], lambda *_: (layer,) + zeros, pipeline_mode=pl.Buffered(1))


def _split_bf16(x):
    hi = x.astype(BF16)
    lo = (x - hi.astype(F32)).astype(BF16)
    return hi, lo


def _dot(a, b):
    return jnp.dot(a, b, preferred_element_type=F32)


def _dot_nt(a, b):
    return lax.dot_general(a, b, (((1,), (1,)), ((), ())), preferred_element_type=F32)


def _dot_tn(a, b):
    return lax.dot_general(a, b, (((0,), (0,)), ((), ())), preferred_element_type=F32)


def _inproj_kernel(h_ref, nw_ref, w_ref, qkv_ref, u_ref, ret_ref, gate_ref):
    x = h_ref[...]
    ms = jnp.mean(x * x, axis=-1, keepdims=True)
    xn = (x * lax.rsqrt(ms + EPS) * nw_ref[...]).astype(BF16)
    off = 0
    for ref in (qkv_ref, u_ref, ret_ref, gate_ref):
        if ref is u_ref:
            u = _dot(xn, w_ref[:, off:off + BRANCH_WIDTH])
            for s in range(u_ref.shape[0]):
                u_ref[s] = u[:, s * LANES:(s + 1) * LANES]
            off += BRANCH_WIDTH
            continue
        width = ref.shape[1]
        for c in range(0, width, 512):
            ref[:, c:c + 512] = _dot(xn, w_ref[:, off + c:off + c + 512]).astype(ref.dtype)
        off += width


def _inproj(h, norm_w, w_in, layer, tm=512):
    t = h.shape[0]
    nslab = BRANCH_WIDTH // LANES

    def rows(w):
        return pl.BlockSpec((tm, w), lambda i: (i, 0))

    return pl.pallas_call(
        _inproj_kernel,
        grid=(t // tm,),
        in_specs=[rows(D_MODEL),
                  pl.BlockSpec((1, D_MODEL), lambda i: (0, 0)),
                  _layer_block(w_in, layer)],
        out_specs=[rows(3 * BRANCH_WIDTH), pl.BlockSpec((nslab, tm, LANES), lambda i: (0, i, 0)),
                   rows(4 * BRANCH_WIDTH), rows(N_BRANCH * D_MODEL)],
        out_shape=[jax.ShapeDtypeStruct((t, 3 * BRANCH_WIDTH), BF16),
                   jax.ShapeDtypeStruct((nslab, t, LANES), F32),
                   jax.ShapeDtypeStruct((t, 4 * BRANCH_WIDTH), BF16),
                   jax.ShapeDtypeStruct((t, N_BRANCH * D_MODEL), BF16)],
        compiler_params=_cparams(("parallel",)),
        name="inproj",
    )(h, norm_w, w_in)


def _sba_kernel(q_ref, k_ref, v_ref, m_ref, o_ref, acc_ref, car_ref, *, blk, nsub):
    step = pl.program_id(2)
    lane = lax.broadcasted_iota(jnp.int32, (blk, LANES), 1)
    zscale = HEAD_DIM ** -0.5 * LOG2_E
    tri = m_ref[...]
    q_heads = []
    for sub in range(nsub):
        q = q_ref[0, sub * blk:(sub + 1) * blk, :]
        zero = jnp.zeros_like(q)
        q_heads.append((jnp.where(lane < HEAD_DIM, q, zero), jnp.where(lane >= HEAD_DIM, q, zero)))

    def keys(j):
        start = pl.multiple_of(j * blk, blk)
        return k_ref[0, pl.ds(start, blk), :], v_ref[0, pl.ds(start, blk), :]

    def log_terms(z2, causal):
        sp = jnp.log(1.0 + jnp.exp2(-jnp.abs(z2))) * LOG2_E
        log_beta = jnp.minimum(z2, 0.0) - sp
        log_1m = log_beta - z2
        if causal is not None:
            log_1m = jnp.where(causal, log_1m, 0.0)
        suffix = _dot(log_1m.astype(BF16), tri)
        return log_beta, suffix, jnp.sum(log_1m, axis=1, keepdims=True)

    def visit(key_blocks, jobs):
        causal = lax.broadcasted_iota(jnp.int32, (blk, blk), 1) < lax.broadcasted_iota(jnp.int32, (blk, blk), 0)
        kv = [keys(j) for j in key_blocks]
        chains = [(n, b, hd) for n, job in enumerate(jobs) for b in range(len(job[1])) for hd in range(2)]

        def mask(n, b):
            return causal if (jobs[n][2] and b == 0) else None

        z2 = {c: _dot_nt(q_heads[jobs[c[0]][0]][c[2]], kv[jobs[c[0]][1][c[1]]][0]) * zscale for c in chains}
        terms = {c: log_terms(z2[c], mask(c[0], c[1])) for c in chains}
        results = []
        for n, (sub, block_ids, _, car_in) in enumerate(jobs):
            per_head = []
            for hd in range(2):
                car = None if car_in is None else car_in[hd]
                acc = None
                for b in range(len(block_ids)):
                    log_beta, suffix, tot = terms[(n, b, hd)]
                    w = jnp.exp2(log_beta + suffix if car is None else log_beta + suffix + car)
                    if mask(n, b) is not None:
                        w = jnp.where(causal, w, 0.0)
                    pv = _dot(w.astype(BF16), kv[block_ids[b]][1])
                    acc = pv if acc is None else acc + pv
                    car = tot if car is None else car + tot
                per_head.append((acc, car))
            results.append(per_head)
        return results

    def first_visit(first_step):
        first = step * nsub
        key_blocks = [first + sub for sub in range(nsub)] + ([] if first_step else [first - 1])
        jobs = []
        for sub in range(nsub):
            ids = [sub]
            if sub > 0:
                ids.append(sub - 1)
            elif not first_step:
                ids.append(nsub)
            jobs.append((sub, ids, True, None))
        for sub, per_head in enumerate(visit(key_blocks, jobs)):
            for hd, (acc, car) in enumerate(per_head):
                acc_ref[sub, hd] = acc
                car_ref[sub, hd] = car

    pl.when(step == 0)(lambda: first_visit(True))
    pl.when(step > 0)(lambda: first_visit(False))

    for sub in range(nsub):
        def mass_left(sub=sub):
            most = jnp.maximum(jnp.max(car_ref[sub, 0]), jnp.max(car_ref[sub, 1]))
            return (most > SB_SKIP_LOG2).astype(jnp.int32)

        def cond(c):
            j, go = c
            return jnp.logical_and(j >= 0, go > 0)

        def body(c, sub=sub, mass_left=mass_left):
            j, _ = c
            (per_head,) = visit([j], [(sub, [0], False, (car_ref[sub, 0], car_ref[sub, 1]))])
            for hd, (acc, car) in enumerate(per_head):
                acc_ref[sub, hd] += acc
                car_ref[sub, hd] = car
            return j - 1, mass_left()

        lax.while_loop(cond, body, (step * nsub + sub - 2, mass_left()))
        o_ref[0, sub * blk:(sub + 1) * blk, :] = jnp.where(lane < HEAD_DIM, acc_ref[sub, 0],
                                                           acc_ref[sub, 1]).astype(o_ref.dtype)


def _sb_attention(qkv, batch, seq, blk=SB_BLOCK, nsub=SB_SUBBLOCKS):
    qkv3 = qkv.reshape(batch, seq, 3 * BRANCH_WIDTH)
    pairs = BRANCH_WIDTH // LANES
    tq = blk * nsub
    r = np.arange(blk)
    m_ext = jnp.asarray(r[:, None] > r[None, :], BF16)
    out = pl.pallas_call(
        functools.partial(_sba_kernel, blk=blk, nsub=nsub),
        grid=(batch, pairs, seq // tq),
        in_specs=[pl.BlockSpec((1, tq, LANES), lambda b, p, i: (b, i, p)),
                  pl.BlockSpec((1, seq, LANES), lambda b, p, i: (b, 0, pairs + p)),
                  pl.BlockSpec((1, seq, LANES), lambda b, p, i: (b, 0, 2 * pairs + p)),
                  pl.BlockSpec(m_ext.shape, lambda b, p, i: (0, 0))],
        out_specs=pl.BlockSpec((1, tq, LANES), lambda b, p, i: (b, i, p)),
        out_shape=jax.ShapeDtypeStruct((batch, seq, BRANCH_WIDTH), BF16),
        scratch_shapes=[pltpu.VMEM((nsub, 2, blk, LANES), F32), pltpu.VMEM((nsub, 2, blk, 1), F32)],
        compiler_params=_cparams(("parallel", "parallel", "parallel")),
        name="sb_attention",
    )(qkv3, qkv3, qkv3, m_ext)
    return out.reshape(batch * seq, BRANCH_WIDTH)


def _s5_operators(lam_re, lam_im, log_dt, b_re, b_im, c_re, c_im, d_skip):
    L = S5_CHUNK
    g, p, c = S5_GROUPS, S5_STATE, S5_GROUP_CH
    dt = jnp.exp(log_dt.astype(F32))[:, None]
    lr = lam_re.astype(F32)
    li = lam_im.astype(F32)

    def a_pow(n):
        n = jnp.asarray(n, F32)[..., None, None]
        mag = jnp.exp(lr * dt * n)
        return mag * jnp.cos(li * dt * n), mag * jnp.sin(li * dt * n)

    ar, ai = a_pow(1.0)
    den = lr * lr + li * li
    nr = ar - 1.0
    fr = (nr * lr + ai * li) / den
    fi = (ai * lr - nr * li) / den
    bbr = fr[..., None] * b_re - fi[..., None] * b_im
    bbi = fr[..., None] * b_im + fi[..., None] * b_re
    cr = c_re.astype(F32)
    ci = c_im.astype(F32)

    pr, pi = a_pow(np.arange(L))
    car = cr[None] * pr[:, :, None, :] - ci[None] * pi[:, :, None, :]
    cai = cr[None] * pi[:, :, None, :] + ci[None] * pr[:, :, None, :]
    klag = jnp.einsum('jgop,gpi->jgoi', car, bbr) - jnp.einsum('jgop,gpi->jgoi', cai, bbi)
    klag = klag.at[0].add(d_skip.astype(F32).reshape(g, c)[:, :, None] * jnp.eye(c, dtype=F32))
    kflat = klag.transpose(1, 3, 0, 2).reshape(g, c, L * c)
    toe = jnp.stack([jnp.pad(kflat[:, :, :(L - s) * c], ((0, 0), (0, 0), (s * c, 0))) for s in range(L)], axis=1)
    toe = toe.reshape(g, L * c, L * c)

    qr, qi = a_pow(L - 1 - np.arange(L))
    p_re = qr[:, :, :, None] * bbr[None] - qi[:, :, :, None] * bbi[None]
    p_im = qr[:, :, :, None] * bbi[None] + qi[:, :, :, None] * bbr[None]
    p_re = p_re.transpose(1, 0, 3, 2).reshape(g, L * c, p)
    p_im = p_im.transpose(1, 0, 3, 2).reshape(g, L * c, p)

    er, ei = a_pow(1 + np.arange(L))
    q_re = (cr[None] * er[:, :, None, :] - ci[None] * ei[:, :, None, :])
    q_im = -(cr[None] * ei[:, :, None, :] + ci[None] * er[:, :, None, :])
    q_re = q_re.transpose(1, 3, 0, 2).reshape(g, p, L * c)
    q_im = q_im.transpose(1, 3, 0, 2).reshape(g, p, L * c)

    a_lr, a_li = a_pow(float(L))

    def pair_diag(m):
        gg, r, cc = m.shape
        m = m.reshape(gg // 2, 2, r, cc)
        z = jnp.zeros_like(m[:, 0])
        top = jnp.concatenate([m[:, 0], z], axis=2)
        bot = jnp.concatenate([z, m[:, 1]], axis=2)
        return jnp.concatenate([top, bot], axis=1)

    toe_p = pair_diag(toe).astype(BF16)
    p_p = jnp.concatenate([pair_diag(p_re), pair_diag(p_im)], axis=2).astype(BF16)
    q_p = jnp.concatenate([pair_diag(q_re), pair_diag(q_im)], axis=1).astype(BF16)
    a_re = a_lr.reshape(1, g * p)
    a_im = a_li.reshape(1, g * p)
    return toe_p, p_p, q_p, a_re, a_im


def _group_block_transpose(tiles):
    group = lax.broadcasted_iota(jnp.int32, tiles[0].shape, 1) // S5_GROUP_CH
    n = len(tiles)
    k = n // 2
    while k >= 1:
        upper = (group & k) != 0
        nxt = list(tiles)
        for i in range(n):
            if i & k == 0:
                lo, hi = tiles[i], tiles[i + k]
                nxt[i] = jnp.where(upper, pltpu.roll(hi, k * S5_GROUP_CH, 1), lo)
                nxt[i + k] = jnp.where(upper, hi, pltpu.roll(lo, LANES - k * S5_GROUP_CH, 1))
        tiles = nxt
        k //= 2
    return tiles


def _s5_state_in_kernel(u_ref, p_ref, up_ref, vre_ref, vim_ref, *, nb):
    L = S5_CHUNK
    per = LANES // S5_GROUP_CH
    for kb in range(u_ref.shape[0]):
        halves = []
        for a in range(L // per):
            steps = [u_ref[kb, pl.ds(a * per + b, nb, stride=L), :] for b in range(per)]
            halves.append(_group_block_transpose(steps))
        for pi in range(per // 2):
            pair = kb * (per // 2) + pi
            row = jnp.concatenate([halves[a][2 * pi + gi] for gi in range(2) for a in range(L // per)], axis=1)
            row = row.astype(BF16)
            up_ref[pair] = row
            v = _dot(row, p_ref[pair])
            vre_ref[:, pair * LANES:(pair + 1) * LANES] = v[:, :LANES]
            vim_ref[:, pair * LANES:(pair + 1) * LANES] = v[:, LANES:]


def _s5_out_kernel(up_ref, sre_ref, sim_ref, toe_ref, q_ref, y_ref, *, nb):
    L = S5_CHUNK
    per = LANES // S5_GROUP_CH
    for kb in range(y_ref.shape[0]):
        halves = [[None] * per for _ in range(L // per)]
        for pi in range(per // 2):
            pair = kb * (per // 2) + pi
            q = q_ref[pair]
            cols = slice(pair * LANES, (pair + 1) * LANES)
            y = _dot(up_ref[pair], toe_ref[pair])
            y += _dot(sre_ref[:, cols].astype(BF16), q[:LANES]) + _dot(sim_ref[:, cols].astype(BF16), q[LANES:])
            y = jax.nn.gelu(y)
            for gi in range(2):
                for a in range(L // per):
                    j = gi * (L // per) + a
                    halves[a][2 * pi + gi] = y[:, j * LANES:(j + 1) * LANES]
        for a in range(L // per):
            steps = _group_block_transpose(halves[a])
            for b in range(per):
                y_ref[kb, pl.ds(a * per + b, nb, stride=L), :] = steps[b]


def _s5_kernel(u_ref, p_ref, toe_ref, q_ref, are_ref, aim_ref, y_ref, up_ref, vre_ref, vim_ref, sre_ref, sim_ref,
               x_ref, *, nb, steps_per_seq):
    @pl.when(pl.program_id(0) % steps_per_seq == 0)
    def _():
        x_ref[...] = jnp.zeros_like(x_ref)

    _s5_state_in_kernel(u_ref, p_ref, up_ref, vre_ref, vim_ref, nb=nb)
    ar = are_ref[...]
    ai = aim_ref[...]

    def step(r, carry):
        xr, xi = carry
        sre_ref[pl.ds(r, 1), :] = xr
        sim_ref[pl.ds(r, 1), :] = xi
        vr = vre_ref[pl.ds(r, 1), :]
        vi = vim_ref[pl.ds(r, 1), :]
        return ar * xr - ai * xi + vr, ar * xi + ai * xr + vi

    xr, xi = lax.fori_loop(0, nb, step, (x_ref[0:1, :], x_ref[1:2, :]))
    x_ref[0:1, :] = xr
    x_ref[1:2, :] = xi
    _s5_out_kernel(up_ref, sre_ref, sim_ref, toe_ref, q_ref, y_ref, nb=nb)


def _s5_ssm(u, batch, seq, ops, nb=128):
    toe_p, p_p, q_p, a_re, a_im = ops
    nslab, t, _ = u.shape
    L = S5_CHUNK
    npair = S5_GROUPS // 2
    nc = t // L
    row = 2 * L * S5_GROUP_CH
    ncols = S5_GROUPS * S5_STATE
    nb = min(nb, seq // L)

    def full(a):
        return pl.BlockSpec(a.shape, lambda i: (0,) * a.ndim, pipeline_mode=pl.Buffered(1))

    slab_spec = pl.BlockSpec((nslab, nb * L, LANES), lambda i: (0, i, 0))
    return pl.pallas_call(
        functools.partial(_s5_kernel, nb=nb, steps_per_seq=seq // (L * nb)),
        grid=(nc // nb,),
        in_specs=[slab_spec, full(p_p), full(toe_p), full(q_p), full(a_re), full(a_im)],
        out_specs=slab_spec,
        out_shape=jax.ShapeDtypeStruct((nslab, t, LANES), F32),
        scratch_shapes=[pltpu.VMEM((npair, nb, row), BF16)] + [pltpu.VMEM((nb, ncols), F32)] * 4
        + [pltpu.VMEM((8, ncols), F32)],
        compiler_params=_cparams(("arbitrary",)),
        name="s5_ssm",
    )(u, p_p, toe_p, q_p, a_re, a_im)


def _ret_tables(seq):
    half = HEAD_DIM // 2
    pos = jnp.arange(seq, dtype=F32)
    inv = ROPE_BASE ** (-jnp.arange(half, dtype=F32) / half)
    ang = pos[:, None] * inv[None, :]
    cos = jnp.tile(jnp.cos(ang), (1, LANES // half))
    sin = jnp.sin(ang)
    sin = jnp.tile(jnp.concatenate([-sin, sin], axis=1), (1, LANES // HEAD_DIM))
    c = RET_CHUNK
    gamma = 1.0 - 2.0 ** (-5.0 - jnp.arange(RET_HEADS, dtype=F32))
    log_g = jnp.log(gamma)
    idx = jnp.arange(c, dtype=F32)
    diff = idx[:, None] - idx[None, :]
    intra = jnp.where(diff >= 0, jnp.exp(jnp.maximum(diff, 0.0)[None] * log_g[:, None, None]), 0.0)
    q_dec = jnp.exp((idx + 1.0)[None, :] * log_g[:, None])
    k_dec = jnp.exp((c - 1.0 - idx)[None, :] * log_g[:, None])
    ch_dec = jnp.exp(c * log_g)

    def lanes(x):
        x = x.reshape(RET_HEADS // 2, 2, c)
        return jnp.repeat(x.transpose(0, 2, 1), HEAD_DIM, axis=2)

    q_dec_l = lanes(q_dec)
    k_dec_l = lanes(k_dec)
    hp = RET_HEADS // 2
    blockdiag = np.kron(np.eye(2), np.ones((HEAD_DIM, HEAD_DIM)))
    st_dec = jnp.repeat(ch_dec.reshape(hp, 2), HEAD_DIM, axis=1)[:, :, None] * blockdiag[None]
    avg = jnp.asarray(blockdiag / HEAD_DIM, BF16)
    return cos, sin, intra, q_dec_l, k_dec_l, st_dec.astype(F32), avg


def _ret_kernel(q_ref, k_ref, v_ref, g_ref, cos_ref, sin_ref, intra_ref, qd_ref, kd_ref, sd_ref, avg_ref, nw_ref,
                o_ref, st_ref, *, nchunk):
    c = RET_CHUNK

    @pl.when(pl.program_id(2) == 0)
    def _():
        st_ref[...] = jnp.zeros_like(st_ref)

    tq = q_ref.shape[1]
    lane = lax.broadcasted_iota(jnp.int32, (tq, LANES), 1)
    first_half = (lane % HEAD_DIM) < (HEAD_DIM // 2)
    head0 = lax.broadcasted_iota(jnp.int32, (c, LANES), 1) < HEAD_DIM
    sd = sd_ref[0]
    avg = avg_ref[...]
    same_head = avg > 0
    cos = cos_ref[...]
    sin = sin_ref[...]

    def rotary(x):
        swapped = jnp.where(first_half, pltpu.roll(x, LANES - HEAD_DIM // 2, 1), pltpu.roll(x, HEAD_DIM // 2, 1))
        return x * cos + swapped * sin

    q_all = rotary(q_ref[0].astype(F32)).astype(BF16)
    k_all = rotary(k_ref[0].astype(F32)) * (HEAD_DIM ** -0.5)
    kb_all = k_all.astype(BF16)
    zero = jnp.zeros((c, LANES), BF16)
    chunks = [slice(ci * c, (ci + 1) * c) for ci in range(nchunk)]
    vs = [v_ref[0, rows, :] for rows in chunks]
    scores, incs = [], []
    for ci, rows in enumerate(chunks):
        qb = q_all[rows]
        q2 = jnp.concatenate([jnp.where(head0, qb, zero), jnp.where(head0, zero, qb)], axis=0)
        scores.append(_dot_nt(q2, kb_all[rows]))
        kd_t = (k_all[rows] * kd_ref[0]).T.astype(BF16)
        incs.append(_dot(kd_t, vs[ci]))
    state = st_ref[...]
    states = []
    for inc in incs:
        states.append(state.astype(BF16))
        state = state * sd + jnp.where(same_head, inc, 0.0)
    st_ref[...] = state
    intra2 = jnp.concatenate([intra_ref[0, 0], intra_ref[0, 1]], axis=0)
    outs = []
    for ci, rows in enumerate(chunks):
        o2 = _dot((scores[ci] * intra2).astype(BF16), vs[ci])
        outs.append(jnp.where(head0, o2[:c], o2[c:]) + _dot(q_all[rows], states[ci]) * qd_ref[0])
    out = jnp.concatenate(outs, axis=0)
    hi, lo = _split_bf16(out)
    mu = _dot(hi, avg) + _dot(lo, avg)
    dlt = out - mu
    hi, lo = _split_bf16(dlt * dlt)
    var = _dot(hi, avg) + _dot(lo, avg)
    gate = g_ref[0].astype(F32)
    y = dlt * lax.rsqrt(var + EPS) * nw_ref[...] * (gate * jax.nn.sigmoid(gate))
    o_ref[0] = y.astype(o_ref.dtype)


def _retention(ret, norm_w, batch, seq, tables, tq=1024):
    cos, sin, intra, q_dec_l, k_dec_l, st_dec, avg = tables
    pairs = BRANCH_WIDTH // LANES
    ret3 = ret.reshape(batch, seq, 4 * BRANCH_WIDTH)
    tq = min(tq, seq)
    c = RET_CHUNK

    def col(k):
        return pl.BlockSpec((1, tq, LANES), lambda b, p, i: (b, i, k * pairs + p))

    out = pl.pallas_call(
        functools.partial(_ret_kernel, nchunk=tq // c),
        grid=(batch, pairs, seq // tq),
        in_specs=[col(0), col(1), col(2), col(3),
                  pl.BlockSpec((tq, LANES), lambda b, p, i: (i, 0)),
                  pl.BlockSpec((tq, LANES), lambda b, p, i: (i, 0)),
                  pl.BlockSpec((1, 2, c, c), lambda b, p, i: (p, 0, 0, 0)),
                  pl.BlockSpec((1, c, LANES), lambda b, p, i: (p, 0, 0)),
                  pl.BlockSpec((1, c, LANES), lambda b, p, i: (p, 0, 0)),
                  pl.BlockSpec((1, LANES, LANES), lambda b, p, i: (p, 0, 0)),
                  pl.BlockSpec((LANES, LANES), lambda b, p, i: (0, 0)),
                  pl.BlockSpec((1, LANES), lambda b, p, i: (0, p))],
        out_specs=pl.BlockSpec((1, tq, LANES), lambda b, p, i: (b, i, p)),
        out_shape=jax.ShapeDtypeStruct((batch, seq, BRANCH_WIDTH), BF16),
        scratch_shapes=[pltpu.VMEM((LANES, LANES), F32)],
        compiler_params=_cparams(("parallel", "parallel", "arbitrary")),
        name="retention",
    )(ret3, ret3, ret3, ret3, cos, sin, intra.reshape(pairs, 2, c, c), q_dec_l, k_dec_l, st_dec, avg,
      norm_w.reshape(1, BRANCH_WIDTH).astype(F32))
    return out.reshape(batch * seq, BRANCH_WIDTH)


def _route(logits):
    lane = lax.broadcasted_iota(jnp.int32, logits.shape, 1).astype(F32)
    neg = jnp.float32(-jnp.inf)
    big = jnp.float32(1 << 20)
    gl = jnp.where(lane < N_GROUPS, logits, neg)
    gmax = jnp.max(gl, axis=1, keepdims=True)
    gsum = jnp.sum(jnp.exp(gl - gmax), axis=1, keepdims=True)
    g_w = 1.0 / gsum
    g_idx = jnp.min(jnp.where(gl == gmax, lane, big), axis=1, keepdims=True)
    lo = ROUTER_LANE0 + EXPERTS_PER_GROUP * g_idx
    sl = jnp.where((lane >= lo) & (lane < lo + EXPERTS_PER_GROUP), logits, neg)
    m1 = jnp.max(sl, axis=1, keepdims=True)
    i1 = jnp.min(jnp.where(sl == m1, lane, big), axis=1, keepdims=True)
    sl2 = jnp.where(lane == i1, neg, sl)
    m2 = jnp.max(sl2, axis=1, keepdims=True)
    i2 = jnp.min(jnp.where(sl2 == m2, lane, big), axis=1, keepdims=True)
    e2 = jnp.exp(m2 - m1)
    w1 = g_w / (1.0 + e2)
    w2 = g_w * e2 / (1.0 + e2)
    packed = jnp.where(lane == 0.0, i1 - ROUTER_LANE0, jnp.where(lane == 1.0, i2 - ROUTER_LANE0, 0.0))
    return packed + jnp.where(lane == 2.0, w1, 0.0) + jnp.where(lane == 3.0, w2, 0.0)


def _merge_kernel(ysb_ref, ys5_ref, yret_ref, gate_ref, h_ref, wglu_ref, wbr_ref, wout_ref, n2_ref, wr_hi_ref,
                  wr_lo_ref, br_ref, h_out_ref, hn_ref, route_ref):
    ys5 = jnp.concatenate([ys5_ref[s] for s in range(ys5_ref.shape[0])], axis=1).astype(BF16)
    glu = _dot(ys5, wglu_ref[...])
    y_s5 = (glu[:, :BRANCH_WIDTH] * jax.nn.sigmoid(glu[:, BRANCH_WIDTH:])).astype(BF16)
    merged = None
    for n, y in enumerate((ysb_ref[...], y_s5, yret_ref[...])):
        g = jax.nn.sigmoid(gate_ref[:, n * D_MODEL:(n + 1) * D_MODEL].astype(F32))
        term = g * _dot(y, wbr_ref[n])
        merged = term if merged is None else merged + term
    h = h_ref[...] + _dot(merged.astype(BF16), wout_ref[...])
    h_out_ref[...] = h
    ms = jnp.mean(h * h, axis=-1, keepdims=True)
    hn = h * lax.rsqrt(ms + EPS) * n2_ref[...]
    hn_ref[...] = hn.astype(BF16)
    hi, lo = _split_bf16(hn)
    logits = _dot(hi, wr_hi_ref[...]) + _dot(hi, wr_lo_ref[...]) + _dot(lo, wr_hi_ref[...]) + br_ref[...]
    route_ref[...] = _route(logits)


def _merge(y_sb, y_s5, y_ret, gates, h, w_glu, w_br, w_out, norm2_w, wr_hi, wr_lo, b_r, layer, tm=512):
    t = h.shape[0]

    def rows(w):
        return pl.BlockSpec((tm, w), lambda i: (i, 0))

    def full(a):
        return pl.BlockSpec(a.shape, lambda i: (0,) * a.ndim)

    return pl.pallas_call(
        _merge_kernel,
        grid=(t // tm,),
        in_specs=[rows(BRANCH_WIDTH), pl.BlockSpec((y_s5.shape[0], tm, LANES), lambda i: (0, i, 0)),
                  rows(BRANCH_WIDTH), rows(N_BRANCH * D_MODEL),
                  rows(D_MODEL), _layer_block(w_glu, layer), _layer_block(w_br, layer), _layer_block(w_out, layer),
                  full(norm2_w), full(wr_hi), full(wr_lo), full(b_r)],
        out_specs=[rows(D_MODEL), rows(D_MODEL), rows(LANES)],
        out_shape=[jax.ShapeDtypeStruct((t, D_MODEL), F32), jax.ShapeDtypeStruct((t, D_MODEL), BF16),
                   jax.ShapeDtypeStruct((t, LANES), F32)],
        compiler_params=_cparams(("parallel",)),
        name="merge_route",
    )(y_sb, y_s5, y_ret, gates, h, w_glu, w_br, w_out, norm2_w, wr_hi, wr_lo, b_r)


MOE_TILE = 512
MOE_ROW_TILE = 512
SEG_ALIGN = 16
MOE_SLOTS = TOP_K * MOE_TILE + N_EXPERTS * SEG_ALIGN
MOE_CHUNKS = MOE_SLOTS // SEG_ALIGN


def _moe_plan_kernel(route_ref, tri_ref, cum_ref, pos_ref, cnt_ref):
    r = route_ref[...]
    tm = r.shape[0]
    lane = lax.broadcasted_iota(jnp.int32, r.shape, 1).astype(F32)
    e1 = r[:, 0:1]
    e2 = r[:, 1:2]
    chosen = (lane == e1) | (lane == e2)
    onehot = jnp.where(chosen, 1.0, 0.0).astype(BF16)
    incl = _dot(tri_ref[...], onehot)
    cnt = incl[tm - 1:tm, :]
    padded = jnp.floor((cnt + (SEG_ALIGN - 1)) * (1.0 / SEG_ALIGN)) * SEG_ALIGN
    start = _dot(jnp.broadcast_to(padded, (8, LANES)).astype(BF16), cum_ref[...])[0:1, :]
    slot = start + incl - 1.0
    p1 = jnp.sum(jnp.where(lane == e1, slot, 0.0), axis=1, keepdims=True)
    p2 = jnp.sum(jnp.where(lane == e2, slot, 0.0), axis=1, keepdims=True)
    pos_ref[...] = jnp.where(lane == 0.0, p1, jnp.where(lane == 1.0, p2, 0.0))
    cnt_ref[0] = jnp.broadcast_to(cnt, (8, LANES))


def _moe_plan(route, tm):
    t = route.shape[0]
    r = np.arange(tm)
    tri = jnp.asarray(r[None, :] <= r[:, None], BF16)
    l = np.arange(LANES)
    cum = jnp.asarray(l[:, None] < l[None, :], BF16)
    return pl.pallas_call(
        _moe_plan_kernel,
        grid=(t // tm,),
        in_specs=[pl.BlockSpec((tm, LANES), lambda i: (i, 0)),
                  pl.BlockSpec((tm, tm), lambda i: (0, 0)),
                  pl.BlockSpec((LANES, LANES), lambda i: (0, 0))],
        out_specs=[pl.BlockSpec((tm, LANES), lambda i: (i, 0)), pl.BlockSpec((1, 8, LANES), lambda i: (i, 0, 0))],
        out_shape=[jax.ShapeDtypeStruct((t, LANES), F32), jax.ShapeDtypeStruct((t // tm, 8, LANES), F32)],
        compiler_params=_cparams(("parallel",)),
        name="moe_plan",
    )(route, tri, cum)


def _moe_tables(cnt, rt, n_rows):
    a = SEG_ALIGN
    ntile = cnt.shape[0]
    p = (cnt + a - 1) // a * a
    lend = jnp.cumsum(p, axis=1)
    lstart = lend - p
    seg = p.sum(axis=0)
    segpad = (seg + rt - 1) // rt * rt
    gend = jnp.cumsum(segpad)
    gstart = gend - segpad
    toff = gstart[None, :] + jnp.cumsum(p, axis=0) - p
    cstart = a * jnp.arange(MOE_CHUNKS, dtype=jnp.int32)
    in_run = (cstart[None, :, None] >= lstart[:, None, :]) & (cstart[None, :, None] < lend[:, None, :])
    gdst = (jnp.sum(jnp.where(in_run, (toff - lstart)[:, None, :], 0), axis=2) + cstart[None, :]) // a
    nch = lend[:, -1] // a
    tstart = rt * jnp.arange(n_rows // rt, dtype=jnp.int32)
    tile_e = jnp.minimum(jnp.sum(tstart[:, None] >= gend[None, :], axis=1), N_EXPERTS - 1)
    n_used = gend[-1] // rt
    zmax = rt // a - 1
    zper = -(-N_EXPERTS * zmax // ntile)
    k = jnp.arange(zmax, dtype=jnp.int32)
    zd = ((gstart + seg) // a)[:, None] + k[None, :]
    valid = (k[None, :] < ((segpad - seg) // a)[:, None]).reshape(-1)
    order = jnp.argsort(jnp.logical_not(valid), stable=True)
    zdst = jnp.pad(zd.reshape(-1)[order], (0, ntile * zper - N_EXPERTS * zmax))
    nz = valid.sum()
    i32 = jnp.int32
    return (gdst.reshape(-1).astype(i32), nch.astype(i32), zdst.astype(i32), nz.reshape(1).astype(i32),
            tile_e.astype(i32), n_used.reshape(1).astype(i32)), zper


def _moe_dispatch_kernel(gdst_ref, nch_ref, zdst_ref, nz_ref, n_used_ref, pos_ref, hn_ref, xs_hbm, stg, zblk, sem,
                         zsem, *, zper):
    i = pl.program_id(0)
    last_step = pl.num_programs(0) - 1
    buf = i % 2
    tm = hn_ref.shape[0]
    pos_t = pos_ref[...].T
    p1 = pos_t[0:1, :]
    p2 = pos_t[1:2, :]
    x = hn_ref[...]
    rb = MOE_ROW_TILE
    for b in range(MOE_SLOTS // rb):
        slot = (lax.broadcasted_iota(jnp.int32, (rb, tm), 0) + b * rb).astype(F32)
        onehot = jnp.where((slot == p1) | (slot == p2), 1.0, 0.0).astype(BF16)
        stg[buf, b * rb:(b + 1) * rb, :] = _dot(onehot, x).astype(BF16)

    def chunk_copy(which, src_chunk, dst_chunk):
        src = pl.multiple_of(src_chunk * SEG_ALIGN, SEG_ALIGN)
        dst = pl.multiple_of(dst_chunk * SEG_ALIGN, SEG_ALIGN)
        return pltpu.make_async_copy(stg.at[which, pl.ds(src, SEG_ALIGN)], xs_hbm.at[pl.ds(dst, SEG_ALIGN)],
                                     sem.at[which])

    def n_copies(step):
        return nch_ref[step], jnp.clip(nz_ref[0] - step * zper, 0, zper)

    def drain(which, step):
        for count in n_copies(step):
            rows = count * SEG_ALIGN

            @pl.when(rows > 0)
            def _():
                pltpu.make_async_copy(stg.at[which, pl.ds(0, rows)], xs_hbm.at[pl.ds(0, rows)],
                                      sem.at[which]).wait()

    n, nzero = n_copies(i)

    @pl.loop(0, n)
    def _(c):
        chunk_copy(buf, c, gdst_ref[i * MOE_CHUNKS + c]).start()

    @pl.loop(0, nzero)
    def _(k):
        chunk_copy(buf, MOE_CHUNKS - 1, zdst_ref[i * zper + k]).start()

    pl.when(i > 0)(lambda: drain(1 - buf, i - 1))

    @pl.when(i == last_step)
    def _():
        drain(buf, i)
        rt = zblk.shape[0]
        zblk[...] = jnp.zeros_like(zblk)
        first = n_used_ref[0]
        last = xs_hbm.shape[0] // rt

        def tile_copy(j):
            return pltpu.make_async_copy(zblk, xs_hbm.at[pl.ds(pl.multiple_of(j * rt, rt), rt)], zsem)

        @pl.loop(first, last)
        def _(j):
            tile_copy(j).start()

        @pl.loop(first, last)
        def _(j):
            tile_copy(0).wait()


def _moe_expert_kernel(tile_e_ref, n_used_ref, xs_ref, wg_ref, wu_ref, wd_ref, ys_ref):
    used = pl.program_id(0) < n_used_ref[0]

    @pl.when(used)
    def _():
        x = xs_ref[...]
        gate = _dot(x, wg_ref[0].astype(BF16))
        up = _dot(x, wu_ref[0].astype(BF16))
        hdn = gate * jax.nn.sigmoid(gate) * up
        ys_ref[...] = _dot(hdn.astype(BF16), wd_ref[0].astype(BF16)).astype(ys_ref.dtype)

    @pl.when(jnp.logical_not(used))
    def _():
        ys_ref[...] = jnp.zeros_like(ys_ref)


def _moe_combine_kernel(gdst_ref, nch_ref, route_ref, pos_ref, h_ref, nw_ref, ys_hbm, o_ref, stg, sem, *, out_norm):
    i = pl.program_id(0)
    nsteps = pl.num_programs(0)
    buf = i % 2
    tm = h_ref.shape[0]

    def chunk_copy(which, src_chunk, dst_chunk):
        src = pl.multiple_of(src_chunk * SEG_ALIGN, SEG_ALIGN)
        dst = pl.multiple_of(dst_chunk * SEG_ALIGN, SEG_ALIGN)
        return pltpu.make_async_copy(ys_hbm.at[pl.ds(src, SEG_ALIGN)], stg.at[which, pl.ds(dst, SEG_ALIGN)],
                                     sem.at[which])

    def fetch(step, which):
        n = nch_ref[step]

        @pl.loop(0, n)
        def _(c):
            chunk_copy(which, gdst_ref[step * MOE_CHUNKS + c], c).start()

        @pl.loop(n, MOE_CHUNKS)
        def _(c):
            rows = pl.ds(pl.multiple_of(c * SEG_ALIGN, SEG_ALIGN), SEG_ALIGN)
            stg[which, rows, :] = jnp.zeros((SEG_ALIGN, D_MODEL), BF16)

    pl.when(i == 0)(lambda: fetch(0, 0))
    pl.when(i + 1 < nsteps)(lambda: fetch(i + 1, 1 - buf))

    r = route_ref[...]
    pos = pos_ref[...]
    slot = lax.broadcasted_iota(jnp.int32, (tm, MOE_SLOTS), 1).astype(F32)
    w = jnp.where(slot == pos[:, 0:1], r[:, 2:3], 0.0) + jnp.where(slot == pos[:, 1:2], r[:, 3:4], 0.0)

    rows = nch_ref[i] * SEG_ALIGN

    @pl.when(rows > 0)
    def _():
        pltpu.make_async_copy(ys_hbm.at[pl.ds(0, rows)], stg.at[buf, pl.ds(0, rows)], sem.at[buf]).wait()

    out = h_ref[...] + _dot(w.astype(BF16), stg[buf])
    if out_norm:
        ms = jnp.mean(out * out, axis=-1, keepdims=True)
        out = out * lax.rsqrt(ms + EPS) * nw_ref[...]
    o_ref[...] = out


def _moe(hn, route, h, w_gate, w_up, w_down, layer, out_norm_w, out_norm):
    t = h.shape[0]
    tm = min(MOE_TILE, t)
    rt = MOE_ROW_TILE
    ntile = t // tm
    n_rows = TOP_K * t + ntile * N_EXPERTS * (SEG_ALIGN - 1) + N_EXPERTS * (rt - SEG_ALIGN)
    n_rows = (n_rows + rt - 1) // rt * rt
    pos, cnt = _moe_plan(route, tm)
    (gdst, nch, zdst, nz, tile_e, n_used), zper = _moe_tables(cnt[:, 0, :N_EXPERTS].astype(jnp.int32), rt, n_rows)
    xs = pl.pallas_call(
        functools.partial(_moe_dispatch_kernel, zper=zper),
        grid_spec=pltpu.PrefetchScalarGridSpec(
            num_scalar_prefetch=5,
            grid=(ntile,),
            in_specs=[pl.BlockSpec((tm, LANES), lambda i, *_: (i, 0)),
                      pl.BlockSpec((tm, D_MODEL), lambda i, *_: (i, 0))],
            out_specs=pl.BlockSpec(memory_space=pl.ANY),
            scratch_shapes=[pltpu.VMEM((2, MOE_SLOTS, D_MODEL), BF16), pltpu.VMEM((rt, D_MODEL), BF16),
                            pltpu.SemaphoreType.DMA((2,)), pltpu.SemaphoreType.DMA(())]),
        out_shape=jax.ShapeDtypeStruct((n_rows, D_MODEL), BF16),
        compiler_params=_cparams(("arbitrary",)),
        name="moe_dispatch",
    )(gdst, nch, zdst, nz, n_used, pos, hn)
    ys = pl.pallas_call(
        _moe_expert_kernel,
        grid_spec=pltpu.PrefetchScalarGridSpec(
            num_scalar_prefetch=2,
            grid=(n_rows // rt,),
            in_specs=[pl.BlockSpec((rt, D_MODEL), lambda j, te, nu: (jnp.minimum(j, nu[0] - 1), 0)),
                      pl.BlockSpec((None, 1, D_MODEL, EXPERT_FF), lambda j, te, nu: (layer, te[j], 0, 0)),
                      pl.BlockSpec((None, 1, D_MODEL, EXPERT_FF), lambda j, te, nu: (layer, te[j], 0, 0)),
                      pl.BlockSpec((None, 1, EXPERT_FF, D_MODEL), lambda j, te, nu: (layer, te[j], 0, 0))],
            out_specs=pl.BlockSpec((rt, D_MODEL), lambda j, te, nu: (j, 0))),
        out_shape=jax.ShapeDtypeStruct((n_rows, D_MODEL), BF16),
        compiler_params=_cparams(("arbitrary",)),
        name="moe_experts",
    )(tile_e, n_used, xs, w_gate, w_up, w_down)
    return pl.pallas_call(
        functools.partial(_moe_combine_kernel, out_norm=out_norm),
        grid_spec=pltpu.PrefetchScalarGridSpec(
            num_scalar_prefetch=2,
            grid=(ntile,),
            in_specs=[pl.BlockSpec((tm, LANES), lambda i, *_: (i, 0)),
                      pl.BlockSpec((tm, LANES), lambda i, *_: (i, 0)),
                      pl.BlockSpec((tm, D_MODEL), lambda i, *_: (i, 0)),
                      pl.BlockSpec((1, D_MODEL), lambda i, *_: (0, 0)),
                      pl.BlockSpec(memory_space=pl.ANY)],
            out_specs=pl.BlockSpec((tm, D_MODEL), lambda i, *_: (i, 0)),
            scratch_shapes=[pltpu.VMEM((2, MOE_SLOTS, D_MODEL), BF16), pltpu.SemaphoreType.DMA((2,))]),
        out_shape=jax.ShapeDtypeStruct((t, D_MODEL), F32),
        compiler_params=_cparams(("arbitrary",)),
        name="moe_combine",
    )(gdst, nch, route, pos, h, out_norm_w, ys)


def kernel(x, norm1_w, w_in, s5_lambda_re, s5_lambda_im, s5_log_dt, s5_b_re, s5_b_im, s5_c_re, s5_c_im, s5_d,
           s5_w_glu, ret_norm_w, w_branch, w_out, norm2_w, w_group, b_group, w_router, b_router, w_gate, w_up,
           w_down, final_norm_w):
    batch, seq, _ = x.shape
    depth = w_in.shape[0]
    t = batch * seq
    h = x.reshape(t, D_MODEL).astype(F32)
    ret_tables = _ret_tables(seq)
    w_in, s5_w_glu, w_branch, w_out = (w.astype(BF16) for w in (w_in, s5_w_glu, w_branch, w_out))
    for layer in range(depth):
        qkv, u_s5, ret, gates = _inproj(h, norm1_w[layer].reshape(1, D_MODEL).astype(F32), w_in, layer)
        y_sb = _sb_attention(qkv, batch, seq)
        s5_ops = _s5_operators(s5_lambda_re[layer], s5_lambda_im[layer], s5_log_dt[layer], s5_b_re[layer],
                               s5_b_im[layer], s5_c_re[layer], s5_c_im[layer], s5_d[layer])
        y_s5 = _s5_ssm(u_s5, batch, seq, s5_ops)
        y_ret = _retention(ret, ret_norm_w[layer], batch, seq, ret_tables)
        w_r = jnp.concatenate([w_group[layer], w_router[layer].transpose(1, 0, 2).reshape(D_MODEL, N_EXPERTS)], axis=1)
        w_r = jnp.pad(w_r.astype(F32), ((0, 0), (0, LANES - w_r.shape[1])))
        wr_hi, wr_lo = _split_bf16(w_r)
        b_r = jnp.pad(jnp.concatenate([b_group[layer], b_router[layer].reshape(-1)]).astype(F32),
                      (0, LANES - N_GROUPS - N_EXPERTS)).reshape(1, LANES)
        h, hn, route = _merge(y_sb, y_s5, y_ret, gates, h, s5_w_glu, w_branch, w_out,
                              norm2_w[layer].reshape(1, D_MODEL).astype(F32), wr_hi, wr_lo, b_r, layer)
        h = _moe(hn, route, h, w_gate, w_up, w_down, layer, final_norm_w.reshape(1, D_MODEL).astype(F32),
                 out_norm=(layer == depth - 1))
    return h.reshape(batch, seq, D_MODEL).astype(x.dtype)
```

```python
import functools
import math

import jax
import jax.numpy as jnp
import numpy as np
from jax import lax
from jax.experimental import pallas as pl
from jax.experimental.pallas import tpu as pltpu

F32 = jnp.float32
BF16 = jnp.bfloat16

D_MODEL = 1024
SB_HEADS = 8
HEAD_DIM = 64
BRANCH_WIDTH = 512
S5_GROUPS = 32
S5_GROUP_CH = 16
S5_STATE = 64
RET_HEADS = 8
RET_CHUNK = 128
ROPE_BASE = 10000.0
N_BRANCH = 3
N_GROUPS = 4
EXPERTS_PER_GROUP = 8
N_EXPERTS = 32
TOP_K = 2
EXPERT_FF = 256
EPS = 1e-6

LANES = 128
VMEM_LIMIT = 56 * 1024 * 1024
S5_CHUNK = 16
SB_BLOCK = 256
SB_SUBBLOCKS = 4
LOG2_E = 1.4426950408889634
SB_SKIP_LOG2 = -152.0
ROUTER_LANE0 = N_GROUPS


def _cparams(sem):
    return pltpu.CompilerParams(dimension_semantics=sem, vmem_limit_bytes=VMEM_LIMIT)


def _layer_block(stacked, layer):
    zeros = (0,) * (stacked.ndim - 1)
    return pl.BlockSpec((None,) + stacked.shape[1:], lambda *_: (layer,) + zeros, pipeline_mode=pl.Buffered(1))


def _split_bf16(x):
    hi = x.astype(BF16)
    lo = (x - hi.astype(F32)).astype(BF16)
    return hi, lo


def _dot(a, b):
    return jnp.dot(a, b, preferred_element_type=F32)


def _dot_nt(a, b):
    return lax.dot_general(a, b, (((1,), (1,)), ((), ())), preferred_element_type=F32)


def _dot_tn(a, b):
    return lax.dot_general(a, b, (((0,), (0,)), ((), ())), preferred_element_type=F32)


def _inproj_kernel(h_ref, nw_ref, w_ref, qkv_ref, u_ref, ret_ref, gate_ref):
    x = h_ref[...]
    ms = jnp.mean(x * x, axis=-1, keepdims=True)
    xn = (x * lax.rsqrt(ms + EPS) * nw_ref[...]).astype(BF16)
    off = 0
    for ref in (qkv_ref, u_ref, ret_ref, gate_ref):
        if ref is u_ref:
            u = _dot(xn, w_ref[:, off:off + BRANCH_WIDTH])
            for s in range(u_ref.shape[0]):
                u_ref[s] = u[:, s * LANES:(s + 1) * LANES]
            off += BRANCH_WIDTH
            continue
        width = ref.shape[1]
        for c in range(0, width, 512):
            ref[:, c:c + 512] = _dot(xn, w_ref[:, off + c:off + c + 512]).astype(ref.dtype)
        off += width


def _inproj(h, norm_w, w_in, layer, tm=512):
    t = h.shape[0]
    nslab = BRANCH_WIDTH // LANES

    def rows(w):
        return pl.BlockSpec((tm, w), lambda i: (i, 0))

    return pl.pallas_call(
        _inproj_kernel,
        grid=(t // tm,),
        in_specs=[rows(D_MODEL),
                  _layer_block(norm_w, layer), _layer_block(w_in, layer)],
        out_specs=[rows(3 * BRANCH_WIDTH), pl.BlockSpec((nslab, tm, LANES), lambda i: (0, i, 0)),
                   rows(4 * BRANCH_WIDTH), rows(N_BRANCH * D_MODEL)],
        out_shape=[jax.ShapeDtypeStruct((t, 3 * BRANCH_WIDTH), BF16),
                   jax.ShapeDtypeStruct((nslab, t, LANES), F32),
                   jax.ShapeDtypeStruct((t, 4 * BRANCH_WIDTH), BF16),
                   jax.ShapeDtypeStruct((t, N_BRANCH * D_MODEL), BF16)],
        compiler_params=_cparams(("parallel",)),
        name="inproj",
    )(h, norm_w, w_in)


def _sba_kernel(q_ref, k_ref, v_ref, m_ref, o_ref, acc_ref, car_ref, *, blk, nsub):
    step = pl.program_id(2)
    lane = lax.broadcasted_iota(jnp.int32, (blk, LANES), 1)
    zscale = HEAD_DIM ** -0.5 * LOG2_E
    tri = m_ref[...]
    q_heads = []
    for sub in range(nsub):
        q = q_ref[0, sub * blk:(sub + 1) * blk, :]
        zero = jnp.zeros_like(q)
        q_heads.append((jnp.where(lane < HEAD_DIM, q, zero), jnp.where(lane >= HEAD_DIM, q, zero)))

    def keys(j):
        start = pl.multiple_of(j * blk, blk)
        return k_ref[0, pl.ds(start, blk), :], v_ref[0, pl.ds(start, blk), :]

    def log_terms(z2, causal):
        sp = jnp.log(1.0 + jnp.exp2(-jnp.abs(z2))) * LOG2_E
        log_beta = jnp.minimum(z2, 0.0) - sp
        log_1m = log_beta - z2
        if causal is not None:
            log_1m = jnp.where(causal, log_1m, 0.0)
        suffix = _dot(log_1m.astype(BF16), tri)
        return log_beta, suffix, jnp.sum(log_1m, axis=1, keepdims=True)

    def visit(key_blocks, jobs):
        causal = lax.broadcasted_iota(jnp.int32, (blk, blk), 1) < lax.broadcasted_iota(jnp.int32, (blk, blk), 0)
        kv = [keys(j) for j in key_blocks]
        chains = [(n, b, hd) for n, job in enumerate(jobs) for b in range(len(job[1])) for hd in range(2)]

        def mask(n, b):
            return causal if (jobs[n][2] and b == 0) else None

        z2 = {c: _dot_nt(q_heads[jobs[c[0]][0]][c[2]], kv[jobs[c[0]][1][c[1]]][0]) * zscale for c in chains}
        terms = {c: log_terms(z2[c], mask(c[0], c[1])) for c in chains}
        results = []
        for n, (sub, block_ids, _, car_in) in enumerate(jobs):
            per_head = []
            for hd in range(2):
                car = None if car_in is None else car_in[hd]
                acc = None
                for b in range(len(block_ids)):
                    log_beta, suffix, tot = terms[(n, b, hd)]
                    w = jnp.exp2(log_beta + suffix if car is None else log_beta + suffix + car)
                    if mask(n, b) is not None:
                        w = jnp.where(causal, w, 0.0)
                    pv = _dot(w.astype(BF16), kv[block_ids[b]][1])
                    acc = pv if acc is None else acc + pv
                    car = tot if car is None else car + tot
                per_head.append((acc, car))
            results.append(per_head)
        return results

    def first_visit(first_step):
        first = step * nsub
        key_blocks = [first + sub for sub in range(nsub)] + ([] if first_step else [first - 1])
        jobs = []
        for sub in range(nsub):
            ids = [sub]
            if sub > 0:
                ids.append(sub - 1)
            elif not first_step:
                ids.append(nsub)
            jobs.append((sub, ids, True, None))
        for sub, per_head in enumerate(visit(key_blocks, jobs)):
            for hd, (acc, car) in enumerate(per_head):
                acc_ref[sub, hd] = acc
                car_ref[sub, hd] = car

    pl.when(step == 0)(lambda: first_visit(True))
    pl.when(step > 0)(lambda: first_visit(False))

    for sub in range(nsub):
        def mass_left(sub=sub):
            most = jnp.maximum(jnp.max(car_ref[sub, 0]), jnp.max(car_ref[sub, 1]))
            return (most > SB_SKIP_LOG2).astype(jnp.int32)

        def cond(c):
            j, go = c
            return jnp.logical_and(j >= 0, go > 0)

        def body(c, sub=sub, mass_left=mass_left):
            j, _ = c
            (per_head,) = visit([j], [(sub, [0], False, (car_ref[sub, 0], car_ref[sub, 1]))])
            for hd, (acc, car) in enumerate(per_head):
                acc_ref[sub, hd] += acc
                car_ref[sub, hd] = car
            return j - 1, mass_left()

        lax.while_loop(cond, body, (step * nsub + sub - 2, mass_left()))
        o_ref[0, sub * blk:(sub + 1) * blk, :] = jnp.where(lane < HEAD_DIM, acc_ref[sub, 0],
                                                           acc_ref[sub, 1]).astype(o_ref.dtype)


def _sb_attention(qkv, batch, seq, blk=SB_BLOCK, nsub=SB_SUBBLOCKS):
    qkv3 = qkv.reshape(batch, seq, 3 * BRANCH_WIDTH)
    pairs = BRANCH_WIDTH // LANES
    tq = blk * nsub
    r = np.arange(blk)
    m_ext = jnp.asarray(r[:, None] > r[None, :], BF16)
    out = pl.pallas_call(
        functools.partial(_sba_kernel, blk=blk, nsub=nsub),
        grid=(batch, pairs, seq // tq),
        in_specs=[pl.BlockSpec((1, tq, LANES), lambda b, p, i: (b, i, p)),
                  pl.BlockSpec((1, seq, LANES), lambda b, p, i: (b, 0, pairs + p)),
                  pl.BlockSpec((1, seq, LANES), lambda b, p, i: (b, 0, 2 * pairs + p)),
                  pl.BlockSpec(m_ext.shape, lambda b, p, i: (0, 0))],
        out_specs=pl.BlockSpec((1, tq, LANES), lambda b, p, i: (b, i, p)),
        out_shape=jax.ShapeDtypeStruct((batch, seq, BRANCH_WIDTH), BF16),
        scratch_shapes=[pltpu.VMEM((nsub, 2, blk, LANES), F32), pltpu.VMEM((nsub, 2, blk, 1), F32)],
        compiler_params=_cparams(("parallel", "parallel", "parallel")),
        name="sb_attention",
    )(qkv3, qkv3, qkv3, m_ext)
    return out.reshape(batch * seq, BRANCH_WIDTH)


def _s5_operators(lam_re, lam_im, log_dt, b_re, b_im, c_re, c_im, d_skip):
    L = S5_CHUNK
    g, p, c = S5_GROUPS, S5_STATE, S5_GROUP_CH
    dt = jnp.exp(log_dt.astype(F32))[:, None]
    lr = lam_re.astype(F32)
    li = lam_im.astype(F32)

    def a_pow(n):
        n = jnp.asarray(n, F32)[..., None, None]
        mag = jnp.exp(lr * dt * n)
        return mag * jnp.cos(li * dt * n), mag * jnp.sin(li * dt * n)

    ar, ai = a_pow(1.0)
    den = lr * lr + li * li
    nr = ar - 1.0
    fr = (nr * lr + ai * li) / den
    fi = (ai * lr - nr * li) / den
    bbr = fr[..., None] * b_re - fi[..., None] * b_im
    bbi = fr[..., None] * b_im + fi[..., None] * b_re
    cr = c_re.astype(F32)
    ci = c_im.astype(F32)

    pr, pi = a_pow(np.arange(L))
    car = cr[None] * pr[:, :, None, :] - ci[None] * pi[:, :, None, :]
    cai = cr[None] * pi[:, :, None, :] + ci[None] * pr[:, :, None, :]
    klag = jnp.einsum('jgop,gpi->jgoi', car, bbr) - jnp.einsum('jgop,gpi->jgoi', cai, bbi)
    klag = klag.at[0].add(d_skip.astype(F32).reshape(g, c)[:, :, None] * jnp.eye(c, dtype=F32))
    kflat = klag.transpose(1, 3, 0, 2).reshape(g, c, L * c)
    toe = jnp.stack([jnp.pad(kflat[:, :, :(L - s) * c], ((0, 0), (0, 0), (s * c, 0))) for s in range(L)], axis=1)
    toe = toe.reshape(g, L * c, L * c)

    qr, qi = a_pow(L - 1 - np.arange(L))
    p_re = qr[:, :, :, None] * bbr[None] - qi[:, :, :, None] * bbi[None]
    p_im = qr[:, :, :, None] * bbi[None] + qi[:, :, :, None] * bbr[None]
    p_re = p_re.transpose(1, 0, 3, 2).reshape(g, L * c, p)
    p_im = p_im.transpose(1, 0, 3, 2).reshape(g, L * c, p)

    er, ei = a_pow(1 + np.arange(L))
    q_re = (cr[None] * er[:, :, None, :] - ci[None] * ei[:, :, None, :])
    q_im = -(cr[None] * ei[:, :, None, :] + ci[None] * er[:, :, None, :])
    q_re = q_re.transpose(1, 3, 0, 2).reshape(g, p, L * c)
    q_im = q_im.transpose(1, 3, 0, 2).reshape(g, p, L * c)

    a_lr, a_li = a_pow(float(L))

    def pair_diag(m):
        gg, r, cc = m.shape
        m = m.reshape(gg // 2, 2, r, cc)
        z = jnp.zeros_like(m[:, 0])
        top = jnp.concatenate([m[:, 0], z], axis=2)
        bot = jnp.concatenate([z, m[:, 1]], axis=2)
        return jnp.concatenate([top, bot], axis=1)

    toe_p = pair_diag(toe).astype(BF16)
    p_p = jnp.concatenate([pair_diag(p_re), pair_diag(p_im)], axis=2).astype(BF16)
    q_p = jnp.concatenate([pair_diag(q_re), pair_diag(q_im)], axis=1).astype(BF16)
    a_re = a_lr.reshape(1, g * p)
    a_im = a_li.reshape(1, g * p)
    return toe_p, p_p, q_p, a_re, a_im


def _group_block_transpose(tiles):
    group = lax.broadcasted_iota(jnp.int32, tiles[0].shape, 1) // S5_GROUP_CH
    n = len(tiles)
    k = n // 2
    while k >= 1:
        upper = (group & k) != 0
        nxt = list(tiles)
        for i in range(n):
            if i & k == 0:
                lo, hi = tiles[i], tiles[i + k]
                nxt[i] = jnp.where(upper, pltpu.roll(hi, k * S5_GROUP_CH, 1), lo)
                nxt[i + k] = jnp.where(upper, hi, pltpu.roll(lo, LANES - k * S5_GROUP_CH, 1))
        tiles = nxt
        k //= 2
    return tiles


def _s5_state_in_kernel(u_ref, p_ref, up_ref, vre_ref, vim_ref, *, nb):
    L = S5_CHUNK
    per = LANES // S5_GROUP_CH
    for kb in range(u_ref.shape[0]):
        halves = []
        for a in range(L // per):
            steps = [u_ref[kb, pl.ds(a * per + b, nb, stride=L), :] for b in range(per)]
            halves.append(_group_block_transpose(steps))
        for pi in range(per // 2):
            pair = kb * (per // 2) + pi
            row = jnp.concatenate([halves[a][2 * pi + gi] for gi in range(2) for a in range(L // per)], axis=1)
            row = row.astype(BF16)
            up_ref[pair] = row
            v = _dot(row, p_ref[pair])
            vre_ref[:, pair * LANES:(pair + 1) * LANES] = v[:, :LANES]
            vim_ref[:, pair * LANES:(pair + 1) * LANES] = v[:, LANES:]


def _s5_out_kernel(up_ref, sre_ref, sim_ref, toe_ref, q_ref, y_ref, *, nb):
    L = S5_CHUNK
    per = LANES // S5_GROUP_CH
    for kb in range(y_ref.shape[0]):
        halves = [[None] * per for _ in range(L // per)]
        for pi in range(per // 2):
            pair = kb * (per // 2) + pi
            q = q_ref[pair]
            cols = slice(pair * LANES, (pair + 1) * LANES)
            y = _dot(up_ref[pair], toe_ref[pair])
            y += _dot(sre_ref[:, cols].astype(BF16), q[:LANES]) + _dot(sim_ref[:, cols].astype(BF16), q[LANES:])
            y = jax.nn.gelu(y)
            for gi in range(2):
                for a in range(L // per):
                    j = gi * (L // per) + a
                    halves[a][2 * pi + gi] = y[:, j * LANES:(j + 1) * LANES]
        for a in range(L // per):
            steps = _group_block_transpose(halves[a])
            for b in range(per):
                y_ref[kb, pl.ds(a * per + b, nb, stride=L), :] = steps[b]


def _s5_kernel(u_ref, p_ref, toe_ref, q_ref, are_ref, aim_ref, y_ref, up_ref, vre_ref, vim_ref, sre_ref, sim_ref,
               x_ref, *, nb, steps_per_seq):
    @pl.when(pl.program_id(0) % steps_per_seq == 0)
    def _():
        x_ref[...] = jnp.zeros_like(x_ref)

    _s5_state_in_kernel(u_ref, p_ref, up_ref, vre_ref, vim_ref, nb=nb)
    ar = are_ref[...]
    ai = aim_ref[...]

    def step(r, carry):
        xr, xi = carry
        sre_ref[pl.ds(r, 1), :] = xr
        sim_ref[pl.ds(r, 1), :] = xi
        vr = vre_ref[pl.ds(r, 1), :]
        vi = vim_ref[pl.ds(r, 1), :]
        return ar * xr - ai * xi + vr, ar * xi + ai * xr + vi

    xr, xi = lax.fori_loop(0, nb, step, (x_ref[0:1, :], x_ref[1:2, :]))
    x_ref[0:1, :] = xr
    x_ref[1:2, :] = xi
    _s5_out_kernel(up_ref, sre_ref, sim_ref, toe_ref, q_ref, y_ref, nb=nb)


def _s5_ssm(u, batch, seq, ops, layer, nb=128):
    toe_p, p_p, q_p, a_re, a_im = ops
    nslab, t, _ = u.shape
    L = S5_CHUNK
    npair = S5_GROUPS // 2
    nc = t // L
    row = 2 * L * S5_GROUP_CH
    ncols = S5_GROUPS * S5_STATE
    nb = min(nb, seq // L)

    slab_spec = pl.BlockSpec((nslab, nb * L, LANES), lambda i: (0, i, 0))
    return pl.pallas_call(
        functools.partial(_s5_kernel, nb=nb, steps_per_seq=seq // (L * nb)),
        grid=(nc // nb,),
        in_specs=[slab_spec] + [_layer_block(a, layer) for a in (p_p, toe_p, q_p, a_re, a_im)],
        out_specs=slab_spec,
        out_shape=jax.ShapeDtypeStruct((nslab, t, LANES), F32),
        scratch_shapes=[pltpu.VMEM((npair, nb, row), BF16)] + [pltpu.VMEM((nb, ncols), F32)] * 4
        + [pltpu.VMEM((8, ncols), F32)],
        compiler_params=_cparams(("arbitrary",)),
        name="s5_ssm",
    )(u, p_p, toe_p, q_p, a_re, a_im)


def _ret_tables(seq):
    half = HEAD_DIM // 2
    pos = jnp.arange(seq, dtype=F32)
    inv = ROPE_BASE ** (-jnp.arange(half, dtype=F32) / half)
    ang = pos[:, None] * inv[None, :]
    cos = jnp.tile(jnp.cos(ang), (1, LANES // half))
    sin = jnp.sin(ang)
    sin = jnp.tile(jnp.concatenate([-sin, sin], axis=1), (1, LANES // HEAD_DIM))
    c = RET_CHUNK
    gamma = 1.0 - 2.0 ** (-5.0 - jnp.arange(RET_HEADS, dtype=F32))
    log_g = jnp.log(gamma)
    idx = jnp.arange(c, dtype=F32)
    diff = idx[:, None] - idx[None, :]
    intra = jnp.where(diff >= 0, jnp.exp(jnp.maximum(diff, 0.0)[None] * log_g[:, None, None]), 0.0)
    q_dec = jnp.exp((idx + 1.0)[None, :] * log_g[:, None])
    k_dec = jnp.exp((c - 1.0 - idx)[None, :] * log_g[:, None])
    ch_dec = jnp.exp(c * log_g)

    def lanes(x):
        x = x.reshape(RET_HEADS // 2, 2, c)
        return jnp.repeat(x.transpose(0, 2, 1), HEAD_DIM, axis=2)

    q_dec_l = lanes(q_dec)
    k_dec_l = lanes(k_dec)
    hp = RET_HEADS // 2
    blockdiag = np.kron(np.eye(2), np.ones((HEAD_DIM, HEAD_DIM)))
    st_dec = jnp.repeat(ch_dec.reshape(hp, 2), HEAD_DIM, axis=1)[:, :, None] * blockdiag[None]
    avg = jnp.asarray(blockdiag / HEAD_DIM, BF16)
    return cos, sin, intra, q_dec_l, k_dec_l, st_dec.astype(F32), avg


def _ret_kernel(q_ref, k_ref, v_ref, g_ref, cos_ref, sin_ref, intra_ref, qd_ref, kd_ref, sd_ref, avg_ref, nw_ref,
                o_ref, st_ref, *, nchunk):
    c = RET_CHUNK

    @pl.when(pl.program_id(2) == 0)
    def _():
        st_ref[...] = jnp.zeros_like(st_ref)

    tq = q_ref.shape[1]
    lane = lax.broadcasted_iota(jnp.int32, (tq, LANES), 1)
    first_half = (lane % HEAD_DIM) < (HEAD_DIM // 2)
    head0 = lax.broadcasted_iota(jnp.int32, (c, LANES), 1) < HEAD_DIM
    sd = sd_ref[0]
    avg = avg_ref[...]
    same_head = avg > 0
    cos = cos_ref[...]
    sin = sin_ref[...]

    def rotary(x):
        swapped = jnp.where(first_half, pltpu.roll(x, LANES - HEAD_DIM // 2, 1), pltpu.roll(x, HEAD_DIM // 2, 1))
        return x * cos + swapped * sin

    q_all = rotary(q_ref[0].astype(F32)).astype(BF16)
    k_all = rotary(k_ref[0].astype(F32)) * (HEAD_DIM ** -0.5)
    kb_all = k_all.astype(BF16)
    zero = jnp.zeros((c, LANES), BF16)
    chunks = [slice(ci * c, (ci + 1) * c) for ci in range(nchunk)]
    vs = [v_ref[0, rows, :] for rows in chunks]
    scores, incs = [], []
    for ci, rows in enumerate(chunks):
        qb = q_all[rows]
        q2 = jnp.concatenate([jnp.where(head0, qb, zero), jnp.where(head0, zero, qb)], axis=0)
        scores.append(_dot_nt(q2, kb_all[rows]))
        kd_t = (k_all[rows] * kd_ref[0]).T.astype(BF16)
        incs.append(_dot(kd_t, vs[ci]))
    state = st_ref[...]
    states = []
    for inc in incs:
        states.append(state.astype(BF16))
        state = state * sd + jnp.where(same_head, inc, 0.0)
    st_ref[...] = state
    intra2 = jnp.concatenate([intra_ref[0, 0], intra_ref[0, 1]], axis=0)
    outs = []
    for ci, rows in enumerate(chunks):
        o2 = _dot((scores[ci] * intra2).astype(BF16), vs[ci])
        outs.append(jnp.where(head0, o2[:c], o2[c:]) + _dot(q_all[rows], states[ci]) * qd_ref[0])
    out = jnp.concatenate(outs, axis=0)
    hi, lo = _split_bf16(out)
    mu = _dot(hi, avg) + _dot(lo, avg)
    dlt = out - mu
    hi, lo = _split_bf16(dlt * dlt)
    var = _dot(hi, avg) + _dot(lo, avg)
    gate = g_ref[0].astype(F32)
    y = dlt * lax.rsqrt(var + EPS) * nw_ref[...] * (gate * jax.nn.sigmoid(gate))
    o_ref[0] = y.astype(o_ref.dtype)


def _retention(ret, norm_w, batch, seq, tables, tq=2048):
    cos, sin, intra, q_dec_l, k_dec_l, st_dec, avg = tables
    pairs = BRANCH_WIDTH // LANES
    ret3 = ret.reshape(batch, seq, 4 * BRANCH_WIDTH)
    tq = min(tq, seq)
    c = RET_CHUNK

    def col(k):
        return pl.BlockSpec((1, tq, LANES), lambda b, p, i: (b, i, k * pairs + p))

    out = pl.pallas_call(
        functools.partial(_ret_kernel, nchunk=tq // c),
        grid=(batch, pairs, seq // tq),
        in_specs=[col(0), col(1), col(2), col(3),
                  pl.BlockSpec((tq, LANES), lambda b, p, i: (i, 0)),
                  pl.BlockSpec((tq, LANES), lambda b, p, i: (i, 0)),
                  pl.BlockSpec((1, 2, c, c), lambda b, p, i: (p, 0, 0, 0)),
                  pl.BlockSpec((1, c, LANES), lambda b, p, i: (p, 0, 0)),
                  pl.BlockSpec((1, c, LANES), lambda b, p, i: (p, 0, 0)),
                  pl.BlockSpec((1, LANES, LANES), lambda b, p, i: (p, 0, 0)),
                  pl.BlockSpec((LANES, LANES), lambda b, p, i: (0, 0)),
                  pl.BlockSpec((1, LANES), lambda b, p, i: (0, p))],
        out_specs=pl.BlockSpec((1, tq, LANES), lambda b, p, i: (b, i, p)),
        out_shape=jax.ShapeDtypeStruct((batch, seq, BRANCH_WIDTH), BF16),
        scratch_shapes=[pltpu.VMEM((LANES, LANES), F32)],
        compiler_params=_cparams(("parallel", "parallel", "arbitrary")),
        name="retention",
    )(ret3, ret3, ret3, ret3, cos, sin, intra.reshape(pairs, 2, c, c), q_dec_l, k_dec_l, st_dec, avg,
      norm_w.reshape(1, BRANCH_WIDTH).astype(F32))
    return out.reshape(batch * seq, BRANCH_WIDTH)


def _route(logits):
    lane = lax.broadcasted_iota(jnp.int32, logits.shape, 1).astype(F32)
    neg = jnp.float32(-jnp.inf)
    big = jnp.float32(1 << 20)
    gl = jnp.where(lane < N_GROUPS, logits, neg)
    gmax = jnp.max(gl, axis=1, keepdims=True)
    gsum = jnp.sum(jnp.exp(gl - gmax), axis=1, keepdims=True)
    g_w = 1.0 / gsum
    g_idx = jnp.min(jnp.where(gl == gmax, lane, big), axis=1, keepdims=True)
    lo = ROUTER_LANE0 + EXPERTS_PER_GROUP * g_idx
    sl = jnp.where((lane >= lo) & (lane < lo + EXPERTS_PER_GROUP), logits, neg)
    m1 = jnp.max(sl, axis=1, keepdims=True)
    i1 = jnp.min(jnp.where(sl == m1, lane, big), axis=1, keepdims=True)
    sl2 = jnp.where(lane == i1, neg, sl)
    m2 = jnp.max(sl2, axis=1, keepdims=True)
    i2 = jnp.min(jnp.where(sl2 == m2, lane, big), axis=1, keepdims=True)
    e2 = jnp.exp(m2 - m1)
    w1 = g_w / (1.0 + e2)
    w2 = g_w * e2 / (1.0 + e2)
    packed = jnp.where(lane == 0.0, i1 - ROUTER_LANE0, jnp.where(lane == 1.0, i2 - ROUTER_LANE0, 0.0))
    return packed + jnp.where(lane == 2.0, w1, 0.0) + jnp.where(lane == 3.0, w2, 0.0)


def _merge_kernel(ysb_ref, ys5_ref, yret_ref, gate_ref, h_ref, wglu_ref, wbr_ref, wout_ref, n2_ref, wr_hi_ref,
                  wr_lo_ref, br_ref, h_out_ref, hn_ref, route_ref):
    ys5 = jnp.concatenate([ys5_ref[s] for s in range(ys5_ref.shape[0])], axis=1).astype(BF16)
    glu = _dot(ys5, wglu_ref[...])
    y_s5 = (glu[:, :BRANCH_WIDTH] * jax.nn.sigmoid(glu[:, BRANCH_WIDTH:])).astype(BF16)
    merged = None
    for n, y in enumerate((ysb_ref[...], y_s5, yret_ref[...])):
        g = jax.nn.sigmoid(gate_ref[:, n * D_MODEL:(n + 1) * D_MODEL].astype(F32))
        term = g * _dot(y, wbr_ref[n])
        merged = term if merged is None else merged + term
    h = h_ref[...] + _dot(merged.astype(BF16), wout_ref[...])
    h_out_ref[...] = h
    ms = jnp.mean(h * h, axis=-1, keepdims=True)
    hn = h * lax.rsqrt(ms + EPS) * n2_ref[...]
    hn_ref[...] = hn.astype(BF16)
    hi, lo = _split_bf16(hn)
    logits = _dot(hi, wr_hi_ref[...]) + _dot(hi, wr_lo_ref[...]) + _dot(lo, wr_hi_ref[...]) + br_ref[...]
    route_ref[...] = _route(logits)


def _merge(y_sb, y_s5, y_ret, gates, h, w_glu, w_br, w_out, norm2_w, wr_hi, wr_lo, b_r, layer, tm=512):
    t = h.shape[0]

    def rows(w):
        return pl.BlockSpec((tm, w), lambda i: (i, 0))

    return pl.pallas_call(
        _merge_kernel,
        grid=(t // tm,),
        in_specs=[rows(BRANCH_WIDTH), pl.BlockSpec((y_s5.shape[0], tm, LANES), lambda i: (0, i, 0)),
                  rows(BRANCH_WIDTH), rows(N_BRANCH * D_MODEL),
                  rows(D_MODEL)] + [_layer_block(a, layer) for a in (w_glu, w_br, w_out, norm2_w, wr_hi, wr_lo, b_r)],
        out_specs=[rows(D_MODEL), rows(D_MODEL), rows(LANES)],
        out_shape=[jax.ShapeDtypeStruct((t, D_MODEL), F32), jax.ShapeDtypeStruct((t, D_MODEL), BF16),
                   jax.ShapeDtypeStruct((t, LANES), F32)],
        compiler_params=_cparams(("parallel",)),
        name="merge_route",
    )(y_sb, y_s5, y_ret, gates, h, w_glu, w_br, w_out, norm2_w, wr_hi, wr_lo, b_r)


MOE_TILE = 512
MOE_ROW_TILE = 512
SEG_ALIGN = 16
MOE_SLOTS = TOP_K * MOE_TILE + N_EXPERTS * SEG_ALIGN
MOE_CHUNKS = MOE_SLOTS // SEG_ALIGN
MOE_SLOT_BLOCK = 256


def _moe_plan_kernel(route_ref, tri_ref, cum_ref, pos_ref, cnt_ref):
    r = route_ref[...]
    tm = r.shape[0]
    lane = lax.broadcasted_iota(jnp.int32, r.shape, 1).astype(F32)
    e1 = r[:, 0:1]
    e2 = r[:, 1:2]
    chosen = (lane == e1) | (lane == e2)
    onehot = jnp.where(chosen, 1.0, 0.0).astype(BF16)
    incl = _dot(tri_ref[...], onehot)
    cnt = incl[tm - 1:tm, :]
    padded = jnp.floor((cnt + (SEG_ALIGN - 1)) * (1.0 / SEG_ALIGN)) * SEG_ALIGN
    start = _dot(jnp.broadcast_to(padded, (8, LANES)).astype(BF16), cum_ref[...])[0:1, :]
    slot = start + incl - 1.0
    p1 = jnp.sum(jnp.where(lane == e1, slot, 0.0), axis=1, keepdims=True)
    p2 = jnp.sum(jnp.where(lane == e2, slot, 0.0), axis=1, keepdims=True)
    pos_ref[...] = jnp.where(lane == 0.0, p1, jnp.where(lane == 1.0, p2, 0.0))
    cnt_ref[0] = jnp.broadcast_to(cnt, (8, LANES))


def _moe_plan(route, tm):
    t = route.shape[0]
    r = np.arange(tm)
    tri = jnp.asarray(r[None, :] <= r[:, None], BF16)
    l = np.arange(LANES)
    cum = jnp.asarray(l[:, None] < l[None, :], BF16)
    return pl.pallas_call(
        _moe_plan_kernel,
        grid=(t // tm,),
        in_specs=[pl.BlockSpec((tm, LANES), lambda i: (i, 0)),
                  pl.BlockSpec((tm, tm), lambda i: (0, 0)),
                  pl.BlockSpec((LANES, LANES), lambda i: (0, 0))],
        out_specs=[pl.BlockSpec((tm, LANES), lambda i: (i, 0)), pl.BlockSpec((1, 8, LANES), lambda i: (i, 0, 0))],
        out_shape=[jax.ShapeDtypeStruct((t, LANES), F32), jax.ShapeDtypeStruct((t // tm, 8, LANES), F32)],
        compiler_params=_cparams(("parallel",)),
        name="moe_plan",
    )(route, tri, cum)


def _moe_tables(cnt, rt, n_rows):
    a = SEG_ALIGN
    ntile = cnt.shape[0]
    p = (cnt + a - 1) // a * a
    lend = jnp.cumsum(p, axis=1)
    lstart = lend - p
    seg = p.sum(axis=0)
    segpad = (seg + rt - 1) // rt * rt
    gend = jnp.cumsum(segpad)
    gstart = gend - segpad
    toff = gstart[None, :] + jnp.cumsum(p, axis=0) - p
    cstart = a * jnp.arange(MOE_CHUNKS, dtype=jnp.int32)
    in_run = (cstart[None, :, None] >= lstart[:, None, :]) & (cstart[None, :, None] < lend[:, None, :])
    gdst = (jnp.sum(jnp.where(in_run, (toff - lstart)[:, None, :], 0), axis=2) + cstart[None, :]) // a
    nch = lend[:, -1] // a
    tstart = rt * jnp.arange(n_rows // rt, dtype=jnp.int32)
    tile_e = jnp.minimum(jnp.sum(tstart[:, None] >= gend[None, :], axis=1), N_EXPERTS - 1)
    n_used = gend[-1] // rt
    zmax = rt // a - 1
    zper = -(-N_EXPERTS * zmax // ntile)
    k = jnp.arange(zmax, dtype=jnp.int32)
    zd = ((gstart + seg) // a)[:, None] + k[None, :]
    valid = (k[None, :] < ((segpad - seg) // a)[:, None]).reshape(-1)
    order = jnp.argsort(jnp.logical_not(valid), stable=True)
    zdst = jnp.pad(zd.reshape(-1)[order], (0, ntile * zper - N_EXPERTS * zmax))
    nz = valid.sum()
    i32 = jnp.int32
    return (gdst.reshape(-1).astype(i32), nch.astype(i32), zdst.astype(i32), nz.reshape(1).astype(i32),
            tile_e.astype(i32), n_used.reshape(1).astype(i32)), zper


def _moe_dispatch_kernel(gdst_ref, nch_ref, zdst_ref, nz_ref, n_used_ref, pos_ref, hn_ref, xs_hbm, stg, zblk, sem,
                         zsem, *, zper):
    i = pl.program_id(0)
    last_step = pl.num_programs(0) - 1
    buf = i % 2
    tm = hn_ref.shape[0]
    pos_t = pos_ref[...].T
    p1 = pos_t[0:1, :]
    p2 = pos_t[1:2, :]
    x = hn_ref[...]
    for lo in range(0, MOE_SLOTS, tm):
        slot = (lax.broadcasted_iota(jnp.int32, (tm, tm), 0) + lo).astype(F32)
        onehot = jnp.where((slot == p1) | (slot == p2), 1.0, 0.0).astype(BF16)
        stg[buf, lo:lo + tm, :] = _dot(onehot, x).astype(BF16)

    def chunk_copy(which, src_chunk, dst_chunk):
        src = pl.multiple_of(src_chunk * SEG_ALIGN, SEG_ALIGN)
        dst = pl.multiple_of(dst_chunk * SEG_ALIGN, SEG_ALIGN)
        return pltpu.make_async_copy(stg.at[which, pl.ds(src, SEG_ALIGN)], xs_hbm.at[pl.ds(dst, SEG_ALIGN)],
                                     sem.at[which])

    def n_copies(step):
        return nch_ref[step], jnp.clip(nz_ref[0] - step * zper, 0, zper)

    def drain(which, step):
        for count in n_copies(step):
            rows = count * SEG_ALIGN

            @pl.when(rows > 0)
            def _():
                pltpu.make_async_copy(stg.at[which, pl.ds(0, rows)], xs_hbm.at[pl.ds(0, rows)],
                                      sem.at[which]).wait()

    n, nzero = n_copies(i)

    @pl.loop(0, n)
    def _(c):
        chunk_copy(buf, c, gdst_ref[i * MOE_CHUNKS + c]).start()

    @pl.loop(0, nzero)
    def _(k):
        chunk_copy(buf, MOE_CHUNKS - 1, zdst_ref[i * zper + k]).start()

    pl.when(i > 0)(lambda: drain(1 - buf, i - 1))

    @pl.when(i == last_step)
    def _():
        drain(buf, i)
        rt = zblk.shape[0]
        zblk[...] = jnp.zeros_like(zblk)
        first = n_used_ref[0]
        last = xs_hbm.shape[0] // rt

        def tile_copy(j):
            return pltpu.make_async_copy(zblk, xs_hbm.at[pl.ds(pl.multiple_of(j * rt, rt), rt)], zsem)

        @pl.loop(first, last)
        def _(j):
            tile_copy(j).start()

        @pl.loop(first, last)
        def _(j):
            tile_copy(0).wait()


def _moe_expert_kernel(tile_e_ref, n_used_ref, xs_ref, wg_ref, wu_ref, wd_ref, ys_ref):
    used = pl.program_id(0) < n_used_ref[0]

    @pl.when(used)
    def _():
        x = xs_ref[...]
        gate = _dot(x, wg_ref[0].astype(BF16))
        up = _dot(x, wu_ref[0].astype(BF16))
        hdn = gate * jax.nn.sigmoid(gate) * up
        ys_ref[...] = _dot(hdn.astype(BF16), wd_ref[0].astype(BF16)).astype(ys_ref.dtype)

    @pl.when(jnp.logical_not(used))
    def _():
        ys_ref[...] = jnp.zeros_like(ys_ref)


def _moe_combine_kernel(gdst_ref, nch_ref, route_ref, pos_ref, h_ref, nw_ref, ys_hbm, o_ref, stg, sem, *, out_norm):
    i = pl.program_id(0)
    nsteps = pl.num_programs(0)
    buf = i % 2
    tm = h_ref.shape[0]

    def chunk_copy(which, src_chunk, dst_chunk):
        src = pl.multiple_of(src_chunk * SEG_ALIGN, SEG_ALIGN)
        dst = pl.multiple_of(dst_chunk * SEG_ALIGN, SEG_ALIGN)
        return pltpu.make_async_copy(ys_hbm.at[pl.ds(src, SEG_ALIGN)], stg.at[which, pl.ds(dst, SEG_ALIGN)],
                                     sem.at[which])

    def fetch(step, which):
        n = nch_ref[step]

        @pl.loop(0, n)
        def _(c):
            chunk_copy(which, gdst_ref[step * MOE_CHUNKS + c], c).start()

        per = MOE_SLOT_BLOCK // SEG_ALIGN
        stop = jnp.minimum((n + per - 1) // per * per, MOE_CHUNKS)

        @pl.loop(n, stop)
        def _(c):
            rows = pl.ds(pl.multiple_of(c * SEG_ALIGN, SEG_ALIGN), SEG_ALIGN)
            stg[which, rows, :] = jnp.zeros((SEG_ALIGN, D_MODEL), BF16)

    pl.when(i == 0)(lambda: fetch(0, 0))
    pl.when(i + 1 < nsteps)(lambda: fetch(i + 1, 1 - buf))

    r = route_ref[...]
    pos = pos_ref[...]

    def scatter(lo, hi):
        slot = (lax.broadcasted_iota(jnp.int32, (tm, hi - lo), 1) + lo).astype(F32)
        w = jnp.where(slot == pos[:, 0:1], r[:, 2:3], 0.0) + jnp.where(slot == pos[:, 1:2], r[:, 3:4], 0.0)
        return _dot(w.astype(BF16), stg[buf, lo:hi, :])

    rows = nch_ref[i] * SEG_ALIGN

    @pl.when(rows > 0)
    def _():
        pltpu.make_async_copy(ys_hbm.at[pl.ds(0, rows)], stg.at[buf, pl.ds(0, rows)], sem.at[buf]).wait()

    always = TOP_K * tm
    o_ref[...] = h_ref[...] + scatter(0, always)
    for lo in range(always, MOE_SLOTS, MOE_SLOT_BLOCK):
        @pl.when(lo < rows)
        def _(lo=lo):
            o_ref[...] += scatter(lo, lo + MOE_SLOT_BLOCK)
    if out_norm:
        out = o_ref[...]
        ms = jnp.mean(out * out, axis=-1, keepdims=True)
        o_ref[...] = out * lax.rsqrt(ms + EPS) * nw_ref[...]


def _moe(hn, route, h, w_gate, w_up, w_down, layer, out_norm_w, out_norm):
    t = h.shape[0]
    tm = min(MOE_TILE, t)
    rt = MOE_ROW_TILE
    ntile = t // tm
    n_rows = TOP_K * t + ntile * N_EXPERTS * (SEG_ALIGN - 1) + N_EXPERTS * (rt - SEG_ALIGN)
    n_rows = (n_rows + rt - 1) // rt * rt
    pos, cnt = _moe_plan(route, tm)
    (gdst, nch, zdst, nz, tile_e, n_used), zper = _moe_tables(cnt[:, 0, :N_EXPERTS].astype(jnp.int32), rt, n_rows)

    def expert_block(rows, cols):
        return pl.BlockSpec((None, 1, rows, cols), lambda j, te, nu: (layer, te[j], 0, 0))

    xs = pl.pallas_call(
        functools.partial(_moe_dispatch_kernel, zper=zper),
        grid_spec=pltpu.PrefetchScalarGridSpec(
            num_scalar_prefetch=5,
            grid=(ntile,),
            in_specs=[pl.BlockSpec((tm, LANES), lambda i, *_: (i, 0)),
                      pl.BlockSpec((tm, D_MODEL), lambda i, *_: (i, 0))],
            out_specs=pl.BlockSpec(memory_space=pl.ANY),
            scratch_shapes=[pltpu.VMEM((2, MOE_SLOTS, D_MODEL), BF16), pltpu.VMEM((rt, D_MODEL), BF16),
                            pltpu.SemaphoreType.DMA((2,)), pltpu.SemaphoreType.DMA(())]),
        out_shape=jax.ShapeDtypeStruct((n_rows, D_MODEL), BF16),
        compiler_params=_cparams(("arbitrary",)),
        name="moe_dispatch",
    )(gdst, nch, zdst, nz, n_used, pos, hn)
    ys = pl.pallas_call(
        _moe_expert_kernel,
        grid_spec=pltpu.PrefetchScalarGridSpec(
            num_scalar_prefetch=2,
            grid=(n_rows // rt,),
            in_specs=[pl.BlockSpec((rt, D_MODEL), lambda j, te, nu: (jnp.minimum(j, nu[0] - 1), 0)),
                      expert_block(D_MODEL, EXPERT_FF), expert_block(D_MODEL, EXPERT_FF),
                      expert_block(EXPERT_FF, D_MODEL)],
            out_specs=pl.BlockSpec((rt, D_MODEL), lambda j, te, nu: (j, 0))),
        out_shape=jax.ShapeDtypeStruct((n_rows, D_MODEL), BF16),
        compiler_params=_cparams(("arbitrary",)),
        name="moe_experts",
    )(tile_e, n_used, xs, w_gate, w_up, w_down)
    return pl.pallas_call(
        functools.partial(_moe_combine_kernel, out_norm=out_norm),
        grid_spec=pltpu.PrefetchScalarGridSpec(
            num_scalar_prefetch=2,
            grid=(ntile,),
            in_specs=[pl.BlockSpec((tm, LANES), lambda i, *_: (i, 0)),
                      pl.BlockSpec((tm, LANES), lambda i, *_: (i, 0)),
                      pl.BlockSpec((tm, D_MODEL), lambda i, *_: (i, 0)),
                      pl.BlockSpec((1, D_MODEL), lambda i, *_: (0, 0)),
                      pl.BlockSpec(memory_space=pl.ANY)],
            out_specs=pl.BlockSpec((tm, D_MODEL), lambda i, *_: (i, 0)),
            scratch_shapes=[pltpu.VMEM((2, MOE_SLOTS, D_MODEL), BF16), pltpu.SemaphoreType.DMA((2,))]),
        out_shape=jax.ShapeDtypeStruct((t, D_MODEL), F32),
        compiler_params=_cparams(("arbitrary",)),
        name="moe_combine",
    )(gdst, nch, route, pos, h, out_norm_w, ys)


def kernel(x, norm1_w, w_in, s5_lambda_re, s5_lambda_im, s5_log_dt, s5_b_re, s5_b_im, s5_c_re, s5_c_im, s5_d,
           s5_w_glu, ret_norm_w, w_branch, w_out, norm2_w, w_group, b_group, w_router, b_router, w_gate, w_up,
           w_down, final_norm_w):
    batch, seq, _ = x.shape
    depth = w_in.shape[0]
    t = batch * seq
    h = x.reshape(t, D_MODEL).astype(F32)
    ret_tables = _ret_tables(seq)
    w_in, s5_w_glu, w_branch, w_out = (w.astype(BF16) for w in (w_in, s5_w_glu, w_branch, w_out))
    norm1_w = norm1_w.reshape(depth, 1, D_MODEL).astype(F32)
    norm2_w = norm2_w.reshape(depth, 1, D_MODEL).astype(F32)
    s5_ops = jax.vmap(_s5_operators)(s5_lambda_re, s5_lambda_im, s5_log_dt, s5_b_re, s5_b_im, s5_c_re, s5_c_im, s5_d)
    w_r = jnp.concatenate([w_group, w_router.transpose(0, 2, 1, 3).reshape(depth, D_MODEL, N_EXPERTS)], axis=2)
    w_r = jnp.pad(w_r.astype(F32), ((0, 0), (0, 0), (0, LANES - w_r.shape[2])))
    wr_hi, wr_lo = _split_bf16(w_r)
    b_r = jnp.pad(jnp.concatenate([b_group, b_router.reshape(depth, -1)], axis=1).astype(F32),
                  ((0, 0), (0, LANES - N_GROUPS - N_EXPERTS))).reshape(depth, 1, LANES)
    for layer in range(depth):
        qkv, u_s5, ret, gates = _inproj(h, norm1_w, w_in, layer)
        y_sb = _sb_attention(qkv, batch, seq)
        y_s5 = _s5_ssm(u_s5, batch, seq, s5_ops, layer)
        y_ret = _retention(ret, ret_norm_w[layer], batch, seq, ret_tables)
        h, hn, route = _merge(y_sb, y_s5, y_ret, gates, h, s5_w_glu, w_branch, w_out, norm2_w, wr_hi, wr_lo, b_r,
                              layer)
        h = _moe(hn, route, h, w_gate, w_up, w_down, layer, final_norm_w.reshape(1, D_MODEL).astype(F32),
                 out_norm=(layer == depth - 1))
    return h.reshape(batch, seq, D_MODEL).astype(x.dtype)
```

```python
import functools
import math

import jax
import jax.numpy as jnp
import numpy as np
from jax import lax
from jax.experimental import pallas as pl
from jax.experimental.pallas import tpu as pltpu

F32 = jnp.float32
BF16 = jnp.bfloat16

D_MODEL = 1024
SB_HEADS = 8
HEAD_DIM = 64
BRANCH_WIDTH = 512
S5_GROUPS = 32
S5_GROUP_CH = 16
S5_STATE = 64
RET_HEADS = 8
RET_CHUNK = 128
ROPE_BASE = 10000.0
N_BRANCH = 3
N_GROUPS = 4
EXPERTS_PER_GROUP = 8
N_EXPERTS = 32
TOP_K = 2
EXPERT_FF = 256
EPS = 1e-6

LANES = 128
VMEM_LIMIT = 56 * 1024 * 1024
S5_CHUNK = 16
SB_BLOCK = 256
SB_SUBBLOCKS = 4
LOG2_E = 1.4426950408889634
SB_SKIP_LOG2 = -152.0
ROUTER_LANE0 = N_GROUPS


def _cparams(sem):
    return pltpu.CompilerParams(dimension_semantics=sem, vmem_limit_bytes=VMEM_LIMIT)


def _layer_block(stacked, layer):
    zeros = (0,) * (stacked.ndim - 1)
    return pl.BlockSpec((None,) + stacked.shape[1:], lambda *_: (layer,) + zeros, pipeline_mode=pl.Buffered(1))


def _split_bf16(x):
    hi = x.astype(BF16)
    lo = (x - hi.astype(F32)).astype(BF16)
    return hi, lo


def _dot(a, b):
    return jnp.dot(a, b, preferred_element_type=F32)


def _dot_nt(a, b):
    return lax.dot_general(a, b, (((1,), (1,)), ((), ())), preferred_element_type=F32)


def _dot_tn(a, b):
    return lax.dot_general(a, b, (((0,), (0,)), ((), ())), preferred_element_type=F32)


def _inproj_kernel(h_ref, nw_ref, w_ref, qkv_ref, u_ref, ret_ref, gate_ref):
    x = h_ref[...]
    ms = jnp.mean(x * x, axis=-1, keepdims=True)
    xn = (x * lax.rsqrt(ms + EPS) * nw_ref[...]).astype(BF16)
    off = 0
    for ref in (qkv_ref, u_ref, ret_ref, gate_ref):
        if ref is u_ref:
            u = _dot(xn, w_ref[:, off:off + BRANCH_WIDTH])
            for s in range(u_ref.shape[0]):
                u_ref[s] = u[:, s * LANES:(s + 1) * LANES]
            off += BRANCH_WIDTH
            continue
        width = ref.shape[1]
        for c in range(0, width, 512):
            ref[:, c:c + 512] = _dot(xn, w_ref[:, off + c:off + c + 512]).astype(ref.dtype)
        off += width


def _inproj(h, norm_w, w_in, layer, tm=512):
    t = h.shape[0]
    nslab = BRANCH_WIDTH // LANES

    def rows(w):
        return pl.BlockSpec((tm, w), lambda i: (i, 0))

    return pl.pallas_call(
        _inproj_kernel,
        grid=(t // tm,),
        in_specs=[rows(D_MODEL),
                  _layer_block(norm_w, layer), _layer_block(w_in, layer)],
        out_specs=[rows(3 * BRANCH_WIDTH), pl.BlockSpec((nslab, tm, LANES), lambda i: (0, i, 0)),
                   rows(4 * BRANCH_WIDTH), rows(N_BRANCH * D_MODEL)],
        out_shape=[jax.ShapeDtypeStruct((t, 3 * BRANCH_WIDTH), BF16),
                   jax.ShapeDtypeStruct((nslab, t, LANES), F32),
                   jax.ShapeDtypeStruct((t, 4 * BRANCH_WIDTH), BF16),
                   jax.ShapeDtypeStruct((t, N_BRANCH * D_MODEL), BF16)],
        compiler_params=_cparams(("parallel",)),
        name="inproj",
    )(h, norm_w, w_in)


def _sba_kernel(q_ref, k_ref, v_ref, m_ref, o_ref, acc_ref, car_ref, *, blk, nsub):
    step = pl.program_id(2)
    lane = lax.broadcasted_iota(jnp.int32, (blk, LANES), 1)
    zscale = HEAD_DIM ** -0.5 * LOG2_E
    tri = m_ref[...]
    q_heads = []
    for sub in range(nsub):
        q = q_ref[0, sub * blk:(sub + 1) * blk, :]
        zero = jnp.zeros_like(q)
        q_heads.append((jnp.where(lane < HEAD_DIM, q, zero), jnp.where(lane >= HEAD_DIM, q, zero)))

    def keys(j):
        start = pl.multiple_of(j * blk, blk)
        return k_ref[0, pl.ds(start, blk), :], v_ref[0, pl.ds(start, blk), :]

    def log_terms(z2, causal):
        sp = jnp.log(1.0 + jnp.exp2(-jnp.abs(z2))) * LOG2_E
        log_beta = jnp.minimum(z2, 0.0) - sp
        log_1m = log_beta - z2
        if causal is not None:
            log_1m = jnp.where(causal, log_1m, 0.0)
        suffix = _dot(log_1m.astype(BF16), tri)
        return log_beta, suffix, jnp.sum(log_1m, axis=1, keepdims=True)

    def visit(key_blocks, jobs):
        causal = lax.broadcasted_iota(jnp.int32, (blk, blk), 1) < lax.broadcasted_iota(jnp.int32, (blk, blk), 0)
        kv = [keys(j) for j in key_blocks]
        chains = [(n, b, hd) for n, job in enumerate(jobs) for b in range(len(job[1])) for hd in range(2)]

        def mask(n, b):
            return causal if (jobs[n][2] and b == 0) else None

        z2 = {c: _dot_nt(q_heads[jobs[c[0]][0]][c[2]], kv[jobs[c[0]][1][c[1]]][0]) * zscale for c in chains}
        terms = {c: log_terms(z2[c], mask(c[0], c[1])) for c in chains}
        results = []
        for n, (sub, block_ids, _, car_in) in enumerate(jobs):
            per_head = []
            for hd in range(2):
                car = None if car_in is None else car_in[hd]
                acc = None
                for b in range(len(block_ids)):
                    log_beta, suffix, tot = terms[(n, b, hd)]
                    w = jnp.exp2(log_beta + suffix if car is None else log_beta + suffix + car)
                    if mask(n, b) is not None:
                        w = jnp.where(causal, w, 0.0)
                    pv = _dot(w.astype(BF16), kv[block_ids[b]][1])
                    acc = pv if acc is None else acc + pv
                    car = tot if car is None else car + tot
                per_head.append((acc, car))
            results.append(per_head)
        return results

    def first_visit(first_step):
        first = step * nsub
        key_blocks = [first + sub for sub in range(nsub)] + ([] if first_step else [first - 1])
        jobs = []
        for sub in range(nsub):
            ids = [sub]
            if sub > 0:
                ids.append(sub - 1)
            elif not first_step:
                ids.append(nsub)
            jobs.append((sub, ids, True, None))
        for sub, per_head in enumerate(visit(key_blocks, jobs)):
            for hd, (acc, car) in enumerate(per_head):
                acc_ref[sub, hd] = acc
                car_ref[sub, hd] = car

    pl.when(step == 0)(lambda: first_visit(True))
    pl.when(step > 0)(lambda: first_visit(False))

    def remaining_blocks():
        for sub in range(nsub):
            def mass_left(sub=sub):
                most = jnp.maximum(jnp.max(car_ref[sub, 0]), jnp.max(car_ref[sub, 1]))
                return (most > SB_SKIP_LOG2).astype(jnp.int32)

            def cond(c):
                j, go = c
                return jnp.logical_and(j >= 0, go > 0)

            def body(c, sub=sub, mass_left=mass_left):
                j, _ = c
                (per_head,) = visit([j], [(sub, [0], False, (car_ref[sub, 0], car_ref[sub, 1]))])
                for hd, (acc, car) in enumerate(per_head):
                    acc_ref[sub, hd] += acc
                    car_ref[sub, hd] = car
                return j - 1, mass_left()

            lax.while_loop(cond, body, (step * nsub + sub - 2, mass_left()))

    most = functools.reduce(jnp.maximum, [car_ref[sub, hd] for sub in range(nsub) for hd in range(2)])
    pl.when(jnp.max(most) > SB_SKIP_LOG2)(remaining_blocks)
    for sub in range(nsub):
        o_ref[0, sub * blk:(sub + 1) * blk, :] = jnp.where(lane < HEAD_DIM, acc_ref[sub, 0],
                                                           acc_ref[sub, 1]).astype(o_ref.dtype)


def _sb_attention(qkv, batch, seq, blk=SB_BLOCK, nsub=SB_SUBBLOCKS):
    qkv3 = qkv.reshape(batch, seq, 3 * BRANCH_WIDTH)
    pairs = BRANCH_WIDTH // LANES
    tq = blk * nsub
    r = np.arange(blk)
    m_ext = jnp.asarray(r[:, None] > r[None, :], BF16)
    out = pl.pallas_call(
        functools.partial(_sba_kernel, blk=blk, nsub=nsub),
        grid=(batch, pairs, seq // tq),
        in_specs=[pl.BlockSpec((1, tq, LANES), lambda b, p, i: (b, i, p)),
                  pl.BlockSpec((1, seq, LANES), lambda b, p, i: (b, 0, pairs + p)),
                  pl.BlockSpec((1, seq, LANES), lambda b, p, i: (b, 0, 2 * pairs + p)),
                  pl.BlockSpec(m_ext.shape, lambda b, p, i: (0, 0))],
        out_specs=pl.BlockSpec((1, tq, LANES), lambda b, p, i: (b, i, p)),
        out_shape=jax.ShapeDtypeStruct((batch, seq, BRANCH_WIDTH), BF16),
        scratch_shapes=[pltpu.VMEM((nsub, 2, blk, LANES), F32), pltpu.VMEM((nsub, 2, blk, 1), F32)],
        compiler_params=_cparams(("parallel", "parallel", "parallel")),
        name="sb_attention",
    )(qkv3, qkv3, qkv3, m_ext)
    return out.reshape(batch * seq, BRANCH_WIDTH)


def _s5_operators(lam_re, lam_im, log_dt, b_re, b_im, c_re, c_im, d_skip):
    L = S5_CHUNK
    g, p, c = S5_GROUPS, S5_STATE, S5_GROUP_CH
    dt = jnp.exp(log_dt.astype(F32))[:, None]
    lr = lam_re.astype(F32)
    li = lam_im.astype(F32)

    def a_pow(n):
        n = jnp.asarray(n, F32)[..., None, None]
        mag = jnp.exp(lr * dt * n)
        return mag * jnp.cos(li * dt * n), mag * jnp.sin(li * dt * n)

    ar, ai = a_pow(1.0)
    den = lr * lr + li * li
    nr = ar - 1.0
    fr = (nr * lr + ai * li) / den
    fi = (ai * lr - nr * li) / den
    bbr = fr[..., None] * b_re - fi[..., None] * b_im
    bbi = fr[..., None] * b_im + fi[..., None] * b_re
    cr = c_re.astype(F32)
    ci = c_im.astype(F32)

    pr, pi = a_pow(np.arange(L))
    car = cr[None] * pr[:, :, None, :] - ci[None] * pi[:, :, None, :]
    cai = cr[None] * pi[:, :, None, :] + ci[None] * pr[:, :, None, :]
    klag = jnp.einsum('jgop,gpi->jgoi', car, bbr) - jnp.einsum('jgop,gpi->jgoi', cai, bbi)
    klag = klag.at[0].add(d_skip.astype(F32).reshape(g, c)[:, :, None] * jnp.eye(c, dtype=F32))
    kflat = klag.transpose(1, 3, 0, 2).reshape(g, c, L * c)
    toe = jnp.stack([jnp.pad(kflat[:, :, :(L - s) * c], ((0, 0), (0, 0), (s * c, 0))) for s in range(L)], axis=1)
    toe = toe.reshape(g, L * c, L * c)

    qr, qi = a_pow(L - 1 - np.arange(L))
    p_re = qr[:, :, :, None] * bbr[None] - qi[:, :, :, None] * bbi[None]
    p_im = qr[:, :, :, None] * bbi[None] + qi[:, :, :, None] * bbr[None]
    p_re = p_re.transpose(1, 0, 3, 2).reshape(g, L * c, p)
    p_im = p_im.transpose(1, 0, 3, 2).reshape(g, L * c, p)

    er, ei = a_pow(1 + np.arange(L))
    q_re = (cr[None] * er[:, :, None, :] - ci[None] * ei[:, :, None, :])
    q_im = -(cr[None] * ei[:, :, None, :] + ci[None] * er[:, :, None, :])
    q_re = q_re.transpose(1, 3, 0, 2).reshape(g, p, L * c)
    q_im = q_im.transpose(1, 3, 0, 2).reshape(g, p, L * c)

    a_lr, a_li = a_pow(float(L))

    def pair_diag(m):
        gg, r, cc = m.shape
        m = m.reshape(gg // 2, 2, r, cc)
        z = jnp.zeros_like(m[:, 0])
        top = jnp.concatenate([m[:, 0], z], axis=2)
        bot = jnp.concatenate([z, m[:, 1]], axis=2)
        return jnp.concatenate([top, bot], axis=1)

    toe_p = pair_diag(toe).astype(BF16)
    p_p = jnp.concatenate([pair_diag(p_re), pair_diag(p_im)], axis=2).astype(BF16)
    q_p = jnp.concatenate([pair_diag(q_re), pair_diag(q_im)], axis=1).astype(BF16)
    a_re = a_lr.reshape(1, g * p)
    a_im = a_li.reshape(1, g * p)
    return toe_p, p_p, q_p, a_re, a_im


def _group_block_transpose(tiles):
    group = lax.broadcasted_iota(jnp.int32, tiles[0].shape, 1) // S5_GROUP_CH
    n = len(tiles)
    k = n // 2
    while k >= 1:
        upper = (group & k) != 0
        nxt = list(tiles)
        for i in range(n):
            if i & k == 0:
                lo, hi = tiles[i], tiles[i + k]
                nxt[i] = jnp.where(upper, pltpu.roll(hi, k * S5_GROUP_CH, 1), lo)
                nxt[i + k] = jnp.where(upper, hi, pltpu.roll(lo, LANES - k * S5_GROUP_CH, 1))
        tiles = nxt
        k //= 2
    return tiles


def _s5_state_in_kernel(u_ref, p_ref, up_ref, vre_ref, vim_ref, *, nb):
    L = S5_CHUNK
    per = LANES // S5_GROUP_CH
    for kb in range(u_ref.shape[0]):
        halves = []
        for a in range(L // per):
            steps = [u_ref[kb, pl.ds(a * per + b, nb, stride=L), :] for b in range(per)]
            halves.append(_group_block_transpose(steps))
        for pi in range(per // 2):
            pair = kb * (per // 2) + pi
            row = jnp.concatenate([halves[a][2 * pi + gi] for gi in range(2) for a in range(L // per)], axis=1)
            row = row.astype(BF16)
            up_ref[pair] = row
            v = _dot(row, p_ref[pair])
            vre_ref[:, pair * LANES:(pair + 1) * LANES] = v[:, :LANES]
            vim_ref[:, pair * LANES:(pair + 1) * LANES] = v[:, LANES:]


def _s5_out_kernel(up_ref, sre_ref, sim_ref, toe_ref, q_ref, y_ref, *, nb):
    L = S5_CHUNK
    per = LANES // S5_GROUP_CH
    for kb in range(y_ref.shape[0]):
        halves = [[None] * per for _ in range(L // per)]
        for pi in range(per // 2):
            pair = kb * (per // 2) + pi
            q = q_ref[pair]
            cols = slice(pair * LANES, (pair + 1) * LANES)
            y = _dot(up_ref[pair], toe_ref[pair])
            y += _dot(sre_ref[:, cols].astype(BF16), q[:LANES]) + _dot(sim_ref[:, cols].astype(BF16), q[LANES:])
            y = jax.nn.gelu(y)
            for gi in range(2):
                for a in range(L // per):
                    j = gi * (L // per) + a
                    halves[a][2 * pi + gi] = y[:, j * LANES:(j + 1) * LANES]
        for a in range(L // per):
            steps = _group_block_transpose(halves[a])
            for b in range(per):
                y_ref[kb, pl.ds(a * per + b, nb, stride=L), :] = steps[b]


def _s5_kernel(u_ref, p_ref, toe_ref, q_ref, are_ref, aim_ref, y_ref, up_ref, vre_ref, vim_ref, sre_ref, sim_ref,
               x_ref, *, nb, steps_per_seq):
    @pl.when(pl.program_id(0) % steps_per_seq == 0)
    def _():
        x_ref[...] = jnp.zeros_like(x_ref)

    _s5_state_in_kernel(u_ref, p_ref, up_ref, vre_ref, vim_ref, nb=nb)
    ar = are_ref[...]
    ai = aim_ref[...]

    def step(r, carry):
        xr, xi = carry
        sre_ref[pl.ds(r, 1), :] = xr
        sim_ref[pl.ds(r, 1), :] = xi
        vr = vre_ref[pl.ds(r, 1), :]
        vi = vim_ref[pl.ds(r, 1), :]
        return ar * xr - ai * xi + vr, ar * xi + ai * xr + vi

    xr, xi = lax.fori_loop(0, nb, step, (x_ref[0:1, :], x_ref[1:2, :]))
    x_ref[0:1, :] = xr
    x_ref[1:2, :] = xi
    _s5_out_kernel(up_ref, sre_ref, sim_ref, toe_ref, q_ref, y_ref, nb=nb)


def _s5_ssm(u, batch, seq, ops, layer, nb=128):
    toe_p, p_p, q_p, a_re, a_im = ops
    nslab, t, _ = u.shape
    L = S5_CHUNK
    npair = S5_GROUPS // 2
    nc = t // L
    row = 2 * L * S5_GROUP_CH
    ncols = S5_GROUPS * S5_STATE
    nb = min(nb, seq // L)

    slab_spec = pl.BlockSpec((nslab, nb * L, LANES), lambda i: (0, i, 0))
    return pl.pallas_call(
        functools.partial(_s5_kernel, nb=nb, steps_per_seq=seq // (L * nb)),
        grid=(nc // nb,),
        in_specs=[slab_spec] + [_layer_block(a, layer) for a in (p_p, toe_p, q_p, a_re, a_im)],
        out_specs=slab_spec,
        out_shape=jax.ShapeDtypeStruct((nslab, t, LANES), F32),
        scratch_shapes=[pltpu.VMEM((npair, nb, row), BF16)] + [pltpu.VMEM((nb, ncols), F32)] * 4
        + [pltpu.VMEM((8, ncols), F32)],
        compiler_params=_cparams(("arbitrary",)),
        name="s5_ssm",
    )(u, p_p, toe_p, q_p, a_re, a_im)


def _ret_tables(seq):
    half = HEAD_DIM // 2
    pos = jnp.arange(seq, dtype=F32)
    inv = ROPE_BASE ** (-jnp.arange(half, dtype=F32) / half)
    ang = pos[:, None] * inv[None, :]
    cos = jnp.tile(jnp.cos(ang), (1, LANES // half))
    sin = jnp.sin(ang)
    sin = jnp.tile(jnp.concatenate([-sin, sin], axis=1), (1, LANES // HEAD_DIM))
    c = RET_CHUNK
    gamma = 1.0 - 2.0 ** (-5.0 - jnp.arange(RET_HEADS, dtype=F32))
    log_g = jnp.log(gamma)
    idx = jnp.arange(c, dtype=F32)
    diff = idx[:, None] - idx[None, :]
    intra = jnp.where(diff >= 0, jnp.exp(jnp.maximum(diff, 0.0)[None] * log_g[:, None, None]), 0.0)
    q_dec = jnp.exp((idx + 1.0)[None, :] * log_g[:, None])
    k_dec = jnp.exp((c - 1.0 - idx)[None, :] * log_g[:, None])
    ch_dec = jnp.exp(c * log_g)

    def lanes(x):
        x = x.reshape(RET_HEADS // 2, 2, c)
        return jnp.repeat(x.transpose(0, 2, 1), HEAD_DIM, axis=2)

    q_dec_l = lanes(q_dec)
    k_dec_l = lanes(k_dec)
    hp = RET_HEADS // 2
    blockdiag = np.kron(np.eye(2), np.ones((HEAD_DIM, HEAD_DIM)))
    st_dec = jnp.repeat(ch_dec.reshape(hp, 2), HEAD_DIM, axis=1)[:, :, None] * blockdiag[None]
    avg = jnp.asarray(blockdiag / HEAD_DIM, BF16)
    return cos, sin, intra, q_dec_l, k_dec_l, st_dec.astype(F32), avg


def _ret_kernel(q_ref, k_ref, v_ref, g_ref, cos_ref, sin_ref, intra_ref, qd_ref, kd_ref, sd_ref, avg_ref, nw_ref,
                o_ref, st_ref, *, nchunk):
    c = RET_CHUNK

    @pl.when(pl.program_id(2) == 0)
    def _():
        st_ref[...] = jnp.zeros_like(st_ref)

    tq = q_ref.shape[1]
    lane = lax.broadcasted_iota(jnp.int32, (tq, LANES), 1)
    first_half = (lane % HEAD_DIM) < (HEAD_DIM // 2)
    head0 = lax.broadcasted_iota(jnp.int32, (c, LANES), 1) < HEAD_DIM
    sd = sd_ref[0]
    avg = avg_ref[...]
    same_head = avg > 0
    cos = cos_ref[...]
    sin = sin_ref[...]

    def rotary(x):
        swapped = jnp.where(first_half, pltpu.roll(x, LANES - HEAD_DIM // 2, 1), pltpu.roll(x, HEAD_DIM // 2, 1))
        return x * cos + swapped * sin

    q_all = rotary(q_ref[0].astype(F32)).astype(BF16)
    k_all = rotary(k_ref[0].astype(F32)) * (HEAD_DIM ** -0.5)
    kb_all = k_all.astype(BF16)
    zero = jnp.zeros((c, LANES), BF16)
    chunks = [slice(ci * c, (ci + 1) * c) for ci in range(nchunk)]
    vs = [v_ref[0, rows, :] for rows in chunks]
    scores, incs = [], []
    for ci, rows in enumerate(chunks):
        qb = q_all[rows]
        q2 = jnp.concatenate([jnp.where(head0, qb, zero), jnp.where(head0, zero, qb)], axis=0)
        scores.append(_dot_nt(q2, kb_all[rows]))
        kd_t = (k_all[rows] * kd_ref[0]).T.astype(BF16)
        incs.append(_dot(kd_t, vs[ci]))
    state = st_ref[...]
    states = []
    for inc in incs:
        states.append(state.astype(BF16))
        state = state * sd + jnp.where(same_head, inc, 0.0)
    st_ref[...] = state
    intra2 = jnp.concatenate([intra_ref[0, 0], intra_ref[0, 1]], axis=0)
    outs = []
    for ci, rows in enumerate(chunks):
        o2 = _dot((scores[ci] * intra2).astype(BF16), vs[ci])
        outs.append(jnp.where(head0, o2[:c], o2[c:]) + _dot(q_all[rows], states[ci]) * qd_ref[0])
    out = jnp.concatenate(outs, axis=0)
    hi, lo = _split_bf16(out)
    mu = _dot(hi, avg) + _dot(lo, avg)
    dlt = out - mu
    hi, lo = _split_bf16(dlt * dlt)
    var = _dot(hi, avg) + _dot(lo, avg)
    gate = g_ref[0].astype(F32)
    y = dlt * lax.rsqrt(var + EPS) * nw_ref[...] * (gate * jax.nn.sigmoid(gate))
    o_ref[0] = y.astype(o_ref.dtype)


def _retention(ret, norm_w, batch, seq, tables, tq=2048):
    cos, sin, intra, q_dec_l, k_dec_l, st_dec, avg = tables
    pairs = BRANCH_WIDTH // LANES
    ret3 = ret.reshape(batch, seq, 4 * BRANCH_WIDTH)
    tq = min(tq, seq)
    c = RET_CHUNK

    def col(k):
        return pl.BlockSpec((1, tq, LANES), lambda b, p, i: (b, i, k * pairs + p))

    out = pl.pallas_call(
        functools.partial(_ret_kernel, nchunk=tq // c),
        grid=(batch, pairs, seq // tq),
        in_specs=[col(0), col(1), col(2), col(3),
                  pl.BlockSpec((tq, LANES), lambda b, p, i: (i, 0)),
                  pl.BlockSpec((tq, LANES), lambda b, p, i: (i, 0)),
                  pl.BlockSpec((1, 2, c, c), lambda b, p, i: (p, 0, 0, 0)),
                  pl.BlockSpec((1, c, LANES), lambda b, p, i: (p, 0, 0)),
                  pl.BlockSpec((1, c, LANES), lambda b, p, i: (p, 0, 0)),
                  pl.BlockSpec((1, LANES, LANES), lambda b, p, i: (p, 0, 0)),
                  pl.BlockSpec((LANES, LANES), lambda b, p, i: (0, 0)),
                  pl.BlockSpec((1, LANES), lambda b, p, i: (0, p))],
        out_specs=pl.BlockSpec((1, tq, LANES), lambda b, p, i: (b, i, p)),
        out_shape=jax.ShapeDtypeStruct((batch, seq, BRANCH_WIDTH), BF16),
        scratch_shapes=[pltpu.VMEM((LANES, LANES), F32)],
        compiler_params=_cparams(("parallel", "parallel", "arbitrary")),
        name="retention",
    )(ret3, ret3, ret3, ret3, cos, sin, intra.reshape(pairs, 2, c, c), q_dec_l, k_dec_l, st_dec, avg,
      norm_w.reshape(1, BRANCH_WIDTH).astype(F32))
    return out.reshape(batch * seq, BRANCH_WIDTH)


def _route(logits):
    lane = lax.broadcasted_iota(jnp.int32, logits.shape, 1).astype(F32)
    neg = jnp.float32(-jnp.inf)
    big = jnp.float32(1 << 20)
    gl = jnp.where(lane < N_GROUPS, logits, neg)
    gmax = jnp.max(gl, axis=1, keepdims=True)
    gsum = jnp.sum(jnp.exp(gl - gmax), axis=1, keepdims=True)
    g_w = 1.0 / gsum
    g_idx = jnp.min(jnp.where(gl == gmax, lane, big), axis=1, keepdims=True)
    lo = ROUTER_LANE0 + EXPERTS_PER_GROUP * g_idx
    sl = jnp.where((lane >= lo) & (lane < lo + EXPERTS_PER_GROUP), logits, neg)
    m1 = jnp.max(sl, axis=1, keepdims=True)
    i1 = jnp.min(jnp.where(sl == m1, lane, big), axis=1, keepdims=True)
    sl2 = jnp.where(lane == i1, neg, sl)
    m2 = jnp.max(sl2, axis=1, keepdims=True)
    i2 = jnp.min(jnp.where(sl2 == m2, lane, big), axis=1, keepdims=True)
    e2 = jnp.exp(m2 - m1)
    w1 = g_w / (1.0 + e2)
    w2 = g_w * e2 / (1.0 + e2)
    packed = jnp.where(lane == 0.0, i1 - ROUTER_LANE0, jnp.where(lane == 1.0, i2 - ROUTER_LANE0, 0.0))
    return packed + jnp.where(lane == 2.0, w1, 0.0) + jnp.where(lane == 3.0, w2, 0.0)


def _merge_kernel(ysb_ref, ys5_ref, yret_ref, gate_ref, h_ref, wglu_ref, wbr_ref, wout_ref, n2_ref, wr_hi_ref,
                  wr_lo_ref, br_ref, tri_ref, cum_ref, h_out_ref, hn_ref, route_ref, pos_ref, cnt_ref):
    ys5 = jnp.concatenate([ys5_ref[s] for s in range(ys5_ref.shape[0])], axis=1).astype(BF16)
    glu = _dot(ys5, wglu_ref[...])
    y_s5 = (glu[:, :BRANCH_WIDTH] * jax.nn.sigmoid(glu[:, BRANCH_WIDTH:])).astype(BF16)
    merged = None
    for n, y in enumerate((ysb_ref[...], y_s5, yret_ref[...])):
        g = jax.nn.sigmoid(gate_ref[:, n * D_MODEL:(n + 1) * D_MODEL].astype(F32))
        term = g * _dot(y, wbr_ref[n])
        merged = term if merged is None else merged + term
    h = h_ref[...] + _dot(merged.astype(BF16), wout_ref[...])
    h_out_ref[...] = h
    ms = jnp.mean(h * h, axis=-1, keepdims=True)
    hn = h * lax.rsqrt(ms + EPS) * n2_ref[...]
    hn_ref[...] = hn.astype(BF16)
    hi, lo = _split_bf16(hn)
    logits = _dot(hi, wr_hi_ref[...]) + _dot(hi, wr_lo_ref[...]) + _dot(lo, wr_hi_ref[...]) + br_ref[...]
    route = _route(logits)
    route_ref[...] = route
    pos_ref[...], cnt_ref[0] = _moe_plan(route, tri_ref[...], cum_ref[...])


def _merge(y_sb, y_s5, y_ret, gates, h, w_glu, w_br, w_out, norm2_w, wr_hi, wr_lo, b_r, layer):
    t = h.shape[0]
    tm = min(MOE_TILE, t)
    r = np.arange(tm)
    tri = jnp.asarray(r[None, :] <= r[:, None], BF16)
    l = np.arange(LANES)
    cum = jnp.asarray(l[:, None] < l[None, :], BF16)

    def rows(w):
        return pl.BlockSpec((tm, w), lambda i: (i, 0))

    return pl.pallas_call(
        _merge_kernel,
        grid=(t // tm,),
        in_specs=[rows(BRANCH_WIDTH), pl.BlockSpec((y_s5.shape[0], tm, LANES), lambda i: (0, i, 0)),
                  rows(BRANCH_WIDTH), rows(N_BRANCH * D_MODEL),
                  rows(D_MODEL)] + [_layer_block(a, layer) for a in (w_glu, w_br, w_out, norm2_w, wr_hi, wr_lo, b_r)]
        + [pl.BlockSpec(tri.shape, lambda i: (0, 0), pipeline_mode=pl.Buffered(1)),
           pl.BlockSpec(cum.shape, lambda i: (0, 0), pipeline_mode=pl.Buffered(1))],
        out_specs=[rows(D_MODEL), rows(D_MODEL), rows(LANES), rows(LANES), pl.BlockSpec((1, 8, LANES), lambda i: (i, 0, 0))],
        out_shape=[jax.ShapeDtypeStruct((t, D_MODEL), F32), jax.ShapeDtypeStruct((t, D_MODEL), BF16),
                   jax.ShapeDtypeStruct((t, LANES), F32), jax.ShapeDtypeStruct((t, LANES), F32),
                   jax.ShapeDtypeStruct((t // tm, 8, LANES), F32)],
        compiler_params=_cparams(("parallel",)),
        name="merge_route",
    )(y_sb, y_s5, y_ret, gates, h, w_glu, w_br, w_out, norm2_w, wr_hi, wr_lo, b_r, tri, cum)


MOE_TILE = 512
MOE_ROW_TILE = 512
SEG_ALIGN = 16
MOE_SLOTS = TOP_K * MOE_TILE + N_EXPERTS * SEG_ALIGN
MOE_CHUNKS = MOE_SLOTS // SEG_ALIGN
MOE_SLOT_BLOCK = 256


def _moe_plan(route, tri, cum):
    tm = route.shape[0]
    lane = lax.broadcasted_iota(jnp.int32, route.shape, 1).astype(F32)
    e1 = route[:, 0:1]
    e2 = route[:, 1:2]
    chosen = (lane == e1) | (lane == e2)
    onehot = jnp.where(chosen, 1.0, 0.0).astype(BF16)
    incl = _dot(tri, onehot)
    cnt = incl[tm - 1:tm, :]
    padded = jnp.floor((cnt + (SEG_ALIGN - 1)) * (1.0 / SEG_ALIGN)) * SEG_ALIGN
    start = _dot(jnp.broadcast_to(padded, (8, LANES)).astype(BF16), cum)[0:1, :]
    slot = start + incl - 1.0
    p1 = jnp.sum(jnp.where(lane == e1, slot, 0.0), axis=1, keepdims=True)
    p2 = jnp.sum(jnp.where(lane == e2, slot, 0.0), axis=1, keepdims=True)
    pos = jnp.where(lane == 0.0, p1, jnp.where(lane == 1.0, p2, 0.0))
    return pos, jnp.broadcast_to(cnt, (8, LANES))


def _moe_tables(cnt, rt, n_rows):
    a = SEG_ALIGN
    ntile = cnt.shape[0]
    p = (cnt + a - 1) // a * a
    lend = jnp.cumsum(p, axis=1)
    lstart = lend - p
    seg = p.sum(axis=0)
    segpad = (seg + rt - 1) // rt * rt
    gend = jnp.cumsum(segpad)
    gstart = gend - segpad
    toff = gstart[None, :] + jnp.cumsum(p, axis=0) - p
    cstart = a * jnp.arange(MOE_CHUNKS, dtype=jnp.int32)
    in_run = (cstart[None, :, None] >= lstart[:, None, :]) & (cstart[None, :, None] < lend[:, None, :])
    gdst = (jnp.sum(jnp.where(in_run, (toff - lstart)[:, None, :], 0), axis=2) + cstart[None, :]) // a
    nch = lend[:, -1] // a
    tstart = rt * jnp.arange(n_rows // rt, dtype=jnp.int32)
    tile_e = jnp.minimum(jnp.sum(tstart[:, None] >= gend[None, :], axis=1), N_EXPERTS - 1)
    n_used = gend[-1] // rt
    zmax = rt // a - 1
    zper = -(-N_EXPERTS * zmax // ntile)
    k = jnp.arange(zmax, dtype=jnp.int32)
    zd = ((gstart + seg) // a)[:, None] + k[None, :]
    valid = (k[None, :] < ((segpad - seg) // a)[:, None]).reshape(-1)
    order = jnp.argsort(jnp.logical_not(valid), stable=True)
    zdst = jnp.pad(zd.reshape(-1)[order], (0, ntile * zper - N_EXPERTS * zmax))
    nz = valid.sum()
    i32 = jnp.int32
    return (gdst.reshape(-1).astype(i32), nch.astype(i32), zdst.astype(i32), nz.reshape(1).astype(i32),
            tile_e.astype(i32), n_used.reshape(1).astype(i32)), zper


def _moe_dispatch_kernel(gdst_ref, nch_ref, zdst_ref, nz_ref, n_used_ref, pos_ref, hn_ref, xs_hbm, stg, zblk, sem,
                         zsem, *, zper):
    i = pl.program_id(0)
    last_step = pl.num_programs(0) - 1
    buf = i % 2
    tm = hn_ref.shape[0]
    pos_t = pos_ref[...].T
    p1 = pos_t[0:1, :]
    p2 = pos_t[1:2, :]
    x = hn_ref[...]
    for lo in range(0, MOE_SLOTS, tm):
        slot = (lax.broadcasted_iota(jnp.int32, (tm, tm), 0) + lo).astype(F32)
        onehot = jnp.where((slot == p1) | (slot == p2), 1.0, 0.0).astype(BF16)
        stg[buf, lo:lo + tm, :] = _dot(onehot, x).astype(BF16)

    def chunk_copy(which, src_chunk, dst_chunk):
        src = pl.multiple_of(src_chunk * SEG_ALIGN, SEG_ALIGN)
        dst = pl.multiple_of(dst_chunk * SEG_ALIGN, SEG_ALIGN)
        return pltpu.make_async_copy(stg.at[which, pl.ds(src, SEG_ALIGN)], xs_hbm.at[pl.ds(dst, SEG_ALIGN)],
                                     sem.at[which])

    def n_copies(step):
        return nch_ref[step], jnp.clip(nz_ref[0] - step * zper, 0, zper)

    def drain(which, step):
        for count in n_copies(step):
            rows = count * SEG_ALIGN

            @pl.when(rows > 0)
            def _():
                pltpu.make_async_copy(stg.at[which, pl.ds(0, rows)], xs_hbm.at[pl.ds(0, rows)],
                                      sem.at[which]).wait()

    n, nzero = n_copies(i)

    @pl.loop(0, n)
    def _(c):
        chunk_copy(buf, c, gdst_ref[i * MOE_CHUNKS + c]).start()

    @pl.loop(0, nzero)
    def _(k):
        chunk_copy(buf, MOE_CHUNKS - 1, zdst_ref[i * zper + k]).start()

    pl.when(i > 0)(lambda: drain(1 - buf, i - 1))

    @pl.when(i == last_step)
    def _():
        drain(buf, i)
        rt = zblk.shape[0]
        zblk[...] = jnp.zeros_like(zblk)
        first = n_used_ref[0]
        last = xs_hbm.shape[0] // rt

        def tile_copy(j):
            return pltpu.make_async_copy(zblk, xs_hbm.at[pl.ds(pl.multiple_of(j * rt, rt), rt)], zsem)

        @pl.loop(first, last)
        def _(j):
            tile_copy(j).start()

        @pl.loop(first, last)
        def _(j):
            tile_copy(0).wait()


def _moe_expert_kernel(tile_e_ref, n_used_ref, xs_ref, wg_ref, wu_ref, wd_ref, ys_ref):
    used = pl.program_id(0) < n_used_ref[0]

    @pl.when(used)
    def _():
        x = xs_ref[...]
        gate = _dot(x, wg_ref[0].astype(BF16))
        up = _dot(x, wu_ref[0].astype(BF16))
        hdn = gate * jax.nn.sigmoid(gate) * up
        ys_ref[...] = _dot(hdn.astype(BF16), wd_ref[0].astype(BF16)).astype(ys_ref.dtype)

    @pl.when(jnp.logical_not(used))
    def _():
        ys_ref[...] = jnp.zeros_like(ys_ref)


def _moe_combine_kernel(gdst_ref, nch_ref, route_ref, pos_ref, h_ref, nw_ref, ys_hbm, o_ref, stg, sem, *, out_norm):
    i = pl.program_id(0)
    nsteps = pl.num_programs(0)
    buf = i % 2
    tm = h_ref.shape[0]

    def chunk_copy(which, src_chunk, dst_chunk):
        src = pl.multiple_of(src_chunk * SEG_ALIGN, SEG_ALIGN)
        dst = pl.multiple_of(dst_chunk * SEG_ALIGN, SEG_ALIGN)
        return pltpu.make_async_copy(ys_hbm.at[pl.ds(src, SEG_ALIGN)], stg.at[which, pl.ds(dst, SEG_ALIGN)],
                                     sem.at[which])

    def fetch(step, which):
        n = nch_ref[step]

        @pl.loop(0, n)
        def _(c):
            chunk_copy(which, gdst_ref[step * MOE_CHUNKS + c], c).start()

        per = MOE_SLOT_BLOCK // SEG_ALIGN
        stop = jnp.minimum((n + per - 1) // per * per, MOE_CHUNKS)

        @pl.loop(n, stop)
        def _(c):
            rows = pl.ds(pl.multiple_of(c * SEG_ALIGN, SEG_ALIGN), SEG_ALIGN)
            stg[which, rows, :] = jnp.zeros((SEG_ALIGN, D_MODEL), BF16)

    pl.when(i == 0)(lambda: fetch(0, 0))
    pl.when(i + 1 < nsteps)(lambda: fetch(i + 1, 1 - buf))

    r = route_ref[...]
    pos = pos_ref[...]

    def scatter(lo, hi):
        slot = (lax.broadcasted_iota(jnp.int32, (tm, hi - lo), 1) + lo).astype(F32)
        w = jnp.where(slot == pos[:, 0:1], r[:, 2:3], 0.0) + jnp.where(slot == pos[:, 1:2], r[:, 3:4], 0.0)
        return _dot(w.astype(BF16), stg[buf, lo:hi, :])

    rows = nch_ref[i] * SEG_ALIGN

    @pl.when(rows > 0)
    def _():
        pltpu.make_async_copy(ys_hbm.at[pl.ds(0, rows)], stg.at[buf, pl.ds(0, rows)], sem.at[buf]).wait()

    always = TOP_K * tm
    o_ref[...] = h_ref[...] + scatter(0, always)
    for lo in range(always, MOE_SLOTS, MOE_SLOT_BLOCK):
        @pl.when(lo < rows)
        def _(lo=lo):
            o_ref[...] += scatter(lo, lo + MOE_SLOT_BLOCK)
    if out_norm:
        out = o_ref[...]
        ms = jnp.mean(out * out, axis=-1, keepdims=True)
        o_ref[...] = out * lax.rsqrt(ms + EPS) * nw_ref[...]


def _moe(hn, route, pos, cnt, h, w_gate, w_up, w_down, layer, out_norm_w, out_norm):
    t = h.shape[0]
    tm = min(MOE_TILE, t)
    rt = MOE_ROW_TILE
    ntile = t // tm
    n_rows = TOP_K * t + ntile * N_EXPERTS * (SEG_ALIGN - 1) + N_EXPERTS * (rt - SEG_ALIGN)
    n_rows = (n_rows + rt - 1) // rt * rt
    (gdst, nch, zdst, nz, tile_e, n_used), zper = _moe_tables(cnt[:, 0, :N_EXPERTS].astype(jnp.int32), rt, n_rows)

    def expert_block(rows, cols):
        return pl.BlockSpec((None, 1, rows, cols), lambda j, te, nu: (layer, te[j], 0, 0))

    xs = pl.pallas_call(
        functools.partial(_moe_dispatch_kernel, zper=zper),
        grid_spec=pltpu.PrefetchScalarGridSpec(
            num_scalar_prefetch=5,
            grid=(ntile,),
            in_specs=[pl.BlockSpec((tm, LANES), lambda i, *_: (i, 0)),
                      pl.BlockSpec((tm, D_MODEL), lambda i, *_: (i, 0))],
            out_specs=pl.BlockSpec(memory_space=pl.ANY),
            scratch_shapes=[pltpu.VMEM((2, MOE_SLOTS, D_MODEL), BF16), pltpu.VMEM((rt, D_MODEL), BF16),
                            pltpu.SemaphoreType.DMA((2,)), pltpu.SemaphoreType.DMA(())]),
        out_shape=jax.ShapeDtypeStruct((n_rows, D_MODEL), BF16),
        compiler_params=_cparams(("arbitrary",)),
        name="moe_dispatch",
    )(gdst, nch, zdst, nz, n_used, pos, hn)
    ys = pl.pallas_call(
        _moe_expert_kernel,
        grid_spec=pltpu.PrefetchScalarGridSpec(
            num_scalar_prefetch=2,
            grid=(n_rows // rt,),
            in_specs=[pl.BlockSpec((rt, D_MODEL), lambda j, te, nu: (jnp.minimum(j, nu[0] - 1), 0)),
                      expert_block(D_MODEL, EXPERT_FF), expert_block(D_MODEL, EXPERT_FF),
                      expert_block(EXPERT_FF, D_MODEL)],
            out_specs=pl.BlockSpec((rt, D_MODEL), lambda j, te, nu: (j, 0))),
        out_shape=jax.ShapeDtypeStruct((n_rows, D_MODEL), BF16),
        compiler_params=_cparams(("arbitrary",)),
        name="moe_experts",
    )(tile_e, n_used, xs, w_gate, w_up, w_down)
    return pl.pallas_call(
        functools.partial(_moe_combine_kernel, out_norm=out_norm),
        grid_spec=pltpu.PrefetchScalarGridSpec(
            num_scalar_prefetch=2,
            grid=(ntile,),
            in_specs=[pl.BlockSpec((tm, LANES), lambda i, *_: (i, 0)),
                      pl.BlockSpec((tm, LANES), lambda i, *_: (i, 0)),
                      pl.BlockSpec((tm, D_MODEL), lambda i, *_: (i, 0)),
                      pl.BlockSpec((1, D_MODEL), lambda i, *_: (0, 0)),
                      pl.BlockSpec(memory_space=pl.ANY)],
            out_specs=pl.BlockSpec((tm, D_MODEL), lambda i, *_: (i, 0)),
            scratch_shapes=[pltpu.VMEM((2, MOE_SLOTS, D_MODEL), BF16), pltpu.SemaphoreType.DMA((2,))]),
        out_shape=jax.ShapeDtypeStruct((t, D_MODEL), F32),
        compiler_params=_cparams(("arbitrary",)),
        name="moe_combine",
    )(gdst, nch, route, pos, h, out_norm_w, ys)


def kernel(x, norm1_w, w_in, s5_lambda_re, s5_lambda_im, s5_log_dt, s5_b_re, s5_b_im, s5_c_re, s5_c_im, s5_d,
           s5_w_glu, ret_norm_w, w_branch, w_out, norm2_w, w_group, b_group, w_router, b_router, w_gate, w_up,
           w_down, final_norm_w):
    batch, seq, _ = x.shape
    depth = w_in.shape[0]
    t = batch * seq
    h = x.reshape(t, D_MODEL).astype(F32)
    ret_tables = _ret_tables(seq)
    w_in, s5_w_glu, w_branch, w_out = (w.astype(BF16) for w in (w_in, s5_w_glu, w_branch, w_out))
    norm1_w = norm1_w.reshape(depth, 1, D_MODEL).astype(F32)
    norm2_w = norm2_w.reshape(depth, 1, D_MODEL).astype(F32)
    s5_params = (s5_lambda_re, s5_lambda_im, s5_log_dt, s5_b_re, s5_b_im, s5_c_re, s5_c_im, s5_d)
    s5_ops = [jnp.stack(op) for op in zip(*(_s5_operators(*(p[layer] for p in s5_params)) for layer in range(depth)))]
    w_r = jnp.concatenate([w_group, w_router.transpose(0, 2, 1, 3).reshape(depth, D_MODEL, N_EXPERTS)], axis=2)
    w_r = jnp.pad(w_r.astype(F32), ((0, 0), (0, 0), (0, LANES - w_r.shape[2])))
    wr_hi, wr_lo = _split_bf16(w_r)
    b_r = jnp.pad(jnp.concatenate([b_group, b_router.reshape(depth, -1)], axis=1).astype(F32),
                  ((0, 0), (0, LANES - N_GROUPS - N_EXPERTS))).reshape(depth, 1, LANES)
    for layer in range(depth):
        qkv, u_s5, ret, gates = _inproj(h, norm1_w, w_in, layer)
        y_sb = _sb_attention(qkv, batch, seq)
        y_s5 = _s5_ssm(u_s5, batch, seq, s5_ops, layer)
        y_ret = _retention(ret, ret_norm_w[layer], batch, seq, ret_tables)
        h, hn, route, pos, cnt = _merge(y_sb, y_s5, y_ret, gates, h, s5_w_glu, w_branch, w_out, norm2_w, wr_hi, wr_lo, b_r,
                              layer)
        h = _moe(hn, route, pos, cnt, h, w_gate, w_up, w_down, layer, final_norm_w.reshape(1, D_MODEL).astype(F32),
                 out_norm=(layer == depth - 1))
    return h.reshape(batch, seq, D_MODEL).astype(x.dtype)
```

```python
import functools
import math

import jax
import jax.numpy as jnp
import numpy as np
from jax import lax
from jax.experimental import pallas as pl
from jax.experimental.pallas import tpu as pltpu

F32 = jnp.float32
BF16 = jnp.bfloat16

D_MODEL = 1024
SB_HEADS = 8
HEAD_DIM = 64
BRANCH_WIDTH = 512
S5_GROUPS = 32
S5_GROUP_CH = 16
S5_STATE = 64
RET_HEADS = 8
RET_CHUNK = 128
ROPE_BASE = 10000.0
N_BRANCH = 3
N_GROUPS = 4
EXPERTS_PER_GROUP = 8
N_EXPERTS = 32
TOP_K = 2
EXPERT_FF = 256
EPS = 1e-6

LANES = 128
VMEM_LIMIT = 56 * 1024 * 1024
S5_CHUNK = 16
SB_BLOCK = 256
SB_SUBBLOCKS = 8
LOG2_E = 1.4426950408889634
SB_SKIP_LOG2 = -152.0
ROUTER_LANE0 = N_GROUPS


def _cparams(sem):
    return pltpu.CompilerParams(dimension_semantics=sem, vmem_limit_bytes=VMEM_LIMIT)


def _layer_block(stacked, layer):
    zeros = (0,) * (stacked.ndim - 1)
    return pl.BlockSpec((None,) + stacked.shape[1:], lambda *_: (layer,) + zeros, pipeline_mode=pl.Buffered(1))


def _split_bf16(x):
    hi = x.astype(BF16)
    lo = (x - hi.astype(F32)).astype(BF16)
    return hi, lo


def _dot(a, b):
    return jnp.dot(a, b, preferred_element_type=F32)


def _dot_nt(a, b):
    return lax.dot_general(a, b, (((1,), (1,)), ((), ())), preferred_element_type=F32)


def _dot_tn(a, b):
    return lax.dot_general(a, b, (((0,), (0,)), ((), ())), preferred_element_type=F32)


def _inproj_kernel(h_ref, nw_ref, w_ref, qkv_ref, u_ref, ret_ref, gate_ref):
    x = h_ref[...]
    ms = jnp.mean(x * x, axis=-1, keepdims=True)
    xn = (x * lax.rsqrt(ms + EPS) * nw_ref[...]).astype(BF16)
    off = 0
    for ref in (qkv_ref, u_ref, ret_ref, gate_ref):
        if ref is u_ref:
            u = _dot(xn, w_ref[:, off:off + BRANCH_WIDTH])
            for s in range(u_ref.shape[0]):
                u_ref[s] = u[:, s * LANES:(s + 1) * LANES]
            off += BRANCH_WIDTH
            continue
        width = ref.shape[1]
        for c in range(0, width, 512):
            ref[:, c:c + 512] = _dot(xn, w_ref[:, off + c:off + c + 512]).astype(ref.dtype)
        off += width


def _inproj(h, norm_w, w_in, layer, tm=512):
    t = h.shape[0]
    nslab = BRANCH_WIDTH // LANES

    def rows(w):
        return pl.BlockSpec((tm, w), lambda i: (i, 0))

    return pl.pallas_call(
        _inproj_kernel,
        grid=(t // tm,),
        in_specs=[rows(D_MODEL),
                  _layer_block(norm_w, layer), _layer_block(w_in, layer)],
        out_specs=[rows(3 * BRANCH_WIDTH), pl.BlockSpec((nslab, tm, LANES), lambda i: (0, i, 0)),
                   rows(4 * BRANCH_WIDTH), rows(N_BRANCH * D_MODEL)],
        out_shape=[jax.ShapeDtypeStruct((t, 3 * BRANCH_WIDTH), BF16),
                   jax.ShapeDtypeStruct((nslab, t, LANES), F32),
                   jax.ShapeDtypeStruct((t, 4 * BRANCH_WIDTH), BF16),
                   jax.ShapeDtypeStruct((t, N_BRANCH * D_MODEL), BF16)],
        compiler_params=_cparams(("parallel",)),
        name="inproj",
    )(h, norm_w, w_in)


def _sba_kernel(q_ref, k_ref, v_ref, m_ref, o_ref, acc_ref, car_ref, *, blk, nsub):
    step = pl.program_id(2)
    lane = lax.broadcasted_iota(jnp.int32, (blk, LANES), 1)
    zscale = HEAD_DIM ** -0.5 * LOG2_E
    tri = m_ref[...]
    q_heads = []
    for sub in range(nsub):
        q = q_ref[0, sub * blk:(sub + 1) * blk, :]
        zero = jnp.zeros_like(q)
        q_heads.append((jnp.where(lane < HEAD_DIM, q, zero), jnp.where(lane >= HEAD_DIM, q, zero)))

    def keys(j):
        start = pl.multiple_of(j * blk, blk)
        return k_ref[0, pl.ds(start, blk), :], v_ref[0, pl.ds(start, blk), :]

    def log_terms(z2, causal):
        sp = jnp.log(1.0 + jnp.exp2(-jnp.abs(z2))) * LOG2_E
        log_beta = jnp.minimum(z2, 0.0) - sp
        log_1m = log_beta - z2
        if causal is not None:
            log_1m = jnp.where(causal, log_1m, 0.0)
        suffix = _dot(log_1m.astype(BF16), tri)
        return log_beta, suffix, jnp.sum(log_1m, axis=1, keepdims=True)

    def visit(key_blocks, jobs):
        causal = lax.broadcasted_iota(jnp.int32, (blk, blk), 1) < lax.broadcasted_iota(jnp.int32, (blk, blk), 0)
        kv = [keys(j) for j in key_blocks]
        chains = [(n, b, hd) for n, job in enumerate(jobs) for b in range(len(job[1])) for hd in range(2)]

        def mask(n, b):
            return causal if (jobs[n][2] and b == 0) else None

        z2 = {c: _dot_nt(q_heads[jobs[c[0]][0]][c[2]], kv[jobs[c[0]][1][c[1]]][0]) * zscale for c in chains}
        terms = {c: log_terms(z2[c], mask(c[0], c[1])) for c in chains}
        results = []
        for n, (sub, block_ids, _, car_in) in enumerate(jobs):
            per_head = []
            for hd in range(2):
                car = None if car_in is None else car_in[hd]
                acc = None
                for b in range(len(block_ids)):
                    log_beta, suffix, tot = terms[(n, b, hd)]
                    w = jnp.exp2(log_beta + suffix if car is None else log_beta + suffix + car)
                    if mask(n, b) is not None:
                        w = jnp.where(causal, w, 0.0)
                    pv = _dot(w.astype(BF16), kv[block_ids[b]][1])
                    acc = pv if acc is None else acc + pv
                    car = tot if car is None else car + tot
                per_head.append((acc, car))
            results.append(per_head)
        return results

    def first_visit(first_step):
        first = step * nsub
        key_blocks = [first + sub for sub in range(nsub)] + ([] if first_step else [first - 1])
        jobs = []
        for sub in range(nsub):
            ids = [sub]
            if sub > 0:
                ids.append(sub - 1)
            elif not first_step:
                ids.append(nsub)
            jobs.append((sub, ids, True, None))
        for sub, per_head in enumerate(visit(key_blocks, jobs)):
            for hd, (acc, car) in enumerate(per_head):
                acc_ref[sub, hd] = acc
                car_ref[sub, hd] = car

    pl.when(step == 0)(lambda: first_visit(True))
    pl.when(step > 0)(lambda: first_visit(False))

    def remaining_blocks():
        for sub in range(nsub):
            def mass_left(sub=sub):
                most = jnp.maximum(jnp.max(car_ref[sub, 0]), jnp.max(car_ref[sub, 1]))
                return (most > SB_SKIP_LOG2).astype(jnp.int32)

            def cond(c):
                j, go = c
                return jnp.logical_and(j >= 0, go > 0)

            def body(c, sub=sub, mass_left=mass_left):
                j, _ = c
                (per_head,) = visit([j], [(sub, [0], False, (car_ref[sub, 0], car_ref[sub, 1]))])
                for hd, (acc, car) in enumerate(per_head):
                    acc_ref[sub, hd] += acc
                    car_ref[sub, hd] = car
                return j - 1, mass_left()

            lax.while_loop(cond, body, (step * nsub + sub - 2, mass_left()))

    most = functools.reduce(jnp.maximum, [car_ref[sub, hd] for sub in range(nsub) for hd in range(2)])
    pl.when(jnp.max(most) > SB_SKIP_LOG2)(remaining_blocks)
    for sub in range(nsub):
        o_ref[0, sub * blk:(sub + 1) * blk, :] = jnp.where(lane < HEAD_DIM, acc_ref[sub, 0],
                                                           acc_ref[sub, 1]).astype(o_ref.dtype)


def _sb_attention(qkv, batch, seq, blk=SB_BLOCK, nsub=SB_SUBBLOCKS):
    qkv3 = qkv.reshape(batch, seq, 3 * BRANCH_WIDTH)
    pairs = BRANCH_WIDTH // LANES
    tq = blk * nsub
    r = np.arange(blk)
    m_ext = jnp.asarray(r[:, None] > r[None, :], BF16)
    out = pl.pallas_call(
        functools.partial(_sba_kernel, blk=blk, nsub=nsub),
        grid=(batch, pairs, seq // tq),
        in_specs=[pl.BlockSpec((1, tq, LANES), lambda b, p, i: (b, i, p)),
                  pl.BlockSpec((1, seq, LANES), lambda b, p, i: (b, 0, pairs + p)),
                  pl.BlockSpec((1, seq, LANES), lambda b, p, i: (b, 0, 2 * pairs + p)),
                  pl.BlockSpec(m_ext.shape, lambda b, p, i: (0, 0))],
        out_specs=pl.BlockSpec((1, tq, LANES), lambda b, p, i: (b, i, p)),
        out_shape=jax.ShapeDtypeStruct((batch, seq, BRANCH_WIDTH), BF16),
        scratch_shapes=[pltpu.VMEM((nsub, 2, blk, LANES), F32), pltpu.VMEM((nsub, 2, blk, 1), F32)],
        compiler_params=_cparams(("parallel", "parallel", "parallel")),
        name="sb_attention",
    )(qkv3, qkv3, qkv3, m_ext)
    return out.reshape(batch * seq, BRANCH_WIDTH)


def _s5_operators(lam_re, lam_im, log_dt, b_re, b_im, c_re, c_im, d_skip):
    L = S5_CHUNK
    g, p, c = lam_re.shape[0], S5_STATE, S5_GROUP_CH
    dt = jnp.exp(log_dt.astype(F32))[:, None]
    lr = lam_re.astype(F32)
    li = lam_im.astype(F32)

    def a_pow(n):
        n = jnp.asarray(n, F32)[..., None, None]
        mag = jnp.exp(lr * dt * n)
        return mag * jnp.cos(li * dt * n), mag * jnp.sin(li * dt * n)

    ar, ai = a_pow(1.0)
    den = lr * lr + li * li
    nr = ar - 1.0
    fr = (nr * lr + ai * li) / den
    fi = (ai * lr - nr * li) / den
    bbr = fr[..., None] * b_re - fi[..., None] * b_im
    bbi = fr[..., None] * b_im + fi[..., None] * b_re
    cr = c_re.astype(F32)
    ci = c_im.astype(F32)

    pr, pi = a_pow(np.arange(L))
    car = cr[None] * pr[:, :, None, :] - ci[None] * pi[:, :, None, :]
    cai = cr[None] * pi[:, :, None, :] + ci[None] * pr[:, :, None, :]
    klag = jnp.einsum('jgop,gpi->jgoi', car, bbr) - jnp.einsum('jgop,gpi->jgoi', cai, bbi)
    klag = klag.at[0].add(d_skip.astype(F32).reshape(g, c)[:, :, None] * jnp.eye(c, dtype=F32))
    kflat = klag.transpose(1, 3, 0, 2).reshape(g, c, L * c)
    toe = jnp.stack([jnp.pad(kflat[:, :, :(L - s) * c], ((0, 0), (0, 0), (s * c, 0))) for s in range(L)], axis=1)
    toe = toe.reshape(g, L * c, L * c)

    qr, qi = a_pow(L - 1 - np.arange(L))
    p_re = qr[:, :, :, None] * bbr[None] - qi[:, :, :, None] * bbi[None]
    p_im = qr[:, :, :, None] * bbi[None] + qi[:, :, :, None] * bbr[None]
    p_re = p_re.transpose(1, 0, 3, 2).reshape(g, L * c, p)
    p_im = p_im.transpose(1, 0, 3, 2).reshape(g, L * c, p)

    er, ei = a_pow(1 + np.arange(L))
    q_re = (cr[None] * er[:, :, None, :] - ci[None] * ei[:, :, None, :])
    q_im = -(cr[None] * ei[:, :, None, :] + ci[None] * er[:, :, None, :])
    q_re = q_re.transpose(1, 3, 0, 2).reshape(g, p, L * c)
    q_im = q_im.transpose(1, 3, 0, 2).reshape(g, p, L * c)

    a_lr, a_li = a_pow(float(L))

    def pair_diag(m):
        gg, r, cc = m.shape
        m = m.reshape(gg // 2, 2, r, cc)
        z = jnp.zeros_like(m[:, 0])
        top = jnp.concatenate([m[:, 0], z], axis=2)
        bot = jnp.concatenate([z, m[:, 1]], axis=2)
        return jnp.concatenate([top, bot], axis=1)

    toe_p = pair_diag(toe).astype(BF16)
    p_p = jnp.concatenate([pair_diag(p_re), pair_diag(p_im)], axis=2).astype(BF16)
    q_p = jnp.concatenate([pair_diag(q_re), pair_diag(q_im)], axis=1).astype(BF16)
    a_re = a_lr.reshape(1, g * p)
    a_im = a_li.reshape(1, g * p)
    return toe_p, p_p, q_p, a_re, a_im


def _group_block_transpose(tiles):
    group = lax.broadcasted_iota(jnp.int32, tiles[0].shape, 1) // S5_GROUP_CH
    n = len(tiles)
    k = n // 2
    while k >= 1:
        upper = (group & k) != 0
        nxt = list(tiles)
        for i in range(n):
            if i & k == 0:
                lo, hi = tiles[i], tiles[i + k]
                nxt[i] = jnp.where(upper, pltpu.roll(hi, k * S5_GROUP_CH, 1), lo)
                nxt[i + k] = jnp.where(upper, hi, pltpu.roll(lo, LANES - k * S5_GROUP_CH, 1))
        tiles = nxt
        k //= 2
    return tiles


def _s5_state_in_kernel(u_ref, p_ref, up_ref, vre_ref, vim_ref, *, nb):
    L = S5_CHUNK
    per = LANES // S5_GROUP_CH
    for kb in range(u_ref.shape[0]):
        halves = []
        for a in range(L // per):
            steps = [u_ref[kb, pl.ds(a * per + b, nb, stride=L), :] for b in range(per)]
            halves.append(_group_block_transpose(steps))
        for pi in range(per // 2):
            pair = kb * (per // 2) + pi
            row = jnp.concatenate([halves[a][2 * pi + gi] for gi in range(2) for a in range(L // per)], axis=1)
            row = row.astype(BF16)
            up_ref[pair] = row
            v = _dot(row, p_ref[pair])
            vre_ref[:, pair * LANES:(pair + 1) * LANES] = v[:, :LANES]
            vim_ref[:, pair * LANES:(pair + 1) * LANES] = v[:, LANES:]


def _s5_out_kernel(up_ref, sre_ref, sim_ref, toe_ref, q_ref, y_ref, *, nb):
    L = S5_CHUNK
    per = LANES // S5_GROUP_CH
    for kb in range(y_ref.shape[0]):
        halves = [[None] * per for _ in range(L // per)]
        for pi in range(per // 2):
            pair = kb * (per // 2) + pi
            q = q_ref[pair]
            cols = slice(pair * LANES, (pair + 1) * LANES)
            y = _dot(up_ref[pair], toe_ref[pair])
            y += _dot(sre_ref[:, cols].astype(BF16), q[:LANES]) + _dot(sim_ref[:, cols].astype(BF16), q[LANES:])
            y = jax.nn.gelu(y)
            for gi in range(2):
                for a in range(L // per):
                    j = gi * (L // per) + a
                    halves[a][2 * pi + gi] = y[:, j * LANES:(j + 1) * LANES]
        for a in range(L // per):
            steps = _group_block_transpose(halves[a])
            for b in range(per):
                y_ref[kb, pl.ds(a * per + b, nb, stride=L), :] = steps[b]


def _s5_kernel(u_ref, p_ref, toe_ref, q_ref, are_ref, aim_ref, y_ref, up_ref, vre_ref, vim_ref, sre_ref, sim_ref,
               x_ref, *, nb, steps_per_seq):
    @pl.when(pl.program_id(0) % steps_per_seq == 0)
    def _():
        x_ref[...] = jnp.zeros_like(x_ref)

    _s5_state_in_kernel(u_ref, p_ref, up_ref, vre_ref, vim_ref, nb=nb)
    ar = are_ref[...]
    ai = aim_ref[...]

    def step(r, carry):
        xr, xi = carry
        sre_ref[pl.ds(r, 1), :] = xr
        sim_ref[pl.ds(r, 1), :] = xi
        vr = vre_ref[pl.ds(r, 1), :]
        vi = vim_ref[pl.ds(r, 1), :]
        return ar * xr - ai * xi + vr, ar * xi + ai * xr + vi

    xr, xi = lax.fori_loop(0, nb, step, (x_ref[0:1, :], x_ref[1:2, :]))
    x_ref[0:1, :] = xr
    x_ref[1:2, :] = xi
    _s5_out_kernel(up_ref, sre_ref, sim_ref, toe_ref, q_ref, y_ref, nb=nb)


def _s5_ssm(u, batch, seq, ops, layer, nb=128):
    toe_p, p_p, q_p, a_re, a_im = ops
    nslab, t, _ = u.shape
    L = S5_CHUNK
    npair = S5_GROUPS // 2
    nc = t // L
    row = 2 * L * S5_GROUP_CH
    ncols = S5_GROUPS * S5_STATE
    nb = min(nb, seq // L)

    slab_spec = pl.BlockSpec((nslab, nb * L, LANES), lambda i: (0, i, 0))
    return pl.pallas_call(
        functools.partial(_s5_kernel, nb=nb, steps_per_seq=seq // (L * nb)),
        grid=(nc // nb,),
        in_specs=[slab_spec] + [_layer_block(a, layer) for a in (p_p, toe_p, q_p, a_re, a_im)],
        out_specs=slab_spec,
        out_shape=jax.ShapeDtypeStruct((nslab, t, LANES), F32),
        scratch_shapes=[pltpu.VMEM((npair, nb, row), BF16)] + [pltpu.VMEM((nb, ncols), F32)] * 4
        + [pltpu.VMEM((8, ncols), F32)],
        compiler_params=_cparams(("arbitrary",)),
        name="s5_ssm",
    )(u, p_p, toe_p, q_p, a_re, a_im)


def _ret_tables(seq):
    half = HEAD_DIM // 2
    pos = jnp.arange(seq, dtype=F32)
    inv = ROPE_BASE ** (-jnp.arange(half, dtype=F32) / half)
    ang = pos[:, None] * inv[None, :]
    cos = jnp.tile(jnp.cos(ang), (1, LANES // half))
    sin = jnp.sin(ang)
    sin = jnp.tile(jnp.concatenate([-sin, sin], axis=1), (1, LANES // HEAD_DIM))
    c = RET_CHUNK
    gamma = 1.0 - 2.0 ** (-5.0 - jnp.arange(RET_HEADS, dtype=F32))
    log_g = jnp.log(gamma)
    idx = jnp.arange(c, dtype=F32)
    diff = idx[:, None] - idx[None, :]
    intra = jnp.where(diff >= 0, jnp.exp(jnp.maximum(diff, 0.0)[None] * log_g[:, None, None]), 0.0)
    q_dec = jnp.exp((idx + 1.0)[None, :] * log_g[:, None])
    k_dec = jnp.exp((c - 1.0 - idx)[None, :] * log_g[:, None])
    ch_dec = jnp.exp(c * log_g)

    def lanes(x):
        x = x.reshape(RET_HEADS // 2, 2, c)
        return jnp.repeat(x.transpose(0, 2, 1), HEAD_DIM, axis=2)

    q_dec_l = lanes(q_dec)
    k_dec_l = lanes(k_dec)
    hp = RET_HEADS // 2
    blockdiag = np.kron(np.eye(2), np.ones((HEAD_DIM, HEAD_DIM)))
    st_dec = jnp.repeat(ch_dec.reshape(hp, 2), HEAD_DIM, axis=1)[:, :, None] * blockdiag[None]
    avg = jnp.asarray(blockdiag / HEAD_DIM, BF16)
    return cos, sin, intra, q_dec_l, k_dec_l, st_dec.astype(F32), avg


def _ret_kernel(q_ref, k_ref, v_ref, g_ref, cos_ref, sin_ref, intra_ref, qd_ref, kd_ref, sd_ref, avg_ref, nw_ref,
                o_ref, st_ref, *, nchunk):
    c = RET_CHUNK

    @pl.when(pl.program_id(2) == 0)
    def _():
        st_ref[...] = jnp.zeros_like(st_ref)

    tq = q_ref.shape[1]
    lane = lax.broadcasted_iota(jnp.int32, (tq, LANES), 1)
    first_half = (lane % HEAD_DIM) < (HEAD_DIM // 2)
    head0 = lax.broadcasted_iota(jnp.int32, (c, LANES), 1) < HEAD_DIM
    sd = sd_ref[0]
    avg = avg_ref[...]
    same_head = avg > 0
    cos = cos_ref[...]
    sin = sin_ref[...]

    def rotary(x):
        swapped = jnp.where(first_half, pltpu.roll(x, LANES - HEAD_DIM // 2, 1), pltpu.roll(x, HEAD_DIM // 2, 1))
        return x * cos + swapped * sin

    q_all = rotary(q_ref[0].astype(F32)).astype(BF16)
    k_all = rotary(k_ref[0].astype(F32)) * (HEAD_DIM ** -0.5)
    kb_all = k_all.astype(BF16)
    zero = jnp.zeros((c, LANES), BF16)
    chunks = [slice(ci * c, (ci + 1) * c) for ci in range(nchunk)]
    vs = [v_ref[0, rows, :] for rows in chunks]
    scores, incs = [], []
    for ci, rows in enumerate(chunks):
        qb = q_all[rows]
        q2 = jnp.concatenate([jnp.where(head0, qb, zero), jnp.where(head0, zero, qb)], axis=0)
        scores.append(_dot_nt(q2, kb_all[rows]))
        kd_t = (k_all[rows] * kd_ref[0]).T.astype(BF16)
        incs.append(_dot(kd_t, vs[ci]))
    state = st_ref[...]
    states = []
    for inc in incs:
        states.append(state.astype(BF16))
        state = state * sd + jnp.where(same_head, inc, 0.0)
    st_ref[...] = state
    intra2 = jnp.concatenate([intra_ref[0, 0], intra_ref[0, 1]], axis=0)
    outs = []
    for ci, rows in enumerate(chunks):
        o2 = _dot((scores[ci] * intra2).astype(BF16), vs[ci])
        outs.append(jnp.where(head0, o2[:c], o2[c:]) + _dot(q_all[rows], states[ci]) * qd_ref[0])
    out = jnp.concatenate(outs, axis=0)
    hi, lo = _split_bf16(out)
    mu = _dot(hi, avg) + _dot(lo, avg)
    dlt = out - mu
    hi, lo = _split_bf16(dlt * dlt)
    var = _dot(hi, avg) + _dot(lo, avg)
    gate = g_ref[0].astype(F32)
    y = dlt * lax.rsqrt(var + EPS) * nw_ref[...] * (gate * jax.nn.sigmoid(gate))
    o_ref[0] = y.astype(o_ref.dtype)


def _retention(ret, norm_w, batch, seq, tables, tq=2048):
    cos, sin, intra, q_dec_l, k_dec_l, st_dec, avg = tables
    pairs = BRANCH_WIDTH // LANES
    ret3 = ret.reshape(batch, seq, 4 * BRANCH_WIDTH)
    tq = min(tq, seq)
    c = RET_CHUNK

    def col(k):
        return pl.BlockSpec((1, tq, LANES), lambda b, p, i: (b, i, k * pairs + p))

    out = pl.pallas_call(
        functools.partial(_ret_kernel, nchunk=tq // c),
        grid=(batch, pairs, seq // tq),
        in_specs=[col(0), col(1), col(2), col(3),
                  pl.BlockSpec((tq, LANES), lambda b, p, i: (i, 0)),
                  pl.BlockSpec((tq, LANES), lambda b, p, i: (i, 0)),
                  pl.BlockSpec((1, 2, c, c), lambda b, p, i: (p, 0, 0, 0)),
                  pl.BlockSpec((1, c, LANES), lambda b, p, i: (p, 0, 0)),
                  pl.BlockSpec((1, c, LANES), lambda b, p, i: (p, 0, 0)),
                  pl.BlockSpec((1, LANES, LANES), lambda b, p, i: (p, 0, 0)),
                  pl.BlockSpec((LANES, LANES), lambda b, p, i: (0, 0)),
                  pl.BlockSpec((1, LANES), lambda b, p, i: (0, p))],
        out_specs=pl.BlockSpec((1, tq, LANES), lambda b, p, i: (b, i, p)),
        out_shape=jax.ShapeDtypeStruct((batch, seq, BRANCH_WIDTH), BF16),
        scratch_shapes=[pltpu.VMEM((LANES, LANES), F32)],
        compiler_params=_cparams(("parallel", "parallel", "arbitrary")),
        name="retention",
    )(ret3, ret3, ret3, ret3, cos, sin, intra.reshape(pairs, 2, c, c), q_dec_l, k_dec_l, st_dec, avg,
      norm_w.reshape(1, BRANCH_WIDTH).astype(F32))
    return out.reshape(batch * seq, BRANCH_WIDTH)


def _route(logits):
    lane = lax.broadcasted_iota(jnp.int32, logits.shape, 1).astype(F32)
    neg = jnp.float32(-jnp.inf)
    big = jnp.float32(1 << 20)
    gl = jnp.where(lane < N_GROUPS, logits, neg)
    gmax = jnp.max(gl, axis=1, keepdims=True)
    gsum = jnp.sum(jnp.exp(gl - gmax), axis=1, keepdims=True)
    g_w = 1.0 / gsum
    g_idx = jnp.min(jnp.where(gl == gmax, lane, big), axis=1, keepdims=True)
    lo = ROUTER_LANE0 + EXPERTS_PER_GROUP * g_idx
    sl = jnp.where((lane >= lo) & (lane < lo + EXPERTS_PER_GROUP), logits, neg)
    m1 = jnp.max(sl, axis=1, keepdims=True)
    i1 = jnp.min(jnp.where(sl == m1, lane, big), axis=1, keepdims=True)
    sl2 = jnp.where(lane == i1, neg, sl)
    m2 = jnp.max(sl2, axis=1, keepdims=True)
    i2 = jnp.min(jnp.where(sl2 == m2, lane, big), axis=1, keepdims=True)
    e2 = jnp.exp(m2 - m1)
    w1 = g_w / (1.0 + e2)
    w2 = g_w * e2 / (1.0 + e2)
    packed = jnp.where(lane == 0.0, i1 - ROUTER_LANE0, jnp.where(lane == 1.0, i2 - ROUTER_LANE0, 0.0))
    return packed + jnp.where(lane == 2.0, w1, 0.0) + jnp.where(lane == 3.0, w2, 0.0)


def _merge_kernel(ysb_ref, ys5_ref, yret_ref, gate_ref, h_ref, wglu_ref, wbr_ref, wout_ref, n2_ref, wr_hi_ref,
                  wr_lo_ref, br_ref, tri_ref, cum_ref, h_out_ref, hn_ref, route_ref, pos_ref, cnt_ref):
    ys5 = jnp.concatenate([ys5_ref[s] for s in range(ys5_ref.shape[0])], axis=1).astype(BF16)
    glu = _dot(ys5, wglu_ref[...])
    y_s5 = (glu[:, :BRANCH_WIDTH] * jax.nn.sigmoid(glu[:, BRANCH_WIDTH:])).astype(BF16)
    merged = None
    for n, y in enumerate((ysb_ref[...], y_s5, yret_ref[...])):
        g = jax.nn.sigmoid(gate_ref[:, n * D_MODEL:(n + 1) * D_MODEL].astype(F32))
        term = g * _dot(y, wbr_ref[n])
        merged = term if merged is None else merged + term
    h = h_ref[...] + _dot(merged.astype(BF16), wout_ref[...])
    h_out_ref[...] = h
    ms = jnp.mean(h * h, axis=-1, keepdims=True)
    hn = h * lax.rsqrt(ms + EPS) * n2_ref[...]
    hn_ref[...] = hn.astype(BF16)
    hi, lo = _split_bf16(hn)
    logits = _dot(hi, wr_hi_ref[...]) + _dot(hi, wr_lo_ref[...]) + _dot(lo, wr_hi_ref[...]) + br_ref[...]
    route = _route(logits)
    route_ref[...] = route
    pos_ref[...], cnt_ref[0] = _moe_plan(route, tri_ref[...], cum_ref[...])


def _merge(y_sb, y_s5, y_ret, gates, h, w_glu, w_br, w_out, norm2_w, wr_hi, wr_lo, b_r, layer):
    t = h.shape[0]
    tm = min(MOE_TILE, t)
    r = np.arange(tm)
    tri = jnp.asarray(r[None, :] <= r[:, None], BF16)
    l = np.arange(LANES)
    cum = jnp.asarray(l[:, None] < l[None, :], BF16)

    def rows(w):
        return pl.BlockSpec((tm, w), lambda i: (i, 0))

    return pl.pallas_call(
        _merge_kernel,
        grid=(t // tm,),
        in_specs=[rows(BRANCH_WIDTH), pl.BlockSpec((y_s5.shape[0], tm, LANES), lambda i: (0, i, 0)),
                  rows(BRANCH_WIDTH), rows(N_BRANCH * D_MODEL),
                  rows(D_MODEL)] + [_layer_block(a, layer) for a in (w_glu, w_br, w_out, norm2_w, wr_hi, wr_lo, b_r)]
        + [pl.BlockSpec(tri.shape, lambda i: (0, 0), pipeline_mode=pl.Buffered(1)),
           pl.BlockSpec(cum.shape, lambda i: (0, 0), pipeline_mode=pl.Buffered(1))],
        out_specs=[rows(D_MODEL), rows(D_MODEL), rows(LANES), rows(LANES), pl.BlockSpec((1, 8, LANES), lambda i: (i, 0, 0))],
        out_shape=[jax.ShapeDtypeStruct((t, D_MODEL), F32), jax.ShapeDtypeStruct((t, D_MODEL), BF16),
                   jax.ShapeDtypeStruct((t, LANES), F32), jax.ShapeDtypeStruct((t, LANES), F32),
                   jax.ShapeDtypeStruct((t // tm, 8, LANES), F32)],
        compiler_params=_cparams(("parallel",)),
        name="merge_route",
    )(y_sb, y_s5, y_ret, gates, h, w_glu, w_br, w_out, norm2_w, wr_hi, wr_lo, b_r, tri, cum)


MOE_TILE = 512
MOE_ROW_TILE = 512
SEG_ALIGN = 16
MOE_SLOTS = TOP_K * MOE_TILE + N_EXPERTS * SEG_ALIGN
MOE_CHUNKS = MOE_SLOTS // SEG_ALIGN
MOE_SLOT_BLOCK = 256


def _moe_plan(route, tri, cum):
    tm = route.shape[0]
    lane = lax.broadcasted_iota(jnp.int32, route.shape, 1).astype(F32)
    e1 = route[:, 0:1]
    e2 = route[:, 1:2]
    chosen = (lane == e1) | (lane == e2)
    onehot = jnp.where(chosen, 1.0, 0.0).astype(BF16)
    incl = _dot(tri, onehot)
    cnt = incl[tm - 1:tm, :]
    padded = jnp.floor((cnt + (SEG_ALIGN - 1)) * (1.0 / SEG_ALIGN)) * SEG_ALIGN
    start = _dot(jnp.broadcast_to(padded, (8, LANES)).astype(BF16), cum)[0:1, :]
    slot = start + incl - 1.0
    p1 = jnp.sum(jnp.where(lane == e1, slot, 0.0), axis=1, keepdims=True)
    p2 = jnp.sum(jnp.where(lane == e2, slot, 0.0), axis=1, keepdims=True)
    pos = jnp.where(lane == 0.0, p1, jnp.where(lane == 1.0, p2, 0.0))
    return pos, jnp.broadcast_to(cnt, (8, LANES))


def _moe_tables(cnt, rt, n_rows):
    a = SEG_ALIGN
    ntile = cnt.shape[0]
    p = (cnt + a - 1) // a * a
    lend = jnp.cumsum(p, axis=1)
    lstart = lend - p
    seg = p.sum(axis=0)
    segpad = (seg + rt - 1) // rt * rt
    gend = jnp.cumsum(segpad)
    gstart = gend - segpad
    toff = gstart[None, :] + jnp.cumsum(p, axis=0) - p
    cstart = a * jnp.arange(MOE_CHUNKS, dtype=jnp.int32)
    in_run = (cstart[None, :, None] >= lstart[:, None, :]) & (cstart[None, :, None] < lend[:, None, :])
    gdst = (jnp.sum(jnp.where(in_run, (toff - lstart)[:, None, :], 0), axis=2) + cstart[None, :]) // a
    nch = lend[:, -1] // a
    tstart = rt * jnp.arange(n_rows // rt, dtype=jnp.int32)
    tile_e = jnp.minimum(jnp.sum(tstart[:, None] >= gend[None, :], axis=1), N_EXPERTS - 1)
    n_used = gend[-1] // rt
    zmax = rt // a - 1
    zper = -(-N_EXPERTS * zmax // ntile)
    k = jnp.arange(zmax, dtype=jnp.int32)
    zd = ((gstart + seg) // a)[:, None] + k[None, :]
    valid = (k[None, :] < ((segpad - seg) // a)[:, None]).reshape(-1)
    order = jnp.argsort(jnp.logical_not(valid), stable=True)
    zdst = jnp.pad(zd.reshape(-1)[order], (0, ntile * zper - N_EXPERTS * zmax))
    nz = valid.sum()
    i32 = jnp.int32
    return (gdst.reshape(-1).astype(i32), nch.astype(i32), zdst.astype(i32), nz.reshape(1).astype(i32),
            tile_e.astype(i32), n_used.reshape(1).astype(i32)), zper


def _moe_dispatch_kernel(gdst_ref, nch_ref, zdst_ref, nz_ref, n_used_ref, pos_ref, hn_ref, xs_hbm, stg, zblk, sem,
                         zsem, *, zper):
    i = pl.program_id(0)
    last_step = pl.num_programs(0) - 1
    buf = i % 2
    tm = hn_ref.shape[0]
    pos_t = pos_ref[...].T
    p1 = pos_t[0:1, :]
    p2 = pos_t[1:2, :]
    x = hn_ref[...]
    for lo in range(0, MOE_SLOTS, tm):
        slot = (lax.broadcasted_iota(jnp.int32, (tm, tm), 0) + lo).astype(F32)
        onehot = jnp.where((slot == p1) | (slot == p2), 1.0, 0.0).astype(BF16)
        stg[buf, lo:lo + tm, :] = _dot(onehot, x).astype(BF16)

    def chunk_copy(which, src_chunk, dst_chunk):
        src = pl.multiple_of(src_chunk * SEG_ALIGN, SEG_ALIGN)
        dst = pl.multiple_of(dst_chunk * SEG_ALIGN, SEG_ALIGN)
        return pltpu.make_async_copy(stg.at[which, pl.ds(src, SEG_ALIGN)], xs_hbm.at[pl.ds(dst, SEG_ALIGN)],
                                     sem.at[which])

    def n_copies(step):
        return nch_ref[step], jnp.clip(nz_ref[0] - step * zper, 0, zper)

    def drain(which, step):
        for count in n_copies(step):
            rows = count * SEG_ALIGN

            @pl.when(rows > 0)
            def _():
                pltpu.make_async_copy(stg.at[which, pl.ds(0, rows)], xs_hbm.at[pl.ds(0, rows)],
                                      sem.at[which]).wait()

    n, nzero = n_copies(i)

    @pl.loop(0, n)
    def _(c):
        chunk_copy(buf, c, gdst_ref[i * MOE_CHUNKS + c]).start()

    @pl.loop(0, nzero)
    def _(k):
        chunk_copy(buf, MOE_CHUNKS - 1, zdst_ref[i * zper + k]).start()

    pl.when(i > 0)(lambda: drain(1 - buf, i - 1))

    @pl.when(i == last_step)
    def _():
        drain(buf, i)
        rt = zblk.shape[0]
        zblk[...] = jnp.zeros_like(zblk)
        first = n_used_ref[0]
        last = xs_hbm.shape[0] // rt

        def tile_copy(j):
            return pltpu.make_async_copy(zblk, xs_hbm.at[pl.ds(pl.multiple_of(j * rt, rt), rt)], zsem)

        @pl.loop(first, last)
        def _(j):
            tile_copy(j).start()

        @pl.loop(first, last)
        def _(j):
            tile_copy(0).wait()


def _moe_expert_kernel(tile_e_ref, n_used_ref, xs_ref, wg_ref, wu_ref, wd_ref, ys_ref):
    used = pl.program_id(0) < n_used_ref[0]

    @pl.when(used)
    def _():
        x = xs_ref[...]
        gate = _dot(x, wg_ref[0].astype(BF16))
        up = _dot(x, wu_ref[0].astype(BF16))
        hdn = gate * jax.nn.sigmoid(gate) * up
        ys_ref[...] = _dot(hdn.astype(BF16), wd_ref[0].astype(BF16)).astype(ys_ref.dtype)

    @pl.when(jnp.logical_not(used))
    def _():
        ys_ref[...] = jnp.zeros_like(ys_ref)


def _moe_combine_kernel(gdst_ref, nch_ref, route_ref, pos_ref, h_ref, nw_ref, ys_hbm, o_ref, stg, sem, *, out_norm):
    i = pl.program_id(0)
    nsteps = pl.num_programs(0)
    buf = i % 2
    tm = h_ref.shape[0]

    def chunk_copy(which, src_chunk, dst_chunk):
        src = pl.multiple_of(src_chunk * SEG_ALIGN, SEG_ALIGN)
        dst = pl.multiple_of(dst_chunk * SEG_ALIGN, SEG_ALIGN)
        return pltpu.make_async_copy(ys_hbm.at[pl.ds(src, SEG_ALIGN)], stg.at[which, pl.ds(dst, SEG_ALIGN)],
                                     sem.at[which])

    def fetch(step, which):
        n = nch_ref[step]

        @pl.loop(0, n)
        def _(c):
            chunk_copy(which, gdst_ref[step * MOE_CHUNKS + c], c).start()

        per = MOE_SLOT_BLOCK // SEG_ALIGN
        stop = jnp.minimum((n + per - 1) // per * per, MOE_CHUNKS)

        @pl.loop(n, stop)
        def _(c):
            rows = pl.ds(pl.multiple_of(c * SEG_ALIGN, SEG_ALIGN), SEG_ALIGN)
            stg[which, rows, :] = jnp.zeros((SEG_ALIGN, D_MODEL), BF16)

    pl.when(i == 0)(lambda: fetch(0, 0))
    pl.when(i + 1 < nsteps)(lambda: fetch(i + 1, 1 - buf))

    r = route_ref[...]
    pos = pos_ref[...]

    def scatter(lo, hi):
        slot = (lax.broadcasted_iota(jnp.int32, (tm, hi - lo), 1) + lo).astype(F32)
        w = jnp.where(slot == pos[:, 0:1], r[:, 2:3], 0.0) + jnp.where(slot == pos[:, 1:2], r[:, 3:4], 0.0)
        return _dot(w.astype(BF16), stg[buf, lo:hi, :])

    rows = nch_ref[i] * SEG_ALIGN

    @pl.when(rows > 0)
    def _():
        pltpu.make_async_copy(ys_hbm.at[pl.ds(0, rows)], stg.at[buf, pl.ds(0, rows)], sem.at[buf]).wait()

    always = TOP_K * tm
    o_ref[...] = h_ref[...] + scatter(0, always)
    for lo in range(always, MOE_SLOTS, MOE_SLOT_BLOCK):
        @pl.when(lo < rows)
        def _(lo=lo):
            o_ref[...] += scatter(lo, lo + MOE_SLOT_BLOCK)
    if out_norm:
        out = o_ref[...]
        ms = jnp.mean(out * out, axis=-1, keepdims=True)
        o_ref[...] = out * lax.rsqrt(ms + EPS) * nw_ref[...]


def _moe(hn, route, pos, cnt, h, w_gate, w_up, w_down, layer, out_norm_w, out_norm):
    t = h.shape[0]
    tm = min(MOE_TILE, t)
    rt = MOE_ROW_TILE
    ntile = t // tm
    n_rows = TOP_K * t + ntile * N_EXPERTS * (SEG_ALIGN - 1) + N_EXPERTS * (rt - SEG_ALIGN)
    n_rows = (n_rows + rt - 1) // rt * rt
    (gdst, nch, zdst, nz, tile_e, n_used), zper = _moe_tables(cnt[:, 0, :N_EXPERTS].astype(jnp.int32), rt, n_rows)

    def expert_block(rows, cols):
        return pl.BlockSpec((None, 1, rows, cols), lambda j, te, nu: (layer, te[j], 0, 0))

    xs = pl.pallas_call(
        functools.partial(_moe_dispatch_kernel, zper=zper),
        grid_spec=pltpu.PrefetchScalarGridSpec(
            num_scalar_prefetch=5,
            grid=(ntile,),
            in_specs=[pl.BlockSpec((tm, LANES), lambda i, *_: (i, 0)),
                      pl.BlockSpec((tm, D_MODEL), lambda i, *_: (i, 0))],
            out_specs=pl.BlockSpec(memory_space=pl.ANY),
            scratch_shapes=[pltpu.VMEM((2, MOE_SLOTS, D_MODEL), BF16), pltpu.VMEM((rt, D_MODEL), BF16),
                            pltpu.SemaphoreType.DMA((2,)), pltpu.SemaphoreType.DMA(())]),
        out_shape=jax.ShapeDtypeStruct((n_rows, D_MODEL), BF16),
        compiler_params=_cparams(("arbitrary",)),
        name="moe_dispatch",
    )(gdst, nch, zdst, nz, n_used, pos, hn)
    ys = pl.pallas_call(
        _moe_expert_kernel,
        grid_spec=pltpu.PrefetchScalarGridSpec(
            num_scalar_prefetch=2,
            grid=(n_rows // rt,),
            in_specs=[pl.BlockSpec((rt, D_MODEL), lambda j, te, nu: (jnp.minimum(j, nu[0] - 1), 0)),
                      expert_block(D_MODEL, EXPERT_FF), expert_block(D_MODEL, EXPERT_FF),
                      expert_block(EXPERT_FF, D_MODEL)],
            out_specs=pl.BlockSpec((rt, D_MODEL), lambda j, te, nu: (j, 0))),
        out_shape=jax.ShapeDtypeStruct((n_rows, D_MODEL), BF16),
        compiler_params=_cparams(("arbitrary",)),
        name="moe_experts",
    )(tile_e, n_used, xs, w_gate, w_up, w_down)
    return pl.pallas_call(
        functools.partial(_moe_combine_kernel, out_norm=out_norm),
        grid_spec=pltpu.PrefetchScalarGridSpec(
            num_scalar_prefetch=2,
            grid=(ntile,),
            in_specs=[pl.BlockSpec((tm, LANES), lambda i, *_: (i, 0)),
                      pl.BlockSpec((tm, LANES), lambda i, *_: (i, 0)),
                      pl.BlockSpec((tm, D_MODEL), lambda i, *_: (i, 0)),
                      pl.BlockSpec((1, D_MODEL), lambda i, *_: (0, 0)),
                      pl.BlockSpec(memory_space=pl.ANY)],
            out_specs=pl.BlockSpec((tm, D_MODEL), lambda i, *_: (i, 0)),
            scratch_shapes=[pltpu.VMEM((2, MOE_SLOTS, D_MODEL), BF16), pltpu.SemaphoreType.DMA((2,))]),
        out_shape=jax.ShapeDtypeStruct((t, D_MODEL), F32),
        compiler_params=_cparams(("arbitrary",)),
        name="moe_combine",
    )(gdst, nch, route, pos, h, out_norm_w, ys)


def kernel(x, norm1_w, w_in, s5_lambda_re, s5_lambda_im, s5_log_dt, s5_b_re, s5_b_im, s5_c_re, s5_c_im, s5_d,
           s5_w_glu, ret_norm_w, w_branch, w_out, norm2_w, w_group, b_group, w_router, b_router, w_gate, w_up,
           w_down, final_norm_w):
    batch, seq, _ = x.shape
    depth = w_in.shape[0]
    t = batch * seq
    h = x.reshape(t, D_MODEL).astype(F32)
    ret_tables = _ret_tables(seq)
    w_in, s5_w_glu, w_branch, w_out = (w.astype(BF16) for w in (w_in, s5_w_glu, w_branch, w_out))
    norm1_w = norm1_w.reshape(depth, 1, D_MODEL).astype(F32)
    norm2_w = norm2_w.reshape(depth, 1, D_MODEL).astype(F32)
    s5_params = (s5_lambda_re, s5_lambda_im, s5_log_dt, s5_b_re, s5_b_im, s5_c_re, s5_c_im)
    s5_flat = [a.reshape((depth * S5_GROUPS,) + a.shape[2:]) for a in s5_params] + [s5_d.reshape(-1)]
    toe_p, p_p, q_p, a_re, a_im = _s5_operators(*s5_flat)
    s5_ops = [op.reshape((depth, -1) + op.shape[1:]) for op in (toe_p, p_p, q_p)]
    s5_ops += [a.reshape(depth, 1, -1) for a in (a_re, a_im)]
    w_r = jnp.concatenate([w_group, w_router.transpose(0, 2, 1, 3).reshape(depth, D_MODEL, N_EXPERTS)], axis=2)
    w_r = jnp.pad(w_r.astype(F32), ((0, 0), (0, 0), (0, LANES - w_r.shape[2])))
    wr_hi, wr_lo = _split_bf16(w_r)
    b_r = jnp.pad(jnp.concatenate([b_group, b_router.reshape(depth, -1)], axis=1).astype(F32),
                  ((0, 0), (0, LANES - N_GROUPS - N_EXPERTS))).reshape(depth, 1, LANES)
    for layer in range(depth):
        qkv, u_s5, ret, gates = _inproj(h, norm1_w, w_in, layer)
        y_sb = _sb_attention(qkv, batch, seq)
        y_s5 = _s5_ssm(u_s5, batch, seq, s5_ops, layer)
        y_ret = _retention(ret, ret_norm_w[layer], batch, seq, ret_tables)
        h, hn, route, pos, cnt = _merge(y_sb, y_s5, y_ret, gates, h, s5_w_glu, w_branch, w_out, norm2_w, wr_hi, wr_lo, b_r,
                              layer)
        h = _moe(hn, route, pos, cnt, h, w_gate, w_up, w_down, layer, final_norm_w.reshape(1, D_MODEL).astype(F32),
                 out_norm=(layer == depth - 1))
    return h.reshape(batch, seq, D_MODEL).astype(x.dtype)
```

```python
import functools
import math

import jax
import jax.numpy as jnp
import numpy as np
from jax import lax
from jax.experimental import pallas as pl
from jax.experimental.pallas import tpu as pltpu

F32 = jnp.float32
BF16 = jnp.bfloat16

D_MODEL = 1024
SB_HEADS = 8
HEAD_DIM = 64
BRANCH_WIDTH = 512
S5_GROUPS = 32
S5_GROUP_CH = 16
S5_STATE = 64
RET_HEADS = 8
RET_CHUNK = 128
ROPE_BASE = 10000.0
N_BRANCH = 3
N_GROUPS = 4
EXPERTS_PER_GROUP = 8
N_EXPERTS = 32
TOP_K = 2
EXPERT_FF = 256
EPS = 1e-6

LANES = 128
VMEM_LIMIT = 56 * 1024 * 1024
S5_CHUNK = 16
SB_BLOCK = 256
SB_SUBBLOCKS = 8
LOG2_E = 1.4426950408889634
SB_SKIP_LOG2 = -152.0
ROUTER_LANE0 = N_GROUPS


def _cparams(sem):
    return pltpu.CompilerParams(dimension_semantics=sem, vmem_limit_bytes=VMEM_LIMIT)


def _layer_block(stacked, layer):
    zeros = (0,) * (stacked.ndim - 1)
    return pl.BlockSpec((None,) + stacked.shape[1:], lambda *_: (layer,) + zeros, pipeline_mode=pl.Buffered(1))


def _split_bf16(x):
    hi = x.astype(BF16)
    lo = (x - hi.astype(F32)).astype(BF16)
    return hi, lo


def _dot(a, b):
    return jnp.dot(a, b, preferred_element_type=F32)


def _dot_nt(a, b):
    return lax.dot_general(a, b, (((1,), (1,)), ((), ())), preferred_element_type=F32)


def _inproj_kernel(h_ref, nw_ref, w_ref, qkv_ref, u_ref, ret_ref, gate_ref):
    x = h_ref[...]
    ms = jnp.mean(x * x, axis=-1, keepdims=True)
    xn = (x * lax.rsqrt(ms + EPS) * nw_ref[...]).astype(BF16)
    off = 0
    for ref in (qkv_ref, u_ref, ret_ref, gate_ref):
        if ref is u_ref:
            u = _dot(xn, w_ref[:, off:off + BRANCH_WIDTH])
            for s in range(u_ref.shape[0]):
                u_ref[s] = u[:, s * LANES:(s + 1) * LANES]
            off += BRANCH_WIDTH
            continue
        width = ref.shape[1]
        for c in range(0, width, 512):
            ref[:, c:c + 512] = _dot(xn, w_ref[:, off + c:off + c + 512]).astype(ref.dtype)
        off += width


def _inproj(h, norm_w, w_in, layer, tm=512):
    t = h.shape[0]
    assert t % tm == 0, (t, tm)
    nslab = BRANCH_WIDTH // LANES

    def rows(w):
        return pl.BlockSpec((tm, w), lambda i: (i, 0))

    return pl.pallas_call(
        _inproj_kernel,
        grid=(t // tm,),
        in_specs=[rows(D_MODEL),
                  _layer_block(norm_w, layer), _layer_block(w_in, layer)],
        out_specs=[rows(3 * BRANCH_WIDTH), pl.BlockSpec((nslab, tm, LANES), lambda i: (0, i, 0)),
                   rows(4 * BRANCH_WIDTH), rows(N_BRANCH * D_MODEL)],
        out_shape=[jax.ShapeDtypeStruct((t, 3 * BRANCH_WIDTH), BF16),
                   jax.ShapeDtypeStruct((nslab, t, LANES), F32),
                   jax.ShapeDtypeStruct((t, 4 * BRANCH_WIDTH), BF16),
                   jax.ShapeDtypeStruct((t, N_BRANCH * D_MODEL), BF16)],
        compiler_params=_cparams(("parallel",)),
        name="inproj",
    )(h, norm_w, w_in)


def _sba_kernel(q_ref, k_ref, v_ref, m_ref, o_ref, acc_ref, car_ref, *, blk, nsub):
    step = pl.program_id(2)
    lane = lax.broadcasted_iota(jnp.int32, (blk, LANES), 1)
    zscale = HEAD_DIM ** -0.5 * LOG2_E
    tri = m_ref[...]
    q_heads = []
    for sub in range(nsub):
        q = q_ref[0, sub * blk:(sub + 1) * blk, :]
        zero = jnp.zeros_like(q)
        q_heads.append((jnp.where(lane < HEAD_DIM, q, zero), jnp.where(lane >= HEAD_DIM, q, zero)))

    def keys(j):
        start = pl.multiple_of(j * blk, blk)
        return k_ref[0, pl.ds(start, blk), :], v_ref[0, pl.ds(start, blk), :]

    def log_terms(z2, causal):
        sp = jnp.log(1.0 + jnp.exp2(-jnp.abs(z2))) * LOG2_E
        log_beta = jnp.minimum(z2, 0.0) - sp
        log_1m = log_beta - z2
        if causal is not None:
            log_1m = jnp.where(causal, log_1m, 0.0)
        suffix = _dot(log_1m.astype(BF16), tri)
        return log_beta, suffix, jnp.sum(log_1m, axis=1, keepdims=True)

    def visit(key_blocks, jobs):
        causal = lax.broadcasted_iota(jnp.int32, (blk, blk), 1) < lax.broadcasted_iota(jnp.int32, (blk, blk), 0)
        kv = [keys(j) for j in key_blocks]
        chains = [(n, b, hd) for n, job in enumerate(jobs) for b in range(len(job[1])) for hd in range(2)]

        def mask(n, b):
            return causal if (jobs[n][2] and b == 0) else None

        z2 = {c: _dot_nt(q_heads[jobs[c[0]][0]][c[2]], kv[jobs[c[0]][1][c[1]]][0]) * zscale for c in chains}
        terms = {c: log_terms(z2[c], mask(c[0], c[1])) for c in chains}
        results = []
        for n, (sub, block_ids, _, car_in) in enumerate(jobs):
            per_head = []
            for hd in range(2):
                car = None if car_in is None else car_in[hd]
                acc = None
                for b in range(len(block_ids)):
                    log_beta, suffix, tot = terms[(n, b, hd)]
                    w = jnp.exp2(log_beta + suffix if car is None else log_beta + suffix + car)
                    if mask(n, b) is not None:
                        w = jnp.where(causal, w, 0.0)
                    pv = _dot(w.astype(BF16), kv[block_ids[b]][1])
                    acc = pv if acc is None else acc + pv
                    car = tot if car is None else car + tot
                per_head.append((acc, car))
            results.append(per_head)
        return results

    def first_visit(first_step):
        first = step * nsub
        key_blocks = [first + sub for sub in range(nsub)] + ([] if first_step else [first - 1])
        jobs = []
        for sub in range(nsub):
            ids = [sub]
            if sub > 0:
                ids.append(sub - 1)
            elif not first_step:
                ids.append(nsub)
            jobs.append((sub, ids, True, None))
        for sub, per_head in enumerate(visit(key_blocks, jobs)):
            for hd, (acc, car) in enumerate(per_head):
                acc_ref[sub, hd] = acc
                car_ref[sub, hd] = car

    pl.when(step == 0)(lambda: first_visit(True))
    pl.when(step > 0)(lambda: first_visit(False))

    def remaining_blocks():
        for sub in range(nsub):
            def mass_left(sub=sub):
                most = jnp.maximum(jnp.max(car_ref[sub, 0]), jnp.max(car_ref[sub, 1]))
                return (most > SB_SKIP_LOG2).astype(jnp.int32)

            def cond(c):
                j, go = c
                return jnp.logical_and(j >= 0, go > 0)

            def body(c, sub=sub, mass_left=mass_left):
                j, _ = c
                (per_head,) = visit([j], [(sub, [0], False, (car_ref[sub, 0], car_ref[sub, 1]))])
                for hd, (acc, car) in enumerate(per_head):
                    acc_ref[sub, hd] += acc
                    car_ref[sub, hd] = car
                return j - 1, mass_left()

            lax.while_loop(cond, body, (step * nsub + sub - 2, mass_left()))

    most = functools.reduce(jnp.maximum, [car_ref[sub, hd] for sub in range(nsub) for hd in range(2)])
    pl.when(jnp.max(most) > SB_SKIP_LOG2)(remaining_blocks)
    for sub in range(nsub):
        o_ref[0, sub * blk:(sub + 1) * blk, :] = jnp.where(lane < HEAD_DIM, acc_ref[sub, 0],
                                                           acc_ref[sub, 1]).astype(o_ref.dtype)


def _sb_attention(qkv, batch, seq, blk=SB_BLOCK, nsub=SB_SUBBLOCKS):
    qkv3 = qkv.reshape(batch, seq, 3 * BRANCH_WIDTH)
    pairs = BRANCH_WIDTH // LANES
    tq = blk * nsub
    assert seq % tq == 0, (seq, tq)
    r = np.arange(blk)
    m_ext = jnp.asarray(r[:, None] > r[None, :], BF16)
    out = pl.pallas_call(
        functools.partial(_sba_kernel, blk=blk, nsub=nsub),
        grid=(batch, pairs, seq // tq),
        in_specs=[pl.BlockSpec((1, tq, LANES), lambda b, p, i: (b, i, p)),
                  pl.BlockSpec((1, seq, LANES), lambda b, p, i: (b, 0, pairs + p)),
                  pl.BlockSpec((1, seq, LANES), lambda b, p, i: (b, 0, 2 * pairs + p)),
                  pl.BlockSpec(m_ext.shape, lambda b, p, i: (0, 0))],
        out_specs=pl.BlockSpec((1, tq, LANES), lambda b, p, i: (b, i, p)),
        out_shape=jax.ShapeDtypeStruct((batch, seq, BRANCH_WIDTH), BF16),
        scratch_shapes=[pltpu.VMEM((nsub, 2, blk, LANES), F32), pltpu.VMEM((nsub, 2, blk, 1), F32)],
        compiler_params=_cparams(("parallel", "parallel", "parallel")),
        name="sb_attention",
    )(qkv3, qkv3, qkv3, m_ext)
    return out.reshape(batch * seq, BRANCH_WIDTH)


def _s5_operators(lam_re, lam_im, log_dt, b_re, b_im, c_re, c_im, d_skip):
    L = S5_CHUNK
    g, p, c = lam_re.shape[0], S5_STATE, S5_GROUP_CH
    dt = jnp.exp(log_dt.astype(F32))[:, None]
    lr = lam_re.astype(F32)
    li = lam_im.astype(F32)

    def a_pow(n):
        n = jnp.asarray(n, F32)[..., None, None]
        mag = jnp.exp(lr * dt * n)
        return mag * jnp.cos(li * dt * n), mag * jnp.sin(li * dt * n)

    ar, ai = a_pow(1.0)
    den = lr * lr + li * li
    nr = ar - 1.0
    fr = (nr * lr + ai * li) / den
    fi = (ai * lr - nr * li) / den
    bbr = fr[..., None] * b_re - fi[..., None] * b_im
    bbi = fr[..., None] * b_im + fi[..., None] * b_re
    cr = c_re.astype(F32)
    ci = c_im.astype(F32)

    pr, pi = a_pow(np.arange(L))
    car = cr[None] * pr[:, :, None, :] - ci[None] * pi[:, :, None, :]
    cai = cr[None] * pi[:, :, None, :] + ci[None] * pr[:, :, None, :]
    klag = jnp.einsum('jgop,gpi->jgoi', car, bbr) - jnp.einsum('jgop,gpi->jgoi', cai, bbi)
    klag = klag.at[0].add(d_skip.astype(F32).reshape(g, c)[:, :, None] * jnp.eye(c, dtype=F32))
    kflat = klag.transpose(1, 3, 0, 2).reshape(g, c, L * c)
    toe = jnp.stack([jnp.pad(kflat[:, :, :(L - s) * c], ((0, 0), (0, 0), (s * c, 0))) for s in range(L)], axis=1)
    toe = toe.reshape(g, L * c, L * c)

    qr, qi = a_pow(L - 1 - np.arange(L))
    p_re = qr[:, :, :, None] * bbr[None] - qi[:, :, :, None] * bbi[None]
    p_im = qr[:, :, :, None] * bbi[None] + qi[:, :, :, None] * bbr[None]
    p_re = p_re.transpose(1, 0, 3, 2).reshape(g, L * c, p)
    p_im = p_im.transpose(1, 0, 3, 2).reshape(g, L * c, p)

    er, ei = a_pow(1 + np.arange(L))
    q_re = (cr[None] * er[:, :, None, :] - ci[None] * ei[:, :, None, :])
    q_im = -(cr[None] * ei[:, :, None, :] + ci[None] * er[:, :, None, :])
    q_re = q_re.transpose(1, 3, 0, 2).reshape(g, p, L * c)
    q_im = q_im.transpose(1, 3, 0, 2).reshape(g, p, L * c)

    a_lr, a_li = a_pow(float(L))

    def pair_diag(m):
        gg, r, cc = m.shape
        m = m.reshape(gg // 2, 2, r, cc)
        z = jnp.zeros_like(m[:, 0])
        top = jnp.concatenate([m[:, 0], z], axis=2)
        bot = jnp.concatenate([z, m[:, 1]], axis=2)
        return jnp.concatenate([top, bot], axis=1)

    toe_p = pair_diag(toe).astype(BF16)
    p_p = jnp.concatenate([pair_diag(p_re), pair_diag(p_im)], axis=2).astype(BF16)
    q_p = jnp.concatenate([pair_diag(q_re), pair_diag(q_im)], axis=1).astype(BF16)
    a_re = a_lr.reshape(1, g * p)
    a_im = a_li.reshape(1, g * p)
    return toe_p, p_p, q_p, a_re, a_im


def _group_block_transpose(tiles):
    group = lax.broadcasted_iota(jnp.int32, tiles[0].shape, 1) // S5_GROUP_CH
    n = len(tiles)
    k = n // 2
    while k >= 1:
        upper = (group & k) != 0
        nxt = list(tiles)
        for i in range(n):
            if i & k == 0:
                lo, hi = tiles[i], tiles[i + k]
                nxt[i] = jnp.where(upper, pltpu.roll(hi, k * S5_GROUP_CH, 1), lo)
                nxt[i + k] = jnp.where(upper, hi, pltpu.roll(lo, LANES - k * S5_GROUP_CH, 1))
        tiles = nxt
        k //= 2
    return tiles


def _s5_state_in_kernel(u_ref, p_ref, up_ref, vre_ref, vim_ref, *, nb):
    L = S5_CHUNK
    per = LANES // S5_GROUP_CH
    for kb in range(u_ref.shape[0]):
        halves = []
        for a in range(L // per):
            steps = [u_ref[kb, pl.ds(a * per + b, nb, stride=L), :] for b in range(per)]
            halves.append(_group_block_transpose(steps))
        for pi in range(per // 2):
            pair = kb * (per // 2) + pi
            row = jnp.concatenate([halves[a][2 * pi + gi] for gi in range(2) for a in range(L // per)], axis=1)
            row = row.astype(BF16)
            up_ref[pair] = row
            v = _dot(row, p_ref[pair])
            vre_ref[:, pair * LANES:(pair + 1) * LANES] = v[:, :LANES]
            vim_ref[:, pair * LANES:(pair + 1) * LANES] = v[:, LANES:]


def _s5_out_kernel(up_ref, sre_ref, sim_ref, toe_ref, q_ref, y_ref, *, nb):
    L = S5_CHUNK
    per = LANES // S5_GROUP_CH
    for kb in range(y_ref.shape[0]):
        halves = [[None] * per for _ in range(L // per)]
        for pi in range(per // 2):
            pair = kb * (per // 2) + pi
            q = q_ref[pair]
            cols = slice(pair * LANES, (pair + 1) * LANES)
            y = _dot(up_ref[pair], toe_ref[pair])
            y += _dot(sre_ref[:, cols].astype(BF16), q[:LANES]) + _dot(sim_ref[:, cols].astype(BF16), q[LANES:])
            y = jax.nn.gelu(y)
            for gi in range(2):
                for a in range(L // per):
                    j = gi * (L // per) + a
                    halves[a][2 * pi + gi] = y[:, j * LANES:(j + 1) * LANES]
        for a in range(L // per):
            steps = _group_block_transpose(halves[a])
            for b in range(per):
                y_ref[kb, pl.ds(a * per + b, nb, stride=L), :] = steps[b]


def _s5_kernel(u_ref, p_ref, toe_ref, q_ref, are_ref, aim_ref, y_ref, up_ref, vre_ref, vim_ref, sre_ref, sim_ref,
               x_ref, *, nb, steps_per_seq):
    @pl.when(pl.program_id(0) % steps_per_seq == 0)
    def _():
        x_ref[...] = jnp.zeros_like(x_ref)

    _s5_state_in_kernel(u_ref, p_ref, up_ref, vre_ref, vim_ref, nb=nb)
    ar = are_ref[...]
    ai = aim_ref[...]

    def step(r, carry):
        xr, xi = carry
        sre_ref[pl.ds(r, 1), :] = xr
        sim_ref[pl.ds(r, 1), :] = xi
        vr = vre_ref[pl.ds(r, 1), :]
        vi = vim_ref[pl.ds(r, 1), :]
        return ar * xr - ai * xi + vr, ar * xi + ai * xr + vi

    xr, xi = lax.fori_loop(0, nb, step, (x_ref[0:1, :], x_ref[1:2, :]))
    x_ref[0:1, :] = xr
    x_ref[1:2, :] = xi
    _s5_out_kernel(up_ref, sre_ref, sim_ref, toe_ref, q_ref, y_ref, nb=nb)


def _s5_ssm(u, batch, seq, ops, layer, nb=128):
    toe_p, p_p, q_p, a_re, a_im = ops
    nslab, t, _ = u.shape
    L = S5_CHUNK
    npair = S5_GROUPS // 2
    nc = t // L
    row = 2 * L * S5_GROUP_CH
    ncols = S5_GROUPS * S5_STATE
    nb = min(nb, seq // L)
    assert seq % (L * nb) == 0, (seq, L, nb)

    slab_spec = pl.BlockSpec((nslab, nb * L, LANES), lambda i: (0, i, 0))
    return pl.pallas_call(
        functools.partial(_s5_kernel, nb=nb, steps_per_seq=seq // (L * nb)),
        grid=(nc // nb,),
        in_specs=[slab_spec] + [_layer_block(a, layer) for a in (p_p, toe_p, q_p, a_re, a_im)],
        out_specs=slab_spec,
        out_shape=jax.ShapeDtypeStruct((nslab, t, LANES), F32),
        scratch_shapes=[pltpu.VMEM((npair, nb, row), BF16)] + [pltpu.VMEM((nb, ncols), F32)] * 4
        + [pltpu.VMEM((8, ncols), F32)],
        compiler_params=_cparams(("arbitrary",)),
        name="s5_ssm",
    )(u, p_p, toe_p, q_p, a_re, a_im)


def _ret_tables(seq):
    half = HEAD_DIM // 2
    pos = jnp.arange(seq, dtype=F32)
    inv = ROPE_BASE ** (-jnp.arange(half, dtype=F32) / half)
    ang = pos[:, None] * inv[None, :]
    cos = jnp.tile(jnp.cos(ang), (1, LANES // half))
    sin = jnp.sin(ang)
    sin = jnp.tile(jnp.concatenate([-sin, sin], axis=1), (1, LANES // HEAD_DIM))
    c = RET_CHUNK
    gamma = 1.0 - 2.0 ** (-5.0 - jnp.arange(RET_HEADS, dtype=F32))
    log_g = jnp.log(gamma)
    idx = jnp.arange(c, dtype=F32)
    diff = idx[:, None] - idx[None, :]
    intra = jnp.where(diff >= 0, jnp.exp(jnp.maximum(diff, 0.0)[None] * log_g[:, None, None]), 0.0)
    q_dec = jnp.exp((idx + 1.0)[None, :] * log_g[:, None])
    k_dec = jnp.exp((c - 1.0 - idx)[None, :] * log_g[:, None])
    ch_dec = jnp.exp(c * log_g)

    def lanes(x):
        x = x.reshape(RET_HEADS // 2, 2, c)
        return jnp.repeat(x.transpose(0, 2, 1), HEAD_DIM, axis=2)

    q_dec_l = lanes(q_dec)
    k_dec_l = lanes(k_dec)
    hp = RET_HEADS // 2
    blockdiag = np.kron(np.eye(2), np.ones((HEAD_DIM, HEAD_DIM)))
    st_dec = jnp.repeat(ch_dec.reshape(hp, 2), HEAD_DIM, axis=1)[:, :, None] * blockdiag[None]
    avg = jnp.asarray(blockdiag / HEAD_DIM, BF16)
    return cos, sin, intra, q_dec_l, k_dec_l, st_dec.astype(F32), avg


def _ret_kernel(q_ref, k_ref, v_ref, g_ref, cos_ref, sin_ref, intra_ref, qd_ref, kd_ref, sd_ref, avg_ref, nw_ref,
                o_ref, st_ref, *, nchunk):
    c = RET_CHUNK

    @pl.when(pl.program_id(2) == 0)
    def _():
        st_ref[...] = jnp.zeros_like(st_ref)

    tq = q_ref.shape[1]
    lane = lax.broadcasted_iota(jnp.int32, (tq, LANES), 1)
    first_half = (lane % HEAD_DIM) < (HEAD_DIM // 2)
    head0 = lax.broadcasted_iota(jnp.int32, (c, LANES), 1) < HEAD_DIM
    sd = sd_ref[0]
    avg = avg_ref[...]
    same_head = avg > 0
    cos = cos_ref[...]
    sin = sin_ref[...]

    def rotary(x):
        swapped = jnp.where(first_half, pltpu.roll(x, LANES - HEAD_DIM // 2, 1), pltpu.roll(x, HEAD_DIM // 2, 1))
        return x * cos + swapped * sin

    q_all = rotary(q_ref[0].astype(F32)).astype(BF16)
    k_all = rotary(k_ref[0].astype(F32)) * (HEAD_DIM ** -0.5)
    kb_all = k_all.astype(BF16)
    zero = jnp.zeros((c, LANES), BF16)
    chunks = [slice(ci * c, (ci + 1) * c) for ci in range(nchunk)]
    vs = [v_ref[0, rows, :] for rows in chunks]
    scores, incs = [], []
    for ci, rows in enumerate(chunks):
        qb = q_all[rows]
        q2 = jnp.concatenate([jnp.where(head0, qb, zero), jnp.where(head0, zero, qb)], axis=0)
        scores.append(_dot_nt(q2, kb_all[rows]))
        kd_t = (k_all[rows] * kd_ref[0]).T.astype(BF16)
        incs.append(_dot(kd_t, vs[ci]))
    state = st_ref[...]
    states = []
    for inc in incs:
        states.append(state.astype(BF16))
        state = state * sd + jnp.where(same_head, inc, 0.0)
    st_ref[...] = state
    intra2 = jnp.concatenate([intra_ref[0, 0], intra_ref[0, 1]], axis=0)
    outs = []
    for ci, rows in enumerate(chunks):
        o2 = _dot((scores[ci] * intra2).astype(BF16), vs[ci])
        outs.append(jnp.where(head0, o2[:c], o2[c:]) + _dot(q_all[rows], states[ci]) * qd_ref[0])
    out = jnp.concatenate(outs, axis=0)
    hi, lo = _split_bf16(out)
    mu = _dot(hi, avg) + _dot(lo, avg)
    dlt = out - mu
    hi, lo = _split_bf16(dlt * dlt)
    var = _dot(hi, avg) + _dot(lo, avg)
    gate = g_ref[0].astype(F32)
    y = dlt * lax.rsqrt(var + EPS) * nw_ref[...] * (gate * jax.nn.sigmoid(gate))
    o_ref[0] = y.astype(o_ref.dtype)


def _retention(ret, norm_w, batch, seq, tables, tq=2048):
    cos, sin, intra, q_dec_l, k_dec_l, st_dec, avg = tables
    pairs = BRANCH_WIDTH // LANES
    ret3 = ret.reshape(batch, seq, 4 * BRANCH_WIDTH)
    tq = min(tq, seq)
    assert seq % tq == 0 and tq % RET_CHUNK == 0, (seq, tq)
    c = RET_CHUNK

    def col(k):
        return pl.BlockSpec((1, tq, LANES), lambda b, p, i: (b, i, k * pairs + p))

    out = pl.pallas_call(
        functools.partial(_ret_kernel, nchunk=tq // c),
        grid=(batch, pairs, seq // tq),
        in_specs=[col(0), col(1), col(2), col(3),
                  pl.BlockSpec((tq, LANES), lambda b, p, i: (i, 0)),
                  pl.BlockSpec((tq, LANES), lambda b, p, i: (i, 0)),
                  pl.BlockSpec((1, 2, c, c), lambda b, p, i: (p, 0, 0, 0)),
                  pl.BlockSpec((1, c, LANES), lambda b, p, i: (p, 0, 0)),
                  pl.BlockSpec((1, c, LANES), lambda b, p, i: (p, 0, 0)),
                  pl.BlockSpec((1, LANES, LANES), lambda b, p, i: (p, 0, 0)),
                  pl.BlockSpec((LANES, LANES), lambda b, p, i: (0, 0)),
                  pl.BlockSpec((1, LANES), lambda b, p, i: (0, p))],
        out_specs=pl.BlockSpec((1, tq, LANES), lambda b, p, i: (b, i, p)),
        out_shape=jax.ShapeDtypeStruct((batch, seq, BRANCH_WIDTH), BF16),
        scratch_shapes=[pltpu.VMEM((LANES, LANES), F32)],
        compiler_params=_cparams(("parallel", "parallel", "arbitrary")),
        name="retention",
    )(ret3, ret3, ret3, ret3, cos, sin, intra.reshape(pairs, 2, c, c), q_dec_l, k_dec_l, st_dec, avg,
      norm_w.reshape(1, BRANCH_WIDTH).astype(F32))
    return out.reshape(batch * seq, BRANCH_WIDTH)


def _route(logits):
    lane = lax.broadcasted_iota(jnp.int32, logits.shape, 1).astype(F32)
    neg = jnp.float32(-jnp.inf)
    big = jnp.float32(1 << 20)
    gl = jnp.where(lane < N_GROUPS, logits, neg)
    gmax = jnp.max(gl, axis=1, keepdims=True)
    gsum = jnp.sum(jnp.exp(gl - gmax), axis=1, keepdims=True)
    g_w = 1.0 / gsum
    g_idx = jnp.min(jnp.where(gl == gmax, lane, big), axis=1, keepdims=True)
    lo = ROUTER_LANE0 + EXPERTS_PER_GROUP * g_idx
    sl = jnp.where((lane >= lo) & (lane < lo + EXPERTS_PER_GROUP), logits, neg)
    m1 = jnp.max(sl, axis=1, keepdims=True)
    i1 = jnp.min(jnp.where(sl == m1, lane, big), axis=1, keepdims=True)
    sl2 = jnp.where(lane == i1, neg, sl)
    m2 = jnp.max(sl2, axis=1, keepdims=True)
    i2 = jnp.min(jnp.where(sl2 == m2, lane, big), axis=1, keepdims=True)
    e2 = jnp.exp(m2 - m1)
    w1 = g_w / (1.0 + e2)
    w2 = g_w * e2 / (1.0 + e2)
    packed = jnp.where(lane == 0.0, i1 - ROUTER_LANE0, jnp.where(lane == 1.0, i2 - ROUTER_LANE0, 0.0))
    return packed + jnp.where(lane == 2.0, w1, 0.0) + jnp.where(lane == 3.0, w2, 0.0)


def _merge_kernel(ysb_ref, ys5_ref, yret_ref, gate_ref, h_ref, wglu_ref, wbr_ref, wout_ref, n2_ref, wr_hi_ref,
                  wr_lo_ref, br_ref, tri_ref, cum_ref, h_out_ref, hn_ref, route_ref, pos_ref, cnt_ref):
    ys5 = jnp.concatenate([ys5_ref[s] for s in range(ys5_ref.shape[0])], axis=1).astype(BF16)
    glu = _dot(ys5, wglu_ref[...])
    y_s5 = (glu[:, :BRANCH_WIDTH] * jax.nn.sigmoid(glu[:, BRANCH_WIDTH:])).astype(BF16)
    merged = None
    for n, y in enumerate((ysb_ref[...], y_s5, yret_ref[...])):
        g = jax.nn.sigmoid(gate_ref[:, n * D_MODEL:(n + 1) * D_MODEL].astype(F32))
        term = g * _dot(y, wbr_ref[n])
        merged = term if merged is None else merged + term
    h = h_ref[...] + _dot(merged.astype(BF16), wout_ref[...])
    h_out_ref[...] = h
    ms = jnp.mean(h * h, axis=-1, keepdims=True)
    hn = h * lax.rsqrt(ms + EPS) * n2_ref[...]
    hn_ref[...] = hn.astype(BF16)
    hi, lo = _split_bf16(hn)
    logits = _dot(hi, wr_hi_ref[...]) + _dot(hi, wr_lo_ref[...]) + _dot(lo, wr_hi_ref[...]) + br_ref[...]
    route = _route(logits)
    route_ref[...] = route
    pos_ref[...], cnt_ref[0] = _moe_plan(route, tri_ref[...], cum_ref[...])


def _merge(y_sb, y_s5, y_ret, gates, h, w_glu, w_br, w_out, norm2_w, wr_hi, wr_lo, b_r, layer):
    t = h.shape[0]
    tm = min(MOE_TILE, t)
    assert t % tm == 0, (t, tm)
    r = np.arange(tm)
    tri = jnp.asarray(r[None, :] <= r[:, None], BF16)
    l = np.arange(LANES)
    cum = jnp.asarray(l[:, None] < l[None, :], BF16)

    def rows(w):
        return pl.BlockSpec((tm, w), lambda i: (i, 0))

    return pl.pallas_call(
        _merge_kernel,
        grid=(t // tm,),
        in_specs=[rows(BRANCH_WIDTH), pl.BlockSpec((y_s5.shape[0], tm, LANES), lambda i: (0, i, 0)),
                  rows(BRANCH_WIDTH), rows(N_BRANCH * D_MODEL),
                  rows(D_MODEL)] + [_layer_block(a, layer) for a in (w_glu, w_br, w_out, norm2_w, wr_hi, wr_lo, b_r)]
        + [pl.BlockSpec(tri.shape, lambda i: (0, 0), pipeline_mode=pl.Buffered(1)),
           pl.BlockSpec(cum.shape, lambda i: (0, 0), pipeline_mode=pl.Buffered(1))],
        out_specs=[rows(D_MODEL), rows(D_MODEL), rows(LANES), rows(LANES), pl.BlockSpec((1, 8, LANES), lambda i: (i, 0, 0))],
        out_shape=[jax.ShapeDtypeStruct((t, D_MODEL), F32), jax.ShapeDtypeStruct((t, D_MODEL), BF16),
                   jax.ShapeDtypeStruct((t, LANES), F32), jax.ShapeDtypeStruct((t, LANES), F32),
                   jax.ShapeDtypeStruct((t // tm, 8, LANES), F32)],
        compiler_params=_cparams(("parallel",)),
        name="merge_route",
    )(y_sb, y_s5, y_ret, gates, h, w_glu, w_br, w_out, norm2_w, wr_hi, wr_lo, b_r, tri, cum)


MOE_TILE = 512
MOE_ROW_TILE = 512
SEG_ALIGN = 16
MOE_SLOTS = TOP_K * MOE_TILE + N_EXPERTS * SEG_ALIGN
MOE_CHUNKS = MOE_SLOTS // SEG_ALIGN
MOE_SLOT_BLOCK = 256


def _moe_plan(route, tri, cum):
    tm = route.shape[0]
    lane = lax.broadcasted_iota(jnp.int32, route.shape, 1).astype(F32)
    e1 = route[:, 0:1]
    e2 = route[:, 1:2]
    chosen = (lane == e1) | (lane == e2)
    onehot = jnp.where(chosen, 1.0, 0.0).astype(BF16)
    incl = _dot(tri, onehot)
    cnt = incl[tm - 1:tm, :]
    padded = jnp.floor((cnt + (SEG_ALIGN - 1)) * (1.0 / SEG_ALIGN)) * SEG_ALIGN
    start = _dot(jnp.broadcast_to(padded, (8, LANES)).astype(BF16), cum)[0:1, :]
    slot = start + incl - 1.0
    p1 = jnp.sum(jnp.where(lane == e1, slot, 0.0), axis=1, keepdims=True)
    p2 = jnp.sum(jnp.where(lane == e2, slot, 0.0), axis=1, keepdims=True)
    pos = jnp.where(lane == 0.0, p1, jnp.where(lane == 1.0, p2, 0.0))
    return pos, jnp.broadcast_to(cnt, (8, LANES))


def _moe_tables(cnt, rt, n_rows):
    a = SEG_ALIGN
    ntile = cnt.shape[0]
    p = (cnt + a - 1) // a * a
    lend = jnp.cumsum(p, axis=1)
    lstart = lend - p
    seg = p.sum(axis=0)
    segpad = (seg + rt - 1) // rt * rt
    gend = jnp.cumsum(segpad)
    gstart = gend - segpad
    toff = gstart[None, :] + jnp.cumsum(p, axis=0) - p
    cstart = a * jnp.arange(MOE_CHUNKS, dtype=jnp.int32)
    in_run = (cstart[None, :, None] >= lstart[:, None, :]) & (cstart[None, :, None] < lend[:, None, :])
    gdst = (jnp.sum(jnp.where(in_run, (toff - lstart)[:, None, :], 0), axis=2) + cstart[None, :]) // a
    nch = lend[:, -1] // a
    tstart = rt * jnp.arange(n_rows // rt, dtype=jnp.int32)
    tile_e = jnp.minimum(jnp.sum(tstart[:, None] >= gend[None, :], axis=1), N_EXPERTS - 1)
    n_used = gend[-1] // rt
    zmax = rt // a - 1
    zper = -(-N_EXPERTS * zmax // ntile)
    k = jnp.arange(zmax, dtype=jnp.int32)
    zd = ((gstart + seg) // a)[:, None] + k[None, :]
    valid = (k[None, :] < ((segpad - seg) // a)[:, None]).reshape(-1)
    order = jnp.argsort(jnp.logical_not(valid), stable=True)
    zdst = jnp.pad(zd.reshape(-1)[order], (0, ntile * zper - N_EXPERTS * zmax))
    nz = valid.sum()
    i32 = jnp.int32
    return (gdst.reshape(-1).astype(i32), nch.astype(i32), zdst.astype(i32), nz.reshape(1).astype(i32),
            tile_e.astype(i32), n_used.reshape(1).astype(i32)), zper


def _moe_dispatch_kernel(gdst_ref, nch_ref, zdst_ref, nz_ref, n_used_ref, pos_ref, hn_ref, xs_hbm, stg, zblk, sem,
                         zsem, *, zper):
    i = pl.program_id(0)
    last_step = pl.num_programs(0) - 1
    buf = i % 2
    tm = hn_ref.shape[0]
    pos_t = pos_ref[...].T
    p1 = pos_t[0:1, :]
    p2 = pos_t[1:2, :]
    x = hn_ref[...]
    for lo in range(0, MOE_SLOTS, tm):
        slot = (lax.broadcasted_iota(jnp.int32, (tm, tm), 0) + lo).astype(F32)
        onehot = jnp.where((slot == p1) | (slot == p2), 1.0, 0.0).astype(BF16)
        stg[buf, lo:lo + tm, :] = _dot(onehot, x).astype(BF16)

    def chunk_copy(which, src_chunk, dst_chunk):
        src = pl.multiple_of(src_chunk * SEG_ALIGN, SEG_ALIGN)
        dst = pl.multiple_of(dst_chunk * SEG_ALIGN, SEG_ALIGN)
        return pltpu.make_async_copy(stg.at[which, pl.ds(src, SEG_ALIGN)], xs_hbm.at[pl.ds(dst, SEG_ALIGN)],
                                     sem.at[which])

    def n_copies(step):
        return nch_ref[step], jnp.clip(nz_ref[0] - step * zper, 0, zper)

    def drain(which, step):
        for count in n_copies(step):
            rows = count * SEG_ALIGN

            @pl.when(rows > 0)
            def _():
                pltpu.make_async_copy(stg.at[which, pl.ds(0, rows)], xs_hbm.at[pl.ds(0, rows)],
                                      sem.at[which]).wait()

    n, nzero = n_copies(i)

    @pl.loop(0, n)
    def _(c):
        chunk_copy(buf, c, gdst_ref[i * MOE_CHUNKS + c]).start()

    @pl.loop(0, nzero)
    def _(k):
        chunk_copy(buf, MOE_CHUNKS - 1, zdst_ref[i * zper + k]).start()

    pl.when(i > 0)(lambda: drain(1 - buf, i - 1))

    @pl.when(i == last_step)
    def _():
        drain(buf, i)
        rt = zblk.shape[0]
        zblk[...] = jnp.zeros_like(zblk)
        first = n_used_ref[0]
        last = xs_hbm.shape[0] // rt

        def tile_copy(j):
            return pltpu.make_async_copy(zblk, xs_hbm.at[pl.ds(pl.multiple_of(j * rt, rt), rt)], zsem)

        @pl.loop(first, last)
        def _(j):
            tile_copy(j).start()

        @pl.loop(first, last)
        def _(j):
            tile_copy(0).wait()


def _moe_expert_kernel(tile_e_ref, n_used_ref, xs_ref, wg_ref, wu_ref, wd_ref, ys_ref):
    used = pl.program_id(0) < n_used_ref[0]

    @pl.when(used)
    def _():
        x = xs_ref[...]
        gate = _dot(x, wg_ref[0].astype(BF16))
        up = _dot(x, wu_ref[0].astype(BF16))
        hdn = gate * jax.nn.sigmoid(gate) * up
        ys_ref[...] = _dot(hdn.astype(BF16), wd_ref[0].astype(BF16)).astype(ys_ref.dtype)

    @pl.when(jnp.logical_not(used))
    def _():
        ys_ref[...] = jnp.zeros_like(ys_ref)


def _moe_combine_kernel(gdst_ref, nch_ref, route_ref, pos_ref, h_ref, nw_ref, ys_hbm, o_ref, stg, sem, *, out_norm):
    i = pl.program_id(0)
    nsteps = pl.num_programs(0)
    buf = i % 2
    tm = h_ref.shape[0]

    def chunk_copy(which, src_chunk, dst_chunk):
        src = pl.multiple_of(src_chunk * SEG_ALIGN, SEG_ALIGN)
        dst = pl.multiple_of(dst_chunk * SEG_ALIGN, SEG_ALIGN)
        return pltpu.make_async_copy(ys_hbm.at[pl.ds(src, SEG_ALIGN)], stg.at[which, pl.ds(dst, SEG_ALIGN)],
                                     sem.at[which])

    def fetch(step, which):
        n = nch_ref[step]

        @pl.loop(0, n)
        def _(c):
            chunk_copy(which, gdst_ref[step * MOE_CHUNKS + c], c).start()

        per = MOE_SLOT_BLOCK // SEG_ALIGN
        stop = jnp.minimum((n + per - 1) // per * per, MOE_CHUNKS)

        @pl.loop(n, stop)
        def _(c):
            rows = pl.ds(pl.multiple_of(c * SEG_ALIGN, SEG_ALIGN), SEG_ALIGN)
            stg[which, rows, :] = jnp.zeros((SEG_ALIGN, D_MODEL), BF16)

    pl.when(i == 0)(lambda: fetch(0, 0))
    pl.when(i + 1 < nsteps)(lambda: fetch(i + 1, 1 - buf))

    r = route_ref[...]
    pos = pos_ref[...]

    def scatter(lo, hi):
        slot = (lax.broadcasted_iota(jnp.int32, (tm, hi - lo), 1) + lo).astype(F32)
        w = jnp.where(slot == pos[:, 0:1], r[:, 2:3], 0.0) + jnp.where(slot == pos[:, 1:2], r[:, 3:4], 0.0)
        return _dot(w.astype(BF16), stg[buf, lo:hi, :])

    rows = nch_ref[i] * SEG_ALIGN

    @pl.when(rows > 0)
    def _():
        pltpu.make_async_copy(ys_hbm.at[pl.ds(0, rows)], stg.at[buf, pl.ds(0, rows)], sem.at[buf]).wait()

    always = TOP_K * tm
    o_ref[...] = h_ref[...] + scatter(0, always)
    for lo in range(always, MOE_SLOTS, MOE_SLOT_BLOCK):
        @pl.when(lo < rows)
        def _(lo=lo):
            o_ref[...] += scatter(lo, lo + MOE_SLOT_BLOCK)
    if out_norm:
        out = o_ref[...]
        ms = jnp.mean(out * out, axis=-1, keepdims=True)
        o_ref[...] = out * lax.rsqrt(ms + EPS) * nw_ref[...]


def _moe(hn, route, pos, cnt, h, w_gate, w_up, w_down, layer, out_norm_w, out_norm):
    t = h.shape[0]
    tm = min(MOE_TILE, t)
    rt = MOE_ROW_TILE
    assert t % tm == 0 and MOE_SLOTS % tm == 0, (t, tm)
    ntile = t // tm
    n_rows = TOP_K * t + ntile * N_EXPERTS * (SEG_ALIGN - 1) + N_EXPERTS * (rt - SEG_ALIGN)
    n_rows = (n_rows + rt - 1) // rt * rt
    (gdst, nch, zdst, nz, tile_e, n_used), zper = _moe_tables(cnt[:, 0, :N_EXPERTS].astype(jnp.int32), rt, n_rows)

    def expert_block(rows, cols):
        return pl.BlockSpec((None, 1, rows, cols), lambda j, te, nu: (layer, te[j], 0, 0))

    xs = pl.pallas_call(
        functools.partial(_moe_dispatch_kernel, zper=zper),
        grid_spec=pltpu.PrefetchScalarGridSpec(
            num_scalar_prefetch=5,
            grid=(ntile,),
            in_specs=[pl.BlockSpec((tm, LANES), lambda i, *_: (i, 0)),
                      pl.BlockSpec((tm, D_MODEL), lambda i, *_: (i, 0))],
            out_specs=pl.BlockSpec(memory_space=pl.ANY),
            scratch_shapes=[pltpu.VMEM((2, MOE_SLOTS, D_MODEL), BF16), pltpu.VMEM((rt, D_MODEL), BF16),
                            pltpu.SemaphoreType.DMA((2,)), pltpu.SemaphoreType.DMA(())]),
        out_shape=jax.ShapeDtypeStruct((n_rows, D_MODEL), BF16),
        compiler_params=_cparams(("arbitrary",)),
        name="moe_dispatch",
    )(gdst, nch, zdst, nz, n_used, pos, hn)
    ys = pl.pallas_call(
        _moe_expert_kernel,
        grid_spec=pltpu.PrefetchScalarGridSpec(
            num_scalar_prefetch=2,
            grid=(n_rows // rt,),
            in_specs=[pl.BlockSpec((rt, D_MODEL), lambda j, te, nu: (jnp.minimum(j, nu[0] - 1), 0)),
                      expert_block(D_MODEL, EXPERT_FF), expert_block(D_MODEL, EXPERT_FF),
                      expert_block(EXPERT_FF, D_MODEL)],
            out_specs=pl.BlockSpec((rt, D_MODEL), lambda j, te, nu: (j, 0))),
        out_shape=jax.ShapeDtypeStruct((n_rows, D_MODEL), BF16),
        compiler_params=_cparams(("arbitrary",)),
        name="moe_experts",
    )(tile_e, n_used, xs, w_gate, w_up, w_down)
    return pl.pallas_call(
        functools.partial(_moe_combine_kernel, out_norm=out_norm),
        grid_spec=pltpu.PrefetchScalarGridSpec(
            num_scalar_prefetch=2,
            grid=(ntile,),
            in_specs=[pl.BlockSpec((tm, LANES), lambda i, *_: (i, 0)),
                      pl.BlockSpec((tm, LANES), lambda i, *_: (i, 0)),
                      pl.BlockSpec((tm, D_MODEL), lambda i, *_: (i, 0)),
                      pl.BlockSpec((1, D_MODEL), lambda i, *_: (0, 0)),
                      pl.BlockSpec(memory_space=pl.ANY)],
            out_specs=pl.BlockSpec((tm, D_MODEL), lambda i, *_: (i, 0)),
            scratch_shapes=[pltpu.VMEM((2, MOE_SLOTS, D_MODEL), BF16), pltpu.SemaphoreType.DMA((2,))]),
        out_shape=jax.ShapeDtypeStruct((t, D_MODEL), F32),
        compiler_params=_cparams(("arbitrary",)),
        name="moe_combine",
    )(gdst, nch, route, pos, h, out_norm_w, ys)


def kernel(x, norm1_w, w_in, s5_lambda_re, s5_lambda_im, s5_log_dt, s5_b_re, s5_b_im, s5_c_re, s5_c_im, s5_d,
           s5_w_glu, ret_norm_w, w_branch, w_out, norm2_w, w_group, b_group, w_router, b_router, w_gate, w_up,
           w_down, final_norm_w):
    batch, seq, _ = x.shape
    depth = w_in.shape[0]
    t = batch * seq
    h = x.reshape(t, D_MODEL).astype(F32)
    ret_tables = _ret_tables(seq)
    w_in, s5_w_glu, w_branch, w_out = (w.astype(BF16) for w in (w_in, s5_w_glu, w_branch, w_out))
    norm1_w = norm1_w.reshape(depth, 1, D_MODEL).astype(F32)
    norm2_w = norm2_w.reshape(depth, 1, D_MODEL).astype(F32)
    s5_params = (s5_lambda_re, s5_lambda_im, s5_log_dt, s5_b_re, s5_b_im, s5_c_re, s5_c_im)
    s5_flat = [a.reshape((depth * S5_GROUPS,) + a.shape[2:]) for a in s5_params] + [s5_d.reshape(-1)]
    toe_p, p_p, q_p, a_re, a_im = _s5_operators(*s5_flat)
    s5_ops = [op.reshape((depth, -1) + op.shape[1:]) for op in (toe_p, p_p, q_p)]
    s5_ops += [a.reshape(depth, 1, -1) for a in (a_re, a_im)]
    w_r = jnp.concatenate([w_group, w_router.transpose(0, 2, 1, 3).reshape(depth, D_MODEL, N_EXPERTS)], axis=2)
    w_r = jnp.pad(w_r.astype(F32), ((0, 0), (0, 0), (0, LANES - w_r.shape[2])))
    wr_hi, wr_lo = _split_bf16(w_r)
    b_r = jnp.pad(jnp.concatenate([b_group, b_router.reshape(depth, -1)], axis=1).astype(F32),
                  ((0, 0), (0, LANES - N_GROUPS - N_EXPERTS))).reshape(depth, 1, LANES)
    for layer in range(depth):
        qkv, u_s5, ret, gates = _inproj(h, norm1_w, w_in, layer)
        y_sb = _sb_attention(qkv, batch, seq)
        y_s5 = _s5_ssm(u_s5, batch, seq, s5_ops, layer)
        y_ret = _retention(ret, ret_norm_w[layer], batch, seq, ret_tables)
        h, hn, route, pos, cnt = _merge(y_sb, y_s5, y_ret, gates, h, s5_w_glu, w_branch, w_out, norm2_w, wr_hi, wr_lo, b_r,
                              layer)
        h = _moe(hn, route, pos, cnt, h, w_gate, w_up, w_down, layer, final_norm_w.reshape(1, D_MODEL).astype(F32),
                 out_norm=(layer == depth - 1))
    return h.reshape(batch, seq, D_MODEL).astype(x.dtype)
```

```python
import functools
import math

import jax
import jax.numpy as jnp
import numpy as np
from jax import lax
from jax.experimental import pallas as pl
from jax.experimental.pallas import tpu as pltpu

F32 = jnp.float32
BF16 = jnp.bfloat16

D_MODEL = 1024
SB_HEADS = 8
HEAD_DIM = 64
BRANCH_WIDTH = 512
S5_GROUPS = 32
S5_GROUP_CH = 16
S5_STATE = 64
RET_HEADS = 8
RET_CHUNK = 128
ROPE_BASE = 10000.0
N_BRANCH = 3
N_GROUPS = 4
EXPERTS_PER_GROUP = 8
N_EXPERTS = 32
TOP_K = 2
EXPERT_FF = 256
EPS = 1e-6

LANES = 128
VMEM_LIMIT = 56 * 1024 * 1024
S5_CHUNK = 16
SB_BLOCK = 256
SB_SUBBLOCKS = 8
LOG2_E = 1.4426950408889634
SB_SKIP_LOG2 = -152.0
ROUTER_LANE0 = N_GROUPS


def _cparams(sem):
    return pltpu.CompilerParams(dimension_semantics=sem, vmem_limit_bytes=VMEM_LIMIT)


def _layer_block(stacked, layer):
    zeros = (0,) * (stacked.ndim - 1)
    return pl.BlockSpec((None,) + stacked.shape[1:], lambda *_: (layer,) + zeros, pipeline_mode=pl.Buffered(1))


def _split_bf16(x):
    hi = x.astype(BF16)
    lo = (x - hi.astype(F32)).astype(BF16)
    return hi, lo


def _dot(a, b):
    return jnp.dot(a, b, preferred_element_type=F32)


def _dot_nt(a, b):
    return lax.dot_general(a, b, (((1,), (1,)), ((), ())), preferred_element_type=F32)


def _inproj_kernel(h_ref, nw_ref, w_ref, qkv_ref, u_ref, ret_ref, gate_ref):
    x = h_ref[...]
    ms = jnp.mean(x * x, axis=-1, keepdims=True)
    xn = (x * lax.rsqrt(ms + EPS) * nw_ref[...]).astype(BF16)
    off = 0
    for ref in (qkv_ref, u_ref, ret_ref, gate_ref):
        if ref is u_ref:
            u = _dot(xn, w_ref[:, off:off + BRANCH_WIDTH])
            for s in range(u_ref.shape[0]):
                u_ref[s] = u[:, s * LANES:(s + 1) * LANES]
            off += BRANCH_WIDTH
            continue
        width = ref.shape[1]
        for c in range(0, width, 512):
            ref[:, c:c + 512] = _dot(xn, w_ref[:, off + c:off + c + 512]).astype(ref.dtype)
        off += width


def _inproj(h, norm_w, w_in, layer, tm=512):
    t = h.shape[0]
    assert t % tm == 0, (t, tm)
    nslab = BRANCH_WIDTH // LANES

    def rows(w):
        return pl.BlockSpec((tm, w), lambda i: (i, 0))

    return pl.pallas_call(
        _inproj_kernel,
        grid=(t // tm,),
        in_specs=[rows(D_MODEL),
                  _layer_block(norm_w, layer), _layer_block(w_in, layer)],
        out_specs=[rows(3 * BRANCH_WIDTH), pl.BlockSpec((nslab, tm, LANES), lambda i: (0, i, 0)),
                   rows(4 * BRANCH_WIDTH), rows(N_BRANCH * D_MODEL)],
        out_shape=[jax.ShapeDtypeStruct((t, 3 * BRANCH_WIDTH), BF16),
                   jax.ShapeDtypeStruct((nslab, t, LANES), F32),
                   jax.ShapeDtypeStruct((t, 4 * BRANCH_WIDTH), BF16),
                   jax.ShapeDtypeStruct((t, N_BRANCH * D_MODEL), BF16)],
        compiler_params=_cparams(("parallel",)),
        name="inproj",
    )(h, norm_w, w_in)


def _sba_kernel(q_ref, k_ref, v_ref, m_ref, o_ref, acc_ref, car_ref, *, blk, nsub):
    step = pl.program_id(2)
    lane = lax.broadcasted_iota(jnp.int32, (blk, LANES), 1)
    zscale = HEAD_DIM ** -0.5 * LOG2_E
    tri = m_ref[...]
    q_heads = []
    for sub in range(nsub):
        q = q_ref[0, sub * blk:(sub + 1) * blk, :]
        zero = jnp.zeros_like(q)
        q_heads.append((jnp.where(lane < HEAD_DIM, q, zero), jnp.where(lane >= HEAD_DIM, q, zero)))

    def keys(j):
        start = pl.multiple_of(j * blk, blk)
        return k_ref[0, pl.ds(start, blk), :], v_ref[0, pl.ds(start, blk), :]

    def log_terms(z2, causal):
        sp = jnp.log(1.0 + jnp.exp2(-jnp.abs(z2))) * LOG2_E
        log_beta = jnp.minimum(z2, 0.0) - sp
        log_1m = log_beta - z2
        if causal is not None:
            log_1m = jnp.where(causal, log_1m, 0.0)
        suffix = _dot(log_1m.astype(BF16), tri)
        return log_beta, suffix, jnp.sum(log_1m, axis=1, keepdims=True)

    def visit(key_blocks, jobs):
        causal = lax.broadcasted_iota(jnp.int32, (blk, blk), 1) < lax.broadcasted_iota(jnp.int32, (blk, blk), 0)
        kv = [keys(j) for j in key_blocks]
        chains = [(n, b, hd) for n, job in enumerate(jobs) for b in range(len(job[1])) for hd in range(2)]

        def mask(n, b):
            return causal if (jobs[n][2] and b == 0) else None

        z2 = {c: _dot_nt(q_heads[jobs[c[0]][0]][c[2]], kv[jobs[c[0]][1][c[1]]][0]) * zscale for c in chains}
        terms = {c: log_terms(z2[c], mask(c[0], c[1])) for c in chains}
        results = []
        for n, (sub, block_ids, _, car_in) in enumerate(jobs):
            per_head = []
            for hd in range(2):
                car = None if car_in is None else car_in[hd]
                acc = None
                for b in range(len(block_ids)):
                    log_beta, suffix, tot = terms[(n, b, hd)]
                    w = jnp.exp2(log_beta + suffix if car is None else log_beta + suffix + car)
                    if mask(n, b) is not None:
                        w = jnp.where(causal, w, 0.0)
                    pv = _dot(w.astype(BF16), kv[block_ids[b]][1])
                    acc = pv if acc is None else acc + pv
                    car = tot if car is None else car + tot
                per_head.append((acc, car))
            results.append(per_head)
        return results

    def first_visit(first_step):
        first = step * nsub
        key_blocks = [first + sub for sub in range(nsub)] + ([] if first_step else [first - 1])
        jobs = []
        for sub in range(nsub):
            ids = [sub]
            if sub > 0:
                ids.append(sub - 1)
            elif not first_step:
                ids.append(nsub)
            jobs.append((sub, ids, True, None))
        for sub, per_head in enumerate(visit(key_blocks, jobs)):
            for hd, (acc, car) in enumerate(per_head):
                acc_ref[sub, hd] = acc
                car_ref[sub, hd] = car

    pl.when(step == 0)(lambda: first_visit(True))
    pl.when(step > 0)(lambda: first_visit(False))

    def remaining_blocks():
        for sub in range(nsub):
            def mass_left(sub=sub):
                most = jnp.maximum(jnp.max(car_ref[sub, 0]), jnp.max(car_ref[sub, 1]))
                return (most > SB_SKIP_LOG2).astype(jnp.int32)

            def cond(c):
                j, go = c
                return jnp.logical_and(j >= 0, go > 0)

            def body(c, sub=sub, mass_left=mass_left):
                j, _ = c
                (per_head,) = visit([j], [(sub, [0], False, (car_ref[sub, 0], car_ref[sub, 1]))])
                for hd, (acc, car) in enumerate(per_head):
                    acc_ref[sub, hd] += acc
                    car_ref[sub, hd] = car
                return j - 1, mass_left()

            lax.while_loop(cond, body, (step * nsub + sub - 2, mass_left()))

    most = functools.reduce(jnp.maximum, [car_ref[sub, hd] for sub in range(nsub) for hd in range(2)])
    pl.when(jnp.max(most) > SB_SKIP_LOG2)(remaining_blocks)
    for sub in range(nsub):
        o_ref[0, sub * blk:(sub + 1) * blk, :] = jnp.where(lane < HEAD_DIM, acc_ref[sub, 0],
                                                           acc_ref[sub, 1]).astype(o_ref.dtype)


def _sb_attention(qkv, batch, seq, blk=SB_BLOCK, nsub=SB_SUBBLOCKS):
    qkv3 = qkv.reshape(batch, seq, 3 * BRANCH_WIDTH)
    pairs = BRANCH_WIDTH // LANES
    tq = blk * nsub
    assert seq % tq == 0, (seq, tq)
    r = np.arange(blk)
    m_ext = jnp.asarray(r[:, None] > r[None, :], BF16)
    out = pl.pallas_call(
        functools.partial(_sba_kernel, blk=blk, nsub=nsub),
        grid=(batch, pairs, seq // tq),
        in_specs=[pl.BlockSpec((1, tq, LANES), lambda b, p, i: (b, i, p)),
                  pl.BlockSpec((1, seq, LANES), lambda b, p, i: (b, 0, pairs + p)),
                  pl.BlockSpec((1, seq, LANES), lambda b, p, i: (b, 0, 2 * pairs + p)),
                  pl.BlockSpec(m_ext.shape, lambda b, p, i: (0, 0))],
        out_specs=pl.BlockSpec((1, tq, LANES), lambda b, p, i: (b, i, p)),
        out_shape=jax.ShapeDtypeStruct((batch, seq, BRANCH_WIDTH), BF16),
        scratch_shapes=[pltpu.VMEM((nsub, 2, blk, LANES), F32), pltpu.VMEM((nsub, 2, blk, 1), F32)],
        compiler_params=_cparams(("parallel", "parallel", "parallel")),
        name="sb_attention",
    )(qkv3, qkv3, qkv3, m_ext)
    return out.reshape(batch * seq, BRANCH_WIDTH)


def _s5_operators(lam_re, lam_im, log_dt, b_re, b_im, c_re, c_im, d_skip):
    L = S5_CHUNK
    g, p, c = lam_re.shape[0], S5_STATE, S5_GROUP_CH
    dt = jnp.exp(log_dt.astype(F32))[:, None]
    lr = lam_re.astype(F32)
    li = lam_im.astype(F32)

    def a_pow(n):
        n = jnp.asarray(n, F32)[..., None, None]
        mag = jnp.exp(lr * dt * n)
        return mag * jnp.cos(li * dt * n), mag * jnp.sin(li * dt * n)

    ar, ai = a_pow(1.0)
    den = lr * lr + li * li
    nr = ar - 1.0
    fr = (nr * lr + ai * li) / den
    fi = (ai * lr - nr * li) / den
    bbr = fr[..., None] * b_re - fi[..., None] * b_im
    bbi = fr[..., None] * b_im + fi[..., None] * b_re
    cr = c_re.astype(F32)
    ci = c_im.astype(F32)

    pr, pi = a_pow(np.arange(L))
    car = cr[None] * pr[:, :, None, :] - ci[None] * pi[:, :, None, :]
    cai = cr[None] * pi[:, :, None, :] + ci[None] * pr[:, :, None, :]
    klag = jnp.einsum('jgop,gpi->jgoi', car, bbr) - jnp.einsum('jgop,gpi->jgoi', cai, bbi)
    klag = klag.at[0].add(d_skip.astype(F32).reshape(g, c)[:, :, None] * jnp.eye(c, dtype=F32))
    kflat = klag.transpose(1, 3, 0, 2).reshape(g, c, L * c)
    toe = jnp.stack([jnp.pad(kflat[:, :, :(L - s) * c], ((0, 0), (0, 0), (s * c, 0))) for s in range(L)], axis=1)
    toe = toe.reshape(g, L * c, L * c)

    qr, qi = a_pow(L - 1 - np.arange(L))
    p_re = qr[:, :, :, None] * bbr[None] - qi[:, :, :, None] * bbi[None]
    p_im = qr[:, :, :, None] * bbi[None] + qi[:, :, :, None] * bbr[None]
    p_re = p_re.transpose(1, 0, 3, 2).reshape(g, L * c, p)
    p_im = p_im.transpose(1, 0, 3, 2).reshape(g, L * c, p)

    er, ei = a_pow(1 + np.arange(L))
    q_re = (cr[None] * er[:, :, None, :] - ci[None] * ei[:, :, None, :])
    q_im = -(cr[None] * ei[:, :, None, :] + ci[None] * er[:, :, None, :])
    q_re = q_re.transpose(1, 3, 0, 2).reshape(g, p, L * c)
    q_im = q_im.transpose(1, 3, 0, 2).reshape(g, p, L * c)

    a_lr, a_li = a_pow(float(L))

    def pair_diag(m):
        gg, r, cc = m.shape
        m = m.reshape(gg // 2, 2, r, cc)
        z = jnp.zeros_like(m[:, 0])
        top = jnp.concatenate([m[:, 0], z], axis=2)
        bot = jnp.concatenate([z, m[:, 1]], axis=2)
        return jnp.concatenate([top, bot], axis=1)

    toe_p = pair_diag(toe).astype(BF16)
    p_p = jnp.concatenate([pair_diag(p_re), pair_diag(p_im)], axis=2).astype(BF16)
    q_p = jnp.concatenate([pair_diag(q_re), pair_diag(q_im)], axis=1).astype(BF16)
    a_re = a_lr.reshape(1, g * p)
    a_im = a_li.reshape(1, g * p)
    return toe_p, p_p, q_p, a_re, a_im


def _group_block_transpose(tiles):
    group = lax.broadcasted_iota(jnp.int32, tiles[0].shape, 1) // S5_GROUP_CH
    n = len(tiles)
    k = n // 2
    while k >= 1:
        upper = (group & k) != 0
        nxt = list(tiles)
        for i in range(n):
            if i & k == 0:
                lo, hi = tiles[i], tiles[i + k]
                nxt[i] = jnp.where(upper, pltpu.roll(hi, k * S5_GROUP_CH, 1), lo)
                nxt[i + k] = jnp.where(upper, hi, pltpu.roll(lo, LANES - k * S5_GROUP_CH, 1))
        tiles = nxt
        k //= 2
    return tiles


def _s5_state_in_kernel(u_ref, p_ref, up_ref, vre_ref, vim_ref, *, nb):
    L = S5_CHUNK
    per = LANES // S5_GROUP_CH
    for kb in range(u_ref.shape[0]):
        halves = []
        for a in range(L // per):
            steps = [u_ref[kb, pl.ds(a * per + b, nb, stride=L), :] for b in range(per)]
            halves.append(_group_block_transpose(steps))
        for pi in range(per // 2):
            pair = kb * (per // 2) + pi
            row = jnp.concatenate([halves[a][2 * pi + gi] for gi in range(2) for a in range(L // per)], axis=1)
            row = row.astype(BF16)
            up_ref[pair] = row
            v = _dot(row, p_ref[pair])
            vre_ref[:, pair * LANES:(pair + 1) * LANES] = v[:, :LANES]
            vim_ref[:, pair * LANES:(pair + 1) * LANES] = v[:, LANES:]


def _s5_out_kernel(up_ref, sre_ref, sim_ref, toe_ref, q_ref, y_ref, *, nb):
    L = S5_CHUNK
    per = LANES // S5_GROUP_CH
    for kb in range(y_ref.shape[0]):
        halves = [[None] * per for _ in range(L // per)]
        for pi in range(per // 2):
            pair = kb * (per // 2) + pi
            q = q_ref[pair]
            cols = slice(pair * LANES, (pair + 1) * LANES)
            y = _dot(up_ref[pair], toe_ref[pair])
            y += _dot(sre_ref[:, cols].astype(BF16), q[:LANES]) + _dot(sim_ref[:, cols].astype(BF16), q[LANES:])
            y = jax.nn.gelu(y)
            for gi in range(2):
                for a in range(L // per):
                    j = gi * (L // per) + a
                    halves[a][2 * pi + gi] = y[:, j * LANES:(j + 1) * LANES]
        for a in range(L // per):
            steps = _group_block_transpose(halves[a])
            for b in range(per):
                y_ref[kb, pl.ds(a * per + b, nb, stride=L), :] = steps[b]


def _s5_kernel(u_ref, p_ref, toe_ref, q_ref, are_ref, aim_ref, y_ref, up_ref, vre_ref, vim_ref, sre_ref, sim_ref,
               x_ref, *, nb, steps_per_seq):
    @pl.when(pl.program_id(0) % steps_per_seq == 0)
    def _():
        x_ref[...] = jnp.zeros_like(x_ref)

    _s5_state_in_kernel(u_ref, p_ref, up_ref, vre_ref, vim_ref, nb=nb)
    ar = are_ref[...]
    ai = aim_ref[...]

    def step(r, carry):
        xr, xi = carry
        sre_ref[pl.ds(r, 1), :] = xr
        sim_ref[pl.ds(r, 1), :] = xi
        vr = vre_ref[pl.ds(r, 1), :]
        vi = vim_ref[pl.ds(r, 1), :]
        return ar * xr - ai * xi + vr, ar * xi + ai * xr + vi

    xr, xi = lax.fori_loop(0, nb, step, (x_ref[0:1, :], x_ref[1:2, :]))
    x_ref[0:1, :] = xr
    x_ref[1:2, :] = xi
    _s5_out_kernel(up_ref, sre_ref, sim_ref, toe_ref, q_ref, y_ref, nb=nb)


def _s5_ssm(u, batch, seq, ops, layer, nb=128):
    toe_p, p_p, q_p, a_re, a_im = ops
    nslab, t, _ = u.shape
    L = S5_CHUNK
    npair = S5_GROUPS // 2
    nc = t // L
    row = 2 * L * S5_GROUP_CH
    ncols = S5_GROUPS * S5_STATE
    nb = min(nb, seq // L)
    assert seq % (L * nb) == 0, (seq, L, nb)

    slab_spec = pl.BlockSpec((nslab, nb * L, LANES), lambda i: (0, i, 0))
    return pl.pallas_call(
        functools.partial(_s5_kernel, nb=nb, steps_per_seq=seq // (L * nb)),
        grid=(nc // nb,),
        in_specs=[slab_spec] + [_layer_block(a, layer) for a in (p_p, toe_p, q_p, a_re, a_im)],
        out_specs=slab_spec,
        out_shape=jax.ShapeDtypeStruct((nslab, t, LANES), F32),
        scratch_shapes=[pltpu.VMEM((npair, nb, row), BF16)] + [pltpu.VMEM((nb, ncols), F32)] * 4
        + [pltpu.VMEM((8, ncols), F32)],
        compiler_params=_cparams(("arbitrary",)),
        name="s5_ssm",
    )(u, p_p, toe_p, q_p, a_re, a_im)


def _ret_tables(seq):
    half = HEAD_DIM // 2
    pos = jnp.arange(seq, dtype=F32)
    inv = ROPE_BASE ** (-jnp.arange(half, dtype=F32) / half)
    ang = pos[:, None] * inv[None, :]
    cos = jnp.tile(jnp.cos(ang), (1, LANES // half))
    sin = jnp.sin(ang)
    sin = jnp.tile(jnp.concatenate([-sin, sin], axis=1), (1, LANES // HEAD_DIM))
    c = RET_CHUNK
    gamma = 1.0 - 2.0 ** (-5.0 - jnp.arange(RET_HEADS, dtype=F32))
    log_g = jnp.log(gamma)
    idx = jnp.arange(c, dtype=F32)
    diff = idx[:, None] - idx[None, :]
    intra = jnp.where(diff >= 0, jnp.exp(jnp.maximum(diff, 0.0)[None] * log_g[:, None, None]), 0.0)
    q_dec = jnp.exp((idx + 1.0)[None, :] * log_g[:, None])
    k_dec = jnp.exp((c - 1.0 - idx)[None, :] * log_g[:, None])
    ch_dec = jnp.exp(c * log_g)

    def lanes(x):
        x = x.reshape(RET_HEADS // 2, 2, c)
        return jnp.repeat(x.transpose(0, 2, 1), HEAD_DIM, axis=2)

    q_dec_l = lanes(q_dec)
    k_dec_l = lanes(k_dec)
    hp = RET_HEADS // 2
    blockdiag = np.kron(np.eye(2), np.ones((HEAD_DIM, HEAD_DIM)))
    st_dec = jnp.repeat(ch_dec.reshape(hp, 2), HEAD_DIM, axis=1)[:, :, None] * blockdiag[None]
    avg = jnp.asarray(blockdiag / HEAD_DIM, BF16)
    return cos, sin, intra, q_dec_l, k_dec_l, st_dec.astype(F32), avg


def _ret_kernel(q_ref, k_ref, v_ref, g_ref, cos_ref, sin_ref, intra_ref, qd_ref, kd_ref, sd_ref, avg_ref, nw_ref,
                o_ref, st_ref, *, nchunk):
    c = RET_CHUNK

    @pl.when(pl.program_id(2) == 0)
    def _():
        st_ref[...] = jnp.zeros_like(st_ref)

    tq = q_ref.shape[1]
    lane = lax.broadcasted_iota(jnp.int32, (tq, LANES), 1)
    first_half = (lane % HEAD_DIM) < (HEAD_DIM // 2)
    head0 = lax.broadcasted_iota(jnp.int32, (c, LANES), 1) < HEAD_DIM
    sd = sd_ref[0]
    avg = avg_ref[...]
    same_head = avg > 0
    cos = cos_ref[...]
    sin = sin_ref[...]

    def rotary(x):
        swapped = jnp.where(first_half, pltpu.roll(x, LANES - HEAD_DIM // 2, 1), pltpu.roll(x, HEAD_DIM // 2, 1))
        return x * cos + swapped * sin

    q_all = rotary(q_ref[0].astype(F32)).astype(BF16)
    k_all = rotary(k_ref[0].astype(F32)) * (HEAD_DIM ** -0.5)
    kb_all = k_all.astype(BF16)
    zero = jnp.zeros((c, LANES), BF16)
    chunks = [slice(ci * c, (ci + 1) * c) for ci in range(nchunk)]
    vs = [v_ref[0, rows, :] for rows in chunks]
    scores, incs = [], []
    for ci, rows in enumerate(chunks):
        qb = q_all[rows]
        q2 = jnp.concatenate([jnp.where(head0, qb, zero), jnp.where(head0, zero, qb)], axis=0)
        scores.append(_dot_nt(q2, kb_all[rows]))
        kd_t = (k_all[rows] * kd_ref[0]).T.astype(BF16)
        incs.append(_dot(kd_t, vs[ci]))
    state = st_ref[...]
    states = []
    for inc in incs:
        states.append(state.astype(BF16))
        state = state * sd + jnp.where(same_head, inc, 0.0)
    st_ref[...] = state
    intra2 = jnp.concatenate([intra_ref[0, 0], intra_ref[0, 1]], axis=0)
    outs = []
    for ci, rows in enumerate(chunks):
        o2 = _dot((scores[ci] * intra2).astype(BF16), vs[ci])
        outs.append(jnp.where(head0, o2[:c], o2[c:]) + _dot(q_all[rows], states[ci]) * qd_ref[0])
    out = jnp.concatenate(outs, axis=0)
    hi, lo = _split_bf16(out)
    mu = _dot(hi, avg) + _dot(lo, avg)
    dlt = out - mu
    hi, lo = _split_bf16(dlt * dlt)
    var = _dot(hi, avg) + _dot(lo, avg)
    gate = g_ref[0].astype(F32)
    y = dlt * lax.rsqrt(var + EPS) * nw_ref[...] * (gate * jax.nn.sigmoid(gate))
    o_ref[0] = y.astype(o_ref.dtype)


def _retention(ret, norm_w, batch, seq, tables, tq=4096):
    cos, sin, intra, q_dec_l, k_dec_l, st_dec, avg = tables
    pairs = BRANCH_WIDTH // LANES
    ret3 = ret.reshape(batch, seq, 4 * BRANCH_WIDTH)
    tq = min(tq, seq)
    assert seq % tq == 0 and tq % RET_CHUNK == 0, (seq, tq)
    c = RET_CHUNK

    def col(k):
        return pl.BlockSpec((1, tq, LANES), lambda b, p, i: (b, i, k * pairs + p))

    out = pl.pallas_call(
        functools.partial(_ret_kernel, nchunk=tq // c),
        grid=(batch, pairs, seq // tq),
        in_specs=[col(0), col(1), col(2), col(3),
                  pl.BlockSpec((tq, LANES), lambda b, p, i: (i, 0)),
                  pl.BlockSpec((tq, LANES), lambda b, p, i: (i, 0)),
                  pl.BlockSpec((1, 2, c, c), lambda b, p, i: (p, 0, 0, 0)),
                  pl.BlockSpec((1, c, LANES), lambda b, p, i: (p, 0, 0)),
                  pl.BlockSpec((1, c, LANES), lambda b, p, i: (p, 0, 0)),
                  pl.BlockSpec((1, LANES, LANES), lambda b, p, i: (p, 0, 0)),
                  pl.BlockSpec((LANES, LANES), lambda b, p, i: (0, 0)),
                  pl.BlockSpec((1, LANES), lambda b, p, i: (0, p))],
        out_specs=pl.BlockSpec((1, tq, LANES), lambda b, p, i: (b, i, p)),
        out_shape=jax.ShapeDtypeStruct((batch, seq, BRANCH_WIDTH), BF16),
        scratch_shapes=[pltpu.VMEM((LANES, LANES), F32)],
        compiler_params=_cparams(("parallel", "parallel", "arbitrary")),
        name="retention",
    )(ret3, ret3, ret3, ret3, cos, sin, intra.reshape(pairs, 2, c, c), q_dec_l, k_dec_l, st_dec, avg,
      norm_w.reshape(1, BRANCH_WIDTH).astype(F32))
    return out.reshape(batch * seq, BRANCH_WIDTH)


def _route(logits):
    lane = lax.broadcasted_iota(jnp.int32, logits.shape, 1).astype(F32)
    neg = jnp.float32(-jnp.inf)
    big = jnp.float32(1 << 20)
    gl = jnp.where(lane < N_GROUPS, logits, neg)
    gmax = jnp.max(gl, axis=1, keepdims=True)
    gsum = jnp.sum(jnp.exp(gl - gmax), axis=1, keepdims=True)
    g_w = 1.0 / gsum
    g_idx = jnp.min(jnp.where(gl == gmax, lane, big), axis=1, keepdims=True)
    lo = ROUTER_LANE0 + EXPERTS_PER_GROUP * g_idx
    sl = jnp.where((lane >= lo) & (lane < lo + EXPERTS_PER_GROUP), logits, neg)
    m1 = jnp.max(sl, axis=1, keepdims=True)
    i1 = jnp.min(jnp.where(sl == m1, lane, big), axis=1, keepdims=True)
    sl2 = jnp.where(lane == i1, neg, sl)
    m2 = jnp.max(sl2, axis=1, keepdims=True)
    i2 = jnp.min(jnp.where(sl2 == m2, lane, big), axis=1, keepdims=True)
    e2 = jnp.exp(m2 - m1)
    w1 = g_w / (1.0 + e2)
    w2 = g_w * e2 / (1.0 + e2)
    packed = jnp.where(lane == 0.0, i1 - ROUTER_LANE0, jnp.where(lane == 1.0, i2 - ROUTER_LANE0, 0.0))
    return packed + jnp.where(lane == 2.0, w1, 0.0) + jnp.where(lane == 3.0, w2, 0.0)


def _merge_kernel(ysb_ref, ys5_ref, yret_ref, gate_ref, h_ref, wglu_ref, wbr_ref, wout_ref, n2_ref, wr_hi_ref,
                  wr_lo_ref, br_ref, tri_ref, cum_ref, h_out_ref, hn_ref, route_ref, pos_ref, cnt_ref):
    ys5 = jnp.concatenate([ys5_ref[s] for s in range(ys5_ref.shape[0])], axis=1).astype(BF16)
    glu = _dot(ys5, wglu_ref[...])
    y_s5 = (glu[:, :BRANCH_WIDTH] * jax.nn.sigmoid(glu[:, BRANCH_WIDTH:])).astype(BF16)
    merged = None
    for n, y in enumerate((ysb_ref[...], y_s5, yret_ref[...])):
        g = jax.nn.sigmoid(gate_ref[:, n * D_MODEL:(n + 1) * D_MODEL].astype(F32))
        term = g * _dot(y, wbr_ref[n])
        merged = term if merged is None else merged + term
    h = h_ref[...] + _dot(merged.astype(BF16), wout_ref[...])
    h_out_ref[...] = h
    ms = jnp.mean(h * h, axis=-1, keepdims=True)
    hn = h * lax.rsqrt(ms + EPS) * n2_ref[...]
    hn_ref[...] = hn.astype(BF16)
    hi, lo = _split_bf16(hn)
    logits = _dot(hi, wr_hi_ref[...]) + _dot(hi, wr_lo_ref[...]) + _dot(lo, wr_hi_ref[...]) + br_ref[...]
    route = _route(logits)
    route_ref[...] = route
    pos_ref[...], cnt_ref[0] = _moe_plan(route, tri_ref[...], cum_ref[...])


def _merge(y_sb, y_s5, y_ret, gates, h, w_glu, w_br, w_out, norm2_w, wr_hi, wr_lo, b_r, layer):
    t = h.shape[0]
    tm = min(MOE_TILE, t)
    assert t % tm == 0, (t, tm)
    r = np.arange(tm)
    tri = jnp.asarray(r[None, :] <= r[:, None], BF16)
    l = np.arange(LANES)
    cum = jnp.asarray(l[:, None] < l[None, :], BF16)

    def rows(w):
        return pl.BlockSpec((tm, w), lambda i: (i, 0))

    return pl.pallas_call(
        _merge_kernel,
        grid=(t // tm,),
        in_specs=[rows(BRANCH_WIDTH), pl.BlockSpec((y_s5.shape[0], tm, LANES), lambda i: (0, i, 0)),
                  rows(BRANCH_WIDTH), rows(N_BRANCH * D_MODEL),
                  rows(D_MODEL)] + [_layer_block(a, layer) for a in (w_glu, w_br, w_out, norm2_w, wr_hi, wr_lo, b_r)]
        + [pl.BlockSpec(tri.shape, lambda i: (0, 0), pipeline_mode=pl.Buffered(1)),
           pl.BlockSpec(cum.shape, lambda i: (0, 0), pipeline_mode=pl.Buffered(1))],
        out_specs=[rows(D_MODEL), rows(D_MODEL), rows(LANES), rows(LANES), pl.BlockSpec((1, 8, LANES), lambda i: (i, 0, 0))],
        out_shape=[jax.ShapeDtypeStruct((t, D_MODEL), F32), jax.ShapeDtypeStruct((t, D_MODEL), BF16),
                   jax.ShapeDtypeStruct((t, LANES), F32), jax.ShapeDtypeStruct((t, LANES), F32),
                   jax.ShapeDtypeStruct((t // tm, 8, LANES), F32)],
        compiler_params=_cparams(("parallel",)),
        name="merge_route",
    )(y_sb, y_s5, y_ret, gates, h, w_glu, w_br, w_out, norm2_w, wr_hi, wr_lo, b_r, tri, cum)


MOE_TILE = 512
MOE_ROW_TILE = 512
SEG_ALIGN = 16
MOE_SLOTS = TOP_K * MOE_TILE + N_EXPERTS * SEG_ALIGN
MOE_CHUNKS = MOE_SLOTS // SEG_ALIGN
MOE_SLOT_BLOCK = 256


def _moe_plan(route, tri, cum):
    tm = route.shape[0]
    lane = lax.broadcasted_iota(jnp.int32, route.shape, 1).astype(F32)
    e1 = route[:, 0:1]
    e2 = route[:, 1:2]
    chosen = (lane == e1) | (lane == e2)
    onehot = jnp.where(chosen, 1.0, 0.0).astype(BF16)
    incl = _dot(tri, onehot)
    cnt = incl[tm - 1:tm, :]
    padded = jnp.floor((cnt + (SEG_ALIGN - 1)) * (1.0 / SEG_ALIGN)) * SEG_ALIGN
    start = _dot(jnp.broadcast_to(padded, (8, LANES)).astype(BF16), cum)[0:1, :]
    slot = start + incl - 1.0
    p1 = jnp.sum(jnp.where(lane == e1, slot, 0.0), axis=1, keepdims=True)
    p2 = jnp.sum(jnp.where(lane == e2, slot, 0.0), axis=1, keepdims=True)
    pos = jnp.where(lane == 0.0, p1, jnp.where(lane == 1.0, p2, 0.0))
    return pos, jnp.broadcast_to(cnt, (8, LANES))


def _moe_tables(cnt, rt, n_rows):
    a = SEG_ALIGN
    ntile = cnt.shape[0]
    p = (cnt + a - 1) // a * a
    lend = jnp.cumsum(p, axis=1)
    lstart = lend - p
    seg = p.sum(axis=0)
    segpad = (seg + rt - 1) // rt * rt
    gend = jnp.cumsum(segpad)
    gstart = gend - segpad
    toff = gstart[None, :] + jnp.cumsum(p, axis=0) - p
    cstart = a * jnp.arange(MOE_CHUNKS, dtype=jnp.int32)
    in_run = (cstart[None, :, None] >= lstart[:, None, :]) & (cstart[None, :, None] < lend[:, None, :])
    gdst = (jnp.sum(jnp.where(in_run, (toff - lstart)[:, None, :], 0), axis=2) + cstart[None, :]) // a
    nch = lend[:, -1] // a
    tstart = rt * jnp.arange(n_rows // rt, dtype=jnp.int32)
    tile_e = jnp.minimum(jnp.sum(tstart[:, None] >= gend[None, :], axis=1), N_EXPERTS - 1)
    n_used = gend[-1] // rt
    zmax = rt // a - 1
    zper = -(-N_EXPERTS * zmax // ntile)
    k = jnp.arange(zmax, dtype=jnp.int32)
    zd = ((gstart + seg) // a)[:, None] + k[None, :]
    valid = (k[None, :] < ((segpad - seg) // a)[:, None]).reshape(-1)
    order = jnp.argsort(jnp.logical_not(valid), stable=True)
    zdst = jnp.pad(zd.reshape(-1)[order], (0, ntile * zper - N_EXPERTS * zmax))
    nz = valid.sum()
    i32 = jnp.int32
    return (gdst.reshape(-1).astype(i32), nch.astype(i32), zdst.astype(i32), nz.reshape(1).astype(i32),
            tile_e.astype(i32), n_used.reshape(1).astype(i32)), zper


def _moe_dispatch_kernel(gdst_ref, nch_ref, zdst_ref, nz_ref, n_used_ref, pos_ref, hn_ref, xs_hbm, stg, zblk, sem,
                         zsem, *, zper):
    i = pl.program_id(0)
    last_step = pl.num_programs(0) - 1
    buf = i % 2
    tm = hn_ref.shape[0]
    pos_t = pos_ref[...].T
    p1 = pos_t[0:1, :]
    p2 = pos_t[1:2, :]
    x = hn_ref[...]
    for lo in range(0, MOE_SLOTS, tm):
        slot = (lax.broadcasted_iota(jnp.int32, (tm, tm), 0) + lo).astype(F32)
        onehot = jnp.where((slot == p1) | (slot == p2), 1.0, 0.0).astype(BF16)
        stg[buf, lo:lo + tm, :] = _dot(onehot, x).astype(BF16)

    def chunk_copy(which, src_chunk, dst_chunk):
        src = pl.multiple_of(src_chunk * SEG_ALIGN, SEG_ALIGN)
        dst = pl.multiple_of(dst_chunk * SEG_ALIGN, SEG_ALIGN)
        return pltpu.make_async_copy(stg.at[which, pl.ds(src, SEG_ALIGN)], xs_hbm.at[pl.ds(dst, SEG_ALIGN)],
                                     sem.at[which])

    def n_copies(step):
        return nch_ref[step], jnp.clip(nz_ref[0] - step * zper, 0, zper)

    def drain(which, step):
        for count in n_copies(step):
            rows = count * SEG_ALIGN

            @pl.when(rows > 0)
            def _():
                pltpu.make_async_copy(stg.at[which, pl.ds(0, rows)], xs_hbm.at[pl.ds(0, rows)],
                                      sem.at[which]).wait()

    n, nzero = n_copies(i)

    @pl.loop(0, n)
    def _(c):
        chunk_copy(buf, c, gdst_ref[i * MOE_CHUNKS + c]).start()

    @pl.loop(0, nzero)
    def _(k):
        chunk_copy(buf, MOE_CHUNKS - 1, zdst_ref[i * zper + k]).start()

    pl.when(i > 0)(lambda: drain(1 - buf, i - 1))

    @pl.when(i == last_step)
    def _():
        drain(buf, i)
        rt = zblk.shape[0]
        zblk[...] = jnp.zeros_like(zblk)
        first = n_used_ref[0]
        last = xs_hbm.shape[0] // rt

        def tile_copy(j):
            return pltpu.make_async_copy(zblk, xs_hbm.at[pl.ds(pl.multiple_of(j * rt, rt), rt)], zsem)

        @pl.loop(first, last)
        def _(j):
            tile_copy(j).start()

        @pl.loop(first, last)
        def _(j):
            tile_copy(0).wait()


def _moe_expert_kernel(tile_e_ref, n_used_ref, xs_ref, wg_ref, wu_ref, wd_ref, ys_ref):
    used = pl.program_id(0) < n_used_ref[0]

    @pl.when(used)
    def _():
        x = xs_ref[...]
        gate = _dot(x, wg_ref[0].astype(BF16))
        up = _dot(x, wu_ref[0].astype(BF16))
        hdn = gate * jax.nn.sigmoid(gate) * up
        ys_ref[...] = _dot(hdn.astype(BF16), wd_ref[0].astype(BF16)).astype(ys_ref.dtype)

    @pl.when(jnp.logical_not(used))
    def _():
        ys_ref[...] = jnp.zeros_like(ys_ref)


def _moe_combine_kernel(gdst_ref, nch_ref, route_ref, pos_ref, h_ref, nw_ref, ys_hbm, o_ref, stg, sem, *, out_norm):
    i = pl.program_id(0)
    nsteps = pl.num_programs(0)
    buf = i % 2
    tm = h_ref.shape[0]

    def chunk_copy(which, src_chunk, dst_chunk):
        src = pl.multiple_of(src_chunk * SEG_ALIGN, SEG_ALIGN)
        dst = pl.multiple_of(dst_chunk * SEG_ALIGN, SEG_ALIGN)
        return pltpu.make_async_copy(ys_hbm.at[pl.ds(src, SEG_ALIGN)], stg.at[which, pl.ds(dst, SEG_ALIGN)],
                                     sem.at[which])

    def fetch(step, which):
        n = nch_ref[step]

        @pl.loop(0, n)
        def _(c):
            chunk_copy(which, gdst_ref[step * MOE_CHUNKS + c], c).start()

        per = MOE_SLOT_BLOCK // SEG_ALIGN
        stop = jnp.minimum((n + per - 1) // per * per, MOE_CHUNKS)

        @pl.loop(n, stop)
        def _(c):
            rows = pl.ds(pl.multiple_of(c * SEG_ALIGN, SEG_ALIGN), SEG_ALIGN)
            stg[which, rows, :] = jnp.zeros((SEG_ALIGN, D_MODEL), BF16)

    pl.when(i == 0)(lambda: fetch(0, 0))
    pl.when(i + 1 < nsteps)(lambda: fetch(i + 1, 1 - buf))

    r = route_ref[...]
    pos = pos_ref[...]

    def scatter(lo, hi):
        slot = (lax.broadcasted_iota(jnp.int32, (tm, hi - lo), 1) + lo).astype(F32)
        w = jnp.where(slot == pos[:, 0:1], r[:, 2:3], 0.0) + jnp.where(slot == pos[:, 1:2], r[:, 3:4], 0.0)
        return _dot(w.astype(BF16), stg[buf, lo:hi, :])

    rows = nch_ref[i] * SEG_ALIGN

    @pl.when(rows > 0)
    def _():
        pltpu.make_async_copy(ys_hbm.at[pl.ds(0, rows)], stg.at[buf, pl.ds(0, rows)], sem.at[buf]).wait()

    always = TOP_K * tm
    o_ref[...] = h_ref[...] + scatter(0, always)
    for lo in range(always, MOE_SLOTS, MOE_SLOT_BLOCK):
        @pl.when(lo < rows)
        def _(lo=lo):
            o_ref[...] += scatter(lo, lo + MOE_SLOT_BLOCK)
    if out_norm:
        out = o_ref[...]
        ms = jnp.mean(out * out, axis=-1, keepdims=True)
        o_ref[...] = out * lax.rsqrt(ms + EPS) * nw_ref[...]


def _moe(hn, route, pos, cnt, h, w_gate, w_up, w_down, layer, out_norm_w, out_norm):
    t = h.shape[0]
    tm = min(MOE_TILE, t)
    rt = MOE_ROW_TILE
    assert t % tm == 0 and MOE_SLOTS % tm == 0, (t, tm)
    ntile = t // tm
    n_rows = TOP_K * t + ntile * N_EXPERTS * (SEG_ALIGN - 1) + N_EXPERTS * (rt - SEG_ALIGN)
    n_rows = (n_rows + rt - 1) // rt * rt
    (gdst, nch, zdst, nz, tile_e, n_used), zper = _moe_tables(cnt[:, 0, :N_EXPERTS].astype(jnp.int32), rt, n_rows)

    def expert_block(rows, cols):
        return pl.BlockSpec((None, 1, rows, cols), lambda j, te, nu: (layer, te[j], 0, 0))

    xs = pl.pallas_call(
        functools.partial(_moe_dispatch_kernel, zper=zper),
        grid_spec=pltpu.PrefetchScalarGridSpec(
            num_scalar_prefetch=5,
            grid=(ntile,),
            in_specs=[pl.BlockSpec((tm, LANES), lambda i, *_: (i, 0)),
                      pl.BlockSpec((tm, D_MODEL), lambda i, *_: (i, 0))],
            out_specs=pl.BlockSpec(memory_space=pl.ANY),
            scratch_shapes=[pltpu.VMEM((2, MOE_SLOTS, D_MODEL), BF16), pltpu.VMEM((rt, D_MODEL), BF16),
                            pltpu.SemaphoreType.DMA((2,)), pltpu.SemaphoreType.DMA(())]),
        out_shape=jax.ShapeDtypeStruct((n_rows, D_MODEL), BF16),
        compiler_params=_cparams(("arbitrary",)),
        name="moe_dispatch",
    )(gdst, nch, zdst, nz, n_used, pos, hn)
    ys = pl.pallas_call(
        _moe_expert_kernel,
        grid_spec=pltpu.PrefetchScalarGridSpec(
            num_scalar_prefetch=2,
            grid=(n_rows // rt,),
            in_specs=[pl.BlockSpec((rt, D_MODEL), lambda j, te, nu: (jnp.minimum(j, nu[0] - 1), 0)),
                      expert_block(D_MODEL, EXPERT_FF), expert_block(D_MODEL, EXPERT_FF),
                      expert_block(EXPERT_FF, D_MODEL)],
            out_specs=pl.BlockSpec((rt, D_MODEL), lambda j, te, nu: (j, 0))),
        out_shape=jax.ShapeDtypeStruct((n_rows, D_MODEL), BF16),
        compiler_params=_cparams(("arbitrary",)),
        name="moe_experts",
    )(tile_e, n_used, xs, w_gate, w_up, w_down)
    return pl.pallas_call(
        functools.partial(_moe_combine_kernel, out_norm=out_norm),
        grid_spec=pltpu.PrefetchScalarGridSpec(
            num_scalar_prefetch=2,
            grid=(ntile,),
            in_specs=[pl.BlockSpec((tm, LANES), lambda i, *_: (i, 0)),
                      pl.BlockSpec((tm, LANES), lambda i, *_: (i, 0)),
                      pl.BlockSpec((tm, D_MODEL), lambda i, *_: (i, 0)),
                      pl.BlockSpec((1, D_MODEL), lambda i, *_: (0, 0)),
                      pl.BlockSpec(memory_space=pl.ANY)],
            out_specs=pl.BlockSpec((tm, D_MODEL), lambda i, *_: (i, 0)),
            scratch_shapes=[pltpu.VMEM((2, MOE_SLOTS, D_MODEL), BF16), pltpu.SemaphoreType.DMA((2,))]),
        out_shape=jax.ShapeDtypeStruct((t, D_MODEL), F32),
        compiler_params=_cparams(("arbitrary",)),
        name="moe_combine",
    )(gdst, nch, route, pos, h, out_norm_w, ys)


def kernel(x, norm1_w, w_in, s5_lambda_re, s5_lambda_im, s5_log_dt, s5_b_re, s5_b_im, s5_c_re, s5_c_im, s5_d,
           s5_w_glu, ret_norm_w, w_branch, w_out, norm2_w, w_group, b_group, w_router, b_router, w_gate, w_up,
           w_down, final_norm_w):
    batch, seq, _ = x.shape
    depth = w_in.shape[0]
    t = batch * seq
    h = x.reshape(t, D_MODEL).astype(F32)
    ret_tables = _ret_tables(seq)
    w_in, s5_w_glu, w_branch, w_out = (w.astype(BF16) for w in (w_in, s5_w_glu, w_branch, w_out))
    norm1_w = norm1_w.reshape(depth, 1, D_MODEL).astype(F32)
    norm2_w = norm2_w.reshape(depth, 1, D_MODEL).astype(F32)
    s5_params = (s5_lambda_re, s5_lambda_im, s5_log_dt, s5_b_re, s5_b_im, s5_c_re, s5_c_im)
    s5_flat = [a.reshape((depth * S5_GROUPS,) + a.shape[2:]) for a in s5_params] + [s5_d.reshape(-1)]
    toe_p, p_p, q_p, a_re, a_im = _s5_operators(*s5_flat)
    s5_ops = [op.reshape((depth, -1) + op.shape[1:]) for op in (toe_p, p_p, q_p)]
    s5_ops += [a.reshape(depth, 1, -1) for a in (a_re, a_im)]
    w_r = jnp.concatenate([w_group, w_router.transpose(0, 2, 1, 3).reshape(depth, D_MODEL, N_EXPERTS)], axis=2)
    w_r = jnp.pad(w_r.astype(F32), ((0, 0), (0, 0), (0, LANES - w_r.shape[2])))
    wr_hi, wr_lo = _split_bf16(w_r)
    b_r = jnp.pad(jnp.concatenate([b_group, b_router.reshape(depth, -1)], axis=1).astype(F32),
                  ((0, 0), (0, LANES - N_GROUPS - N_EXPERTS))).reshape(depth, 1, LANES)
    for layer in range(depth):
        qkv, u_s5, ret, gates = _inproj(h, norm1_w, w_in, layer)
        y_sb = _sb_attention(qkv, batch, seq)
        y_s5 = _s5_ssm(u_s5, batch, seq, s5_ops, layer)
        y_ret = _retention(ret, ret_norm_w[layer], batch, seq, ret_tables)
        h, hn, route, pos, cnt = _merge(y_sb, y_s5, y_ret, gates, h, s5_w_glu, w_branch, w_out, norm2_w, wr_hi, wr_lo, b_r,
                              layer)
        h = _moe(hn, route, pos, cnt, h, w_gate, w_up, w_down, layer, final_norm_w.reshape(1, D_MODEL).astype(F32),
                 out_norm=(layer == depth - 1))
    return h.reshape(batch, seq, D_MODEL).astype(x.dtype)
```
